```python
import math
import jax
import jax.numpy as jnp
from jax import lax
import numpy as np

D_MODEL = 2048
BATCH = 16
SEQ = 2048
DEPTH = 4

GRID_W = 64
CTX_LEN = 256

RWKV_HEADS = 12
RWKV_HEAD = 64
RWKV_W = RWKV_HEADS * RWKV_HEAD
DECAY_LORA = 64
AAA_LORA = 64
GATE_LORA = 128
RWKV_GN_EPS = 64e-5

DIFF_HEADS = 6
DIFF_DK = 64
DIFF_DV = 2 * DIFF_DK
DIFF_W = DIFF_HEADS * DIFF_DV
DIFF_SCALE = DIFF_DK ** -0.5

WIN_Q_HEADS = 8
WIN_KV_HEADS = 2
WIN_GROUP = WIN_Q_HEADS // WIN_KV_HEADS
WIN_HEAD = 64
WIN_W = WIN_Q_HEADS * WIN_HEAD
WIN_KV_W = WIN_KV_HEADS * WIN_HEAD
WIN_SCALE = WIN_HEAD ** -0.5
WINDOW = 128
QBLK = WINDOW

MIX_W = RWKV_W + DIFF_W + WIN_W
N_BRANCH = 3

ROPE_BASE = 10000.0
ROPE_AX_FREQS = 16

RWKV_IN = 3 * RWKV_W + 2 * DECAY_LORA + 2 * AAA_LORA + GATE_LORA
DIFF_IN = 3 * DIFF_W
WIN_IN = WIN_W + 2 * WIN_KV_W
DIFF_OFF = RWKV_IN
WIN_OFF = DIFF_OFF + DIFF_IN
GATE_OFF = WIN_OFF + WIN_IN
N_IN = GATE_OFF + N_BRANCH * D_MODEL

N_GROUPS = 4
EXPERTS_PER_GROUP = 8
N_EXPERTS = N_GROUPS * EXPERTS_PER_GROUP
EXPERT_TOP_K = 2
D_EXPERT = D_MODEL // 4
MOE_BLK = 256

DN_ALPHA = (2 * DEPTH) ** 0.25
DN_BETA = (8 * DEPTH) ** -0.25
ADA_EPS = 1e-6
LN_EPS = 1e-5
NEG_INF = -1e30

kernel_name = "hybrid_rwkv7_diffattn_swa_hmoe_dit"


def layer_norm(x, g, b, eps):
    xf = x.astype(jnp.float32)
    mu = jnp.mean(xf, axis=-1, keepdims=True)
    var = jnp.mean(jnp.square(xf - mu), axis=-1, keepdims=True)
    y = (xf - mu) * lax.rsqrt(var + eps)
    if g is not None:
        y = y * g.astype(jnp.float32) + b.astype(jnp.float32)
    return y.astype(x.dtype)


def modulate(x, shift, scale):
    return layer_norm(x, None, None, ADA_EPS) * (1 + scale) + shift


def axial_rope(x, rope):
    cos, sin = rope
    xs = x.reshape(x.shape[:-1] + (2, 2, ROPE_AX_FREQS)).astype(jnp.float32)
    x1 = xs[..., 0, :]
    x2 = xs[..., 1, :]
    out = jnp.stack([x1 * cos - x2 * sin, x1 * sin + x2 * cos], axis=-2)
    return out.reshape(x.shape).astype(x.dtype)


def centred_shift(p, mu_prev, mu_next):
    zero = jnp.zeros_like(p[:, :1])
    prev = jnp.concatenate([zero, p[:, :-1]], axis=1)
    nxt = jnp.concatenate([p[:, 1:], zero], axis=1)
    return p + mu_prev * (prev - p) + mu_next * (nxt - p)


def rwkv_features(p, mu, w0, w_up, a0, a_up, g_up, kvec):
    B, L, _ = p.shape
    p = centred_shift(p, mu[0], mu[1])
    r = p[..., :RWKV_W]
    k = p[..., RWKV_W:2 * RWKV_W]
    v = p[..., 2 * RWKV_W:3 * RWKV_W]
    o = 3 * RWKV_W
    wd = p[..., o:o + 2 * DECAY_LORA].reshape(B, L, 2, DECAY_LORA)
    o += 2 * DECAY_LORA
    ad = p[..., o:o + 2 * AAA_LORA].reshape(B, L, 2, AAA_LORA)
    o += 2 * AAA_LORA
    gd = p[..., o:o + GATE_LORA]
    w_log = (w0 + jnp.einsum('bldr,drc->bldc', jnp.tanh(wd), w_up)).astype(jnp.float32)
    decay = jnp.exp(-jnp.exp(-jax.nn.softplus(-w_log) - 0.5))
    a = jax.nn.sigmoid((a0 + jnp.einsum('bldr,drc->bldc', ad, a_up)).astype(jnp.float32))
    g = jax.nn.sigmoid(gd) @ g_up
    kv = kvec.astype(jnp.float32)
    k_k, k_a = kv[0], kv[1]
    kf = k.astype(jnp.float32)
    kk = (kf * k_k).reshape(B, L, RWKV_HEADS, RWKV_HEAD)
    kk = kk / jnp.maximum(jnp.sqrt(jnp.sum(kk * kk, axis=-1, keepdims=True)), 1e-12)
    kk_flat = kk.reshape(B, L, RWKV_W)
    k_dir = kf[:, :, None] * (1 + (a - 1) * k_a)
    b_dir = kk_flat[:, :, None] * a
    heads = lambda t: t.reshape(t.shape[:-1] + (RWKV_HEADS, RWKV_HEAD))
    return (heads(r.astype(jnp.float32)), heads(v.astype(jnp.float32)), g,
            heads(decay), kk, heads(k_dir), heads(b_dir))


def rwkv_scan(s0, r, decay, kk, b, k, v, reverse):
    xs = tuple(jnp.moveaxis(t, 1, 0) for t in (r, decay, kk, b, k, v))

    def step(s, inp):
        r_t, w_t, kk_t, b_t, k_t, v_t = inp
        sa = jnp.einsum('bhij,bhj->bhi', s, -kk_t)
        s = (s * w_t[:, :, None, :] + sa[..., None] * b_t[:, :, None, :]
             + v_t[..., None] * k_t[:, :, None, :])
        return s, jnp.einsum('bhij,bhj->bhi', s, r_t)

    s_fin, ys = lax.scan(step, s0, xs, reverse=reverse)
    return s_fin, jnp.moveaxis(ys, 0, 1)


def rwkv_readout(y, r, k_dir, v, g, r_k, lnx):
    B, L, H, N = y.shape
    mu = jnp.mean(y, axis=-1, keepdims=True)
    var = jnp.mean(jnp.square(y - mu), axis=-1, keepdims=True)
    lf = lnx.astype(jnp.float32)
    yn = (y - mu) * lax.rsqrt(var + RWKV_GN_EPS) * lf[0].reshape(H, N) + lf[1].reshape(H, N)
    rk = r_k.astype(jnp.float32).reshape(H, N)
    bonus = jnp.sum(r[:, :, None] * k_dir * rk, axis=(2, 4))[..., None]
    out = (yn + bonus * v).reshape(B, L, H * N) * g.astype(jnp.float32)
    return out.astype(g.dtype)


def rwkv_bidir(f_l, f_c, r_k, lnx, with_ctx):
    r_l, v_l, g_l, dec_l, kk_l, kd_l, bd_l = f_l
    r_c, v_c, g_c, dec_c, kk_c, kd_c, bd_c = f_c
    B, _, H, N = r_l.shape
    s0 = jnp.zeros((B, H, N, N), jnp.float32)
    ys_l, ys_c = [], []
    for d in range(2):
        rev = d == 1
        s_ctx, yc = rwkv_scan(s0, r_c, dec_c[:, :, d], kk_c, bd_c[:, :, d], kd_c[:, :, d], v_c, rev)
        _, yl = rwkv_scan(s_ctx, r_l, dec_l[:, :, d], kk_l, bd_l[:, :, d], kd_l[:, :, d], v_l, rev)
        ys_l.append(yl)
        ys_c.append(yc)
    out_l = rwkv_readout(ys_l[0] + ys_l[1], r_l, kd_l, v_l, g_l, r_k, lnx)
    out_c = rwkv_readout(ys_c[0] + ys_c[1], r_c, kd_c, v_c, g_c, r_k, lnx) if with_ctx else None
    return out_l, out_c


def diff_attention(p_l, p_c, lam_vecs, subln_g, lam_init, rope, with_ctx):
    def split(p):
        B, L, _ = p.shape
        q = p[..., :DIFF_W].reshape(B, L, DIFF_HEADS, 2, DIFF_DK).transpose(0, 2, 3, 1, 4)
        k = p[..., DIFF_W:2 * DIFF_W].reshape(B, L, DIFF_HEADS, 2, DIFF_DK).transpose(0, 2, 3, 1, 4)
        v = p[..., 2 * DIFF_W:].reshape(B, L, DIFF_HEADS, DIFF_DV).transpose(0, 2, 1, 3)
        return q, k, v

    q_l, k_l, v_l = split(p_l)
    q_c, k_c, v_c = split(p_c)
    q_l = axial_rope(q_l, rope)
    k_l = axial_rope(k_l, rope)
    lf = lam_vecs.astype(jnp.float32)
    lam = jnp.exp(jnp.sum(lf[0] * lf[1])) - jnp.exp(jnp.sum(lf[2] * lf[3])) + lam_init

    def attend(q, keys, vals):
        s = jnp.einsum('bhmqd,bhmkd->bhmqk', q, keys).astype(jnp.float32) * DIFF_SCALE
        pr = jax.nn.softmax(s, axis=-1)
        a = pr[:, :, 0] - lam * pr[:, :, 1]
        return jnp.einsum('bhqk,bhkd->bhqd', a.astype(vals.dtype), vals)

    def finish(o):
        B, H, L, _ = o.shape
        of = o.astype(jnp.float32)
        of = of * lax.rsqrt(jnp.mean(of * of, axis=-1, keepdims=True) + 1e-5)
        of = of * subln_g.astype(jnp.float32) * (1 - lam_init)
        return of.transpose(0, 2, 1, 3).reshape(B, L, H * DIFF_DV).astype(o.dtype)

    keys = jnp.concatenate([k_c, k_l], axis=3)
    vals = jnp.concatenate([v_c, v_l], axis=2)
    B, H, _, S, dk = q_l.shape
    nq = S // QBLK
    qb = q_l.reshape(B, H, 2, nq, QBLK, dk).transpose(3, 0, 1, 2, 4, 5)
    o_l = lax.map(lambda q: attend(q, keys, vals), qb)
    o_l = o_l.transpose(1, 2, 0, 3, 4).reshape(B, H, S, DIFF_DV)
    out_c = finish(attend(q_c, k_c, v_c)) if with_ctx else None
    return finish(o_l), out_c


def sink_attend(q, keys, vals, sink_hg, mask):
    s = jnp.einsum('bhgqd,bhkd->bhgqk', q, keys).astype(jnp.float32) * WIN_SCALE
    if mask is not None:
        s = jnp.where(mask, s, NEG_INF)
    sink = jnp.broadcast_to(sink_hg[None, :, :, None, None], s.shape[:-1] + (1,))
    pr = jax.nn.softmax(jnp.concatenate([s, sink], axis=-1), axis=-1)[..., :-1]
    return jnp.einsum('bhgqk,bhkd->bhgqd', pr.astype(vals.dtype), vals)


def window_attention(p_l, p_c, sink, rope, with_ctx):
    def split(p):
        B, L, _ = p.shape
        q = p[..., :WIN_W].reshape(B, L, WIN_KV_HEADS, WIN_GROUP, WIN_HEAD).transpose(0, 2, 3, 1, 4)
        k = p[..., WIN_W:WIN_W + WIN_KV_W].reshape(B, L, WIN_KV_HEADS, WIN_HEAD).transpose(0, 2, 1, 3)
        v = p[..., WIN_W + WIN_KV_W:].reshape(B, L, WIN_KV_HEADS, WIN_HEAD).transpose(0, 2, 1, 3)
        return q, k, v

    q_l, k_l, v_l = split(p_l)
    q_c, k_c, v_c = split(p_c)
    q_l = axial_rope(q_l, rope)
    k_l = axial_rope(k_l, rope)
    sink_hg = sink.astype(jnp.float32).reshape(WIN_KV_HEADS, WIN_GROUP)
    B, Hkv, G, S, dh = q_l.shape
    C = k_c.shape[2]
    nb = S // QBLK
    qb = jnp.moveaxis(q_l.reshape(B, Hkv, G, nb, QBLK, dh), 3, 0)

    def band(t):
        tp = jnp.pad(t, ((0, 0), (0, 0), (QBLK, QBLK), (0, 0))).reshape(B, Hkv, nb + 2, QBLK, dh)
        tw = jnp.concatenate([tp[:, :, :-2], tp[:, :, 1:-1], tp[:, :, 2:]], axis=3)
        return jnp.moveaxis(tw, 2, 0)

    blk_ids = jnp.arange(nb)[:, None, None] * QBLK
    qpos = blk_ids + jnp.arange(QBLK)[None, :, None]
    kpos = blk_ids - QBLK + jnp.arange(3 * QBLK)[None, None, :]
    valid = (jnp.abs(qpos - kpos) <= WINDOW) & (kpos >= 0) & (kpos < S)
    valid = jnp.concatenate([valid, jnp.ones((nb, QBLK, C), bool)], axis=-1)

    def blk(args):
        q, kw, vw, m = args
        return sink_attend(q, jnp.concatenate([kw, k_c], axis=2),
                           jnp.concatenate([vw, v_c], axis=2), sink_hg, m)

    o = lax.map(blk, (qb, band(k_l), band(v_l), valid))
    out_l = o.transpose(1, 0, 4, 2, 3, 5).reshape(B, S, WIN_W)
    out_c = None
    if with_ctx:
        o_c = sink_attend(q_c, k_c, v_c, sink_hg, None)
        out_c = o_c.transpose(0, 3, 1, 2, 4).reshape(B, C, WIN_W)
    return out_l, out_c


def merge(p_gate, y_a, y_b, y_c, w_branch, w_out):
    g = jax.nn.sigmoid(p_gate.reshape(p_gate.shape[:-1] + (N_BRANCH, D_MODEL)))
    z = (g[..., 0, :] * (y_a @ w_branch[:RWKV_W])
         + g[..., 1, :] * (y_b @ w_branch[RWKV_W:RWKV_W + DIFF_W])
         + g[..., 2, :] * (y_c @ w_branch[RWKV_W + DIFF_W:]))
    return z @ w_out


def token_mixer(u_l, u_c, w_in, mu, w0, w_up, a0, a_up, g_up, kvec, lnx,
                lam_vecs, subln_g, sink, w_branch, w_out, rope, lam_init, with_ctx):
    p_l = u_l @ w_in
    p_c = u_c @ w_in
    f_l = rwkv_features(p_l[..., :RWKV_IN], mu, w0, w_up, a0, a_up, g_up, kvec)
    f_c = rwkv_features(p_c[..., :RWKV_IN], mu, w0, w_up, a0, a_up, g_up, kvec)
    ya_l, ya_c = rwkv_bidir(f_l, f_c, kvec[2], lnx, with_ctx)
    yb_l, yb_c = diff_attention(p_l[..., DIFF_OFF:WIN_OFF], p_c[..., DIFF_OFF:WIN_OFF],
                                lam_vecs, subln_g, lam_init, rope, with_ctx)
    yc_l, yc_c = window_attention(p_l[..., WIN_OFF:GATE_OFF], p_c[..., WIN_OFF:GATE_OFF],
                                  sink, rope, with_ctx)
    out_l = merge(p_l[..., GATE_OFF:], ya_l, yb_l, yc_l, w_branch, w_out)
    out_c = merge(p_c[..., GATE_OFF:], ya_c, yb_c, yc_c, w_branch, w_out) if with_ctx else None
    return out_l, out_c


def expert_dispatch(u, experts, gates, w1, w3, w2):
    T, D = u.shape
    K = experts.shape[1]
    A = T * K
    E = w1.shape[0]
    nb = -(-A // MOE_BLK) + E
    e_flat = experts.reshape(A)
    tok = jnp.arange(A, dtype=jnp.int32) // K
    g_flat = gates.reshape(A)
    order = jnp.argsort(e_flat)
    e_s = e_flat[order]
    counts = jnp.bincount(e_flat, length=E)
    padded = (counts + MOE_BLK - 1) // MOE_BLK * MOE_BLK
    pad_end = jnp.cumsum(padded)
    pad_start = pad_end - padded
    start = jnp.cumsum(counts) - counts
    dest = pad_start[e_s] + jnp.arange(A, dtype=jnp.int32) - start[e_s]
    slot_tok = jnp.full((nb * MOE_BLK,), T, jnp.int32).at[dest].set(tok[order])
    slot_gate = jnp.zeros((nb * MOE_BLK,), u.dtype).at[dest].set(g_flat[order].astype(u.dtype))
    blk_expert = jnp.minimum(jnp.searchsorted(pad_end, jnp.arange(nb) * MOE_BLK, side='right'), E - 1)
    u_pad = jnp.concatenate([u, jnp.zeros((1, D), u.dtype)], axis=0)

    def run_block(args):
        t_idx, gw, e = args
        xb = u_pad[t_idx]
        h = jax.nn.silu(xb @ w1[e]) * (xb @ w3[e])
        return (h @ w2[e]) * gw[:, None]

    y_slots = lax.map(run_block, (slot_tok.reshape(nb, MOE_BLK), slot_gate.reshape(nb, MOE_BLK), blk_expert))
    y = jax.ops.segment_sum(y_slots.reshape(-1, D), slot_tok, num_segments=T + 1)
    return y[:T]


def hier_moe(u, w_rg, b_rg, w_re, b_re, w1, w3, w2):
    T = u.shape[0]
    pg = jax.nn.softmax((u @ w_rg).astype(jnp.float32) + b_rg.astype(jnp.float32), axis=-1)
    pg_top, g_idx = lax.top_k(pg, 1)
    le = ((u @ w_re).astype(jnp.float32) + b_re.astype(jnp.float32)).reshape(T, N_GROUPS, EXPERTS_PER_GROUP)
    le_sel = jnp.take_along_axis(le, g_idx[:, :, None], axis=1)[:, 0]
    pe_top, e_loc = lax.top_k(jax.nn.softmax(le_sel, axis=-1), EXPERT_TOP_K)
    gates = pg_top * pe_top / jnp.sum(pe_top, axis=-1, keepdims=True)
    experts = g_idx * EXPERTS_PER_GROUP + e_loc
    return expert_dispatch(u, experts, gates, w1, w3, w2)


def setup_inputs(seed: int = 0) -> dict:
    key = jax.random.key(seed)
    ks = jax.random.split(key, 32)
    f32 = jnp.float32
    nrm = lambda k, shape, s: jax.random.normal(k, shape, f32) * s
    L, D, E, F = DEPTH, D_MODEL, N_EXPERTS, D_EXPERT
    col_scale = np.ones((N_IN,), np.float32)
    col_scale[2 * RWKV_W:3 * RWKV_W] = DN_BETA
    col_scale[DIFF_OFF + 2 * DIFF_W:DIFF_OFF + 3 * DIFF_W] = DN_BETA
    col_scale[WIN_OFF + WIN_W + WIN_KV_W:GATE_OFF] = DN_BETA
    kvec_mean = jnp.array([0.85, 1.0, 0.0], f32)[None, :, None]
    kvec_std = jnp.array([0.02, 0.02, 0.1], f32)[None, :, None]
    return {
        "x": nrm(ks[0], (BATCH, SEQ, D), 1.0),
        "c": nrm(ks[1], (BATCH, D), 1.0),
        "ctx": nrm(ks[2], (BATCH, CTX_LEN, D), 1.0),
        "c_ctx": nrm(ks[3], (D,), 1.0),
        "w_mod": nrm(ks[4], (L, D, 6 * D), 0.5 * D ** -0.5),
        "b_mod": nrm(ks[5], (L, 6 * D), 0.02),
        "w_in": nrm(ks[6], (L, D, N_IN), D ** -0.5) * jnp.asarray(col_scale),
        "rwkv_mu": jax.random.uniform(ks[7], (L, 2, RWKV_IN), f32, 0.0, 0.5),
        "rwkv_w0": jax.random.uniform(ks[8], (L, 2, RWKV_W), f32, -6.0, -1.0),
        "rwkv_w_up": nrm(ks[9], (L, 2, DECAY_LORA, RWKV_W), 0.1 * DECAY_LORA ** -0.5),
        "rwkv_a0": nrm(ks[10], (L, 2, RWKV_W), 0.1),
        "rwkv_a_up": nrm(ks[11], (L, 2, AAA_LORA, RWKV_W), 0.5 * AAA_LORA ** -0.5),
        "rwkv_g_up": nrm(ks[12], (L, GATE_LORA, RWKV_W), GATE_LORA ** -0.5),
        "rwkv_kvec": kvec_mean + kvec_std * jax.random.normal(ks[13], (L, 3, RWKV_W), f32),
        "rwkv_lnx": jnp.array([1.0, 0.0], f32)[None, :, None] + nrm(ks[14], (L, 2, RWKV_W), 0.02),
        "diff_lam": nrm(ks[15], (L, 4, DIFF_DK), 0.1),
        "diff_subln": 1.0 + nrm(ks[16], (L, DIFF_DV), 0.02),
        "win_sink": nrm(ks[17], (L, WIN_Q_HEADS), 0.5),
        "w_branch": nrm(ks[18], (L, MIX_W, D), (MIX_W / N_BRANCH) ** -0.5),
        "w_out": nrm(ks[19], (L, D, D), D ** -0.5 * DN_BETA),
        "ln_g": 1.0 + nrm(ks[20], (L, 2, D), 0.02),
        "ln_b": nrm(ks[21], (L, 2, D), 0.02),
        "w_rg": nrm(ks[22], (L, D, N_GROUPS), D ** -0.5),
        "b_rg": nrm(ks[23], (L, N_GROUPS), 0.01),
        "w_re": nrm(ks[24], (L, D, E), D ** -0.5),
        "b_re": nrm(ks[25], (L, E), 0.01),
        "w1": nrm(ks[26], (L, E, D, F), D ** -0.5),
        "w3": nrm(ks[27], (L, E, D, F), D ** -0.5),
        "w2": nrm(ks[28], (L, E, F, D), F ** -0.5 * DN_BETA),
    }


def reference(x, c, ctx, c_ctx, w_mod, b_mod, w_in, rwkv_mu, rwkv_w0, rwkv_w_up, rwkv_a0,
              rwkv_a_up, rwkv_g_up, rwkv_kvec, rwkv_lnx, diff_lam, diff_subln, win_sink,
              w_branch, w_out, ln_g, ln_b, w_rg, b_rg, w_re, b_re, w1, w3, w2):
    B, S, D = x.shape
    rows = S // GRID_W
    row = jnp.repeat(jnp.arange(rows), GRID_W).astype(jnp.float32)
    col = (jnp.arange(rows * GRID_W) % GRID_W).astype(jnp.float32)
    inv = ROPE_BASE ** (-jnp.arange(ROPE_AX_FREQS, dtype=jnp.float32) / ROPE_AX_FREQS)
    ang = jnp.stack([row[:, None] * inv, col[:, None] * inv], axis=1)
    rope = (jnp.cos(ang), jnp.sin(ang))
    xc = ctx
    for i in range(DEPTH):
        last = i == DEPTH - 1
        lam_init = 0.8 - 0.6 * math.exp(-0.3 * i)
        mod_l = jnp.split((jax.nn.silu(c) @ w_mod[i] + b_mod[i])[:, None, :], 6, axis=-1)
        mod_c = jnp.split(jax.nn.silu(c_ctx) @ w_mod[i] + b_mod[i], 6, axis=-1)
        u_l = modulate(x, mod_l[0], mod_l[1])
        u_c = modulate(xc, mod_c[0], mod_c[1])
        m_l, m_c = token_mixer(u_l, u_c, w_in[i], rwkv_mu[i], rwkv_w0[i], rwkv_w_up[i], rwkv_a0[i],
                               rwkv_a_up[i], rwkv_g_up[i], rwkv_kvec[i], rwkv_lnx[i], diff_lam[i],
                               diff_subln[i], win_sink[i], w_branch[i], w_out[i], rope, lam_init,
                               not last)
        x = layer_norm(DN_ALPHA * x + mod_l[2] * m_l, ln_g[i, 0], ln_b[i, 0], LN_EPS)
        v_l = modulate(x, mod_l[3], mod_l[4])
        if last:
            y_l = hier_moe(v_l.reshape(-1, D), w_rg[i], b_rg[i], w_re[i], b_re[i],
                           w1[i], w3[i], w2[i]).reshape(B, S, D)
        else:
            xc = layer_norm(DN_ALPHA * xc + mod_c[2] * m_c, ln_g[i, 0], ln_b[i, 0], LN_EPS)
            v_c = modulate(xc, mod_c[3], mod_c[4])
            y = hier_moe(jnp.concatenate([v_l.reshape(-1, D), v_c.reshape(-1, D)], axis=0),
                         w_rg[i], b_rg[i], w_re[i], b_re[i], w1[i], w3[i], w2[i])
            y_l = y[:B * S].reshape(B, S, D)
            y_c = y[B * S:].reshape(xc.shape)
            xc = layer_norm(DN_ALPHA * xc + mod_c[5] * y_c, ln_g[i, 1], ln_b[i, 1], LN_EPS)
        x = layer_norm(DN_ALPHA * x + mod_l[5] * y_l, ln_g[i, 1], ln_b[i, 1], LN_EPS)
    return x
```

```python
import functools
import math

import jax
import jax.numpy as jnp
from jax import lax
from jax.experimental import pallas as pl
from jax.experimental.pallas import tpu as pltpu

F32 = jnp.float32
BF16 = jnp.bfloat16

D_MODEL = 2048
GRID_W = 64
RWKV_HEADS = 12
RWKV_HEAD = 64
RWKV_W = RWKV_HEADS * RWKV_HEAD
DECAY_LORA = 64
AAA_LORA = 64
GATE_LORA = 128
RWKV_GN_EPS = 64e-5
DIFF_HEADS = 6
DIFF_DK = 64
DIFF_DV = 2 * DIFF_DK
DIFF_W = DIFF_HEADS * DIFF_DV
DIFF_SCALE = DIFF_DK ** -0.5
WIN_Q_HEADS = 8
WIN_KV_HEADS = 2
WIN_GROUP = WIN_Q_HEADS // WIN_KV_HEADS
WIN_HEAD = 64
WIN_W = WIN_Q_HEADS * WIN_HEAD
WIN_KV_W = WIN_KV_HEADS * WIN_HEAD
WIN_SCALE = WIN_HEAD ** -0.5
WINDOW = 128
QBLK = WINDOW
MIX_W = RWKV_W + DIFF_W + WIN_W
N_BRANCH = 3
ROPE_BASE = 10000.0
ROPE_AX_FREQS = 16
RWKV_IN = 3 * RWKV_W + 2 * DECAY_LORA + 2 * AAA_LORA + GATE_LORA
DIFF_IN = 3 * DIFF_W
WIN_IN = WIN_W + 2 * WIN_KV_W
DIFF_OFF = RWKV_IN
WIN_OFF = DIFF_OFF + DIFF_IN
GATE_OFF = WIN_OFF + WIN_IN
N_IN = GATE_OFF + N_BRANCH * D_MODEL
N_GROUPS = 4
EXPERTS_PER_GROUP = 8
N_EXPERTS = N_GROUPS * EXPERTS_PER_GROUP
EXPERT_TOP_K = 2
D_EXPERT = D_MODEL // 4
MOE_BLK = 256
ADA_EPS = 1e-6
LN_EPS = 1e-5
NEG_INF = -1e30

LANE = 128
SEG = 256
P_RWKV = 0
P_GATE = 3072
P_DIFF = P_GATE + N_BRANCH * D_MODEL
P_WIN = P_DIFF + DIFF_IN
P_COLS = P_WIN + WIN_IN
CHUNK = 64
VMEM_LIMIT = 56 * 1024 * 1024


def _cparams(sem):
    return pltpu.CompilerParams(dimension_semantics=sem, vmem_limit_bytes=VMEM_LIMIT)


def _dot(a, b):
    return jnp.dot(a.astype(BF16), b.astype(BF16), preferred_element_type=F32)


def _dot_nt(a, b):
    return lax.dot_general(a.astype(BF16), b.astype(BF16), (((1,), (1,)), ((), ())),
                           preferred_element_type=F32)


def _dot_tn(a, b):
    return lax.dot_general(a.astype(BF16), b.astype(BF16), (((0,), (0,)), ((), ())),
                           preferred_element_type=F32)


def _split(x):
    hi = x.astype(BF16)
    lo = (x - hi.astype(F32)).astype(BF16)
    return hi, lo


def _dot3(a, b):
    ah, al = _split(a)
    bh, bl = _split(b)
    d = functools.partial(jnp.dot, preferred_element_type=F32)
    return d(ah, bh) + (d(ah, bl) + d(al, bh))


def _sigmoid(x):
    return 1.0 / (1.0 + jnp.exp(-x))


def _mm_kernel(a_ref, w_ref, o_ref):
    o_ref[...] = _dot(a_ref[...], w_ref[...]).astype(o_ref.dtype)


def _mm(a, w, out_dtype, tm, tn, name):
    m, k = a.shape
    n = w.shape[1]
    return pl.pallas_call(
        _mm_kernel,
        out_shape=jax.ShapeDtypeStruct((m, n), out_dtype),
        grid=(n // tn, m // tm),
        in_specs=[pl.BlockSpec((tm, k), lambda j, i: (i, 0)),
                  pl.BlockSpec((k, tn), lambda j, i: (0, j))],
        out_specs=pl.BlockSpec((tm, tn), lambda j, i: (i, j)),
        compiler_params=_cparams(("arbitrary", "arbitrary")),
        name=name,
    )(a, w)


def _router_kernel(a_ref, w_ref, o_ref):
    o_ref[...] = _dot3(a_ref[...], w_ref[...])


def _router(a, w, tm):
    m, k = a.shape
    n = w.shape[1]
    return pl.pallas_call(
        _router_kernel,
        out_shape=jax.ShapeDtypeStruct((m, n), F32),
        grid=(m // tm,),
        in_specs=[pl.BlockSpec((tm, k), lambda i: (i, 0)),
                  pl.BlockSpec((k, n), lambda i: (0, 0))],
        out_specs=pl.BlockSpec((tm, n), lambda i: (i, 0)),
        compiler_params=_cparams(("arbitrary",)),
        name="router",
    )(a, w)


def _rwkv_kernel(p_ref, hp_ref, hn_ref, mu_ref, vec_ref, wup_ref, aup_ref, gup_ref, msk_ref,
                 y_ref, bv_ref, g_ref, state, *, nc_ctx, nc_lat):
    d = pl.program_id(1)
    i = pl.program_id(2)
    chunk = _rwkv_chunk(d, i, nc_ctx, nc_lat)

    @pl.when(i == 0)
    def _():
        state[...] = jnp.zeros_like(state)

    n = CHUNK
    p = p_ref[...].astype(F32)
    first = jnp.logical_or(chunk == 0, chunk == nc_ctx)
    last = jnp.logical_or(chunk == nc_ctx - 1, chunk == nc_ctx + nc_lat - 1)
    hp = jnp.where(first, 0.0, hp_ref[15:16, :].astype(F32))
    hn = jnp.where(last, 0.0, hn_ref[0:1, :].astype(F32))
    row = lax.broadcasted_iota(jnp.int32, (n, 1), 0)
    prev = jnp.where(row == 0, hp, pltpu.roll(p, 1, axis=0))
    nxt = jnp.where(row == n - 1, hn, pltpu.roll(p, n - 1, axis=0))
    ps = p + mu_ref[0:1, :] * (prev - p) + mu_ref[1:2, :] * (nxt - p)

    r = ps[:, 0:RWKV_W]
    k = ps[:, RWKV_W:2 * RWKV_W]
    v = ps[:, 2 * RWKV_W:3 * RWKV_W]
    o = 3 * RWKV_W
    wd = jnp.tanh(ps[:, o:o + 2 * DECAY_LORA])
    ad = ps[:, o + 2 * DECAY_LORA:o + 2 * DECAY_LORA + 2 * AAA_LORA]
    gd = ps[:, o + 2 * DECAY_LORA + 2 * AAA_LORA:]
    w0 = vec_ref[0, 0:1, :]
    a0 = vec_ref[0, 1:2, :]
    k_k = vec_ref[0, 2:3, :]
    k_a = vec_ref[0, 3:4, :]
    r_k = vec_ref[0, 4:5, :]
    w_log = w0 + _dot(wd, wup_ref[0])
    a = _sigmoid(a0 + _dot(ad, aup_ref[0]))
    g_ref[0] = _dot(_sigmoid(gd), gup_ref[...]).astype(g_ref.dtype)
    logw = -math.exp(-0.5) * _sigmoid(w_log)

    strict = msk_ref[0, 0] > 0.5
    incl_f = msk_ref[0, 1]
    incl = incl_f > 0.5
    lw_hi, lw_lo = _split(logw)
    incl_b = incl_f.astype(BF16)
    cl = (jnp.dot(incl_b, lw_hi, preferred_element_type=F32)
          + jnp.dot(incl_b, lw_lo, preferred_element_type=F32))
    tot = jnp.sum(logw, axis=0, keepdims=True)
    e_in = jnp.exp(cl)
    e_ex = jnp.exp(cl - logw)
    e_inv = jnp.exp(-cl)
    e_end = jnp.exp(tot - cl)
    p_all = jnp.exp(tot)

    eye = (lax.broadcasted_iota(jnp.int32, (n, n), 0) == lax.broadcasted_iota(jnp.int32, (n, n), 1))
    eye_f = eye.astype(F32)
    zeros = jnp.zeros((n, RWKV_HEAD), F32)
    ys, bvs = [], []
    for h in range(RWKV_HEADS):
        sl = slice(h * RWKV_HEAD, (h + 1) * RWKV_HEAD)
        r_h, k_h, v_h, a_h = r[:, sl], k[:, sl], v[:, sl], a[:, sl]
        kk = k_h * k_k[:, sl]
        kk = kk / jnp.maximum(jnp.sqrt(jnp.sum(kk * kk, axis=-1, keepdims=True)), 1e-12)
        kd = k_h * (1.0 + (a_h - 1.0) * k_a[:, sl])
        bd = kk * a_h
        bonus = jnp.sum(r_h * kd * r_k[:, sl], axis=-1, keepdims=True)
        bvs.append(bonus * v_h)

        kk_t = kk * e_ex[:, sl]
        r_t = r_h * e_in[:, sl]
        b_i = bd * e_inv[:, sl]
        k_i = kd * e_inv[:, sl]
        k_e = kd * e_end[:, sl]
        b_e = bd * e_end[:, sl]

        gram = _dot_nt(jnp.concatenate([kk_t, r_t], axis=0), jnp.concatenate([b_i, k_i], axis=0))
        l_b = jnp.where(strict, gram[:n, :n], 0.0)
        l_k = jnp.where(strict, gram[:n, n:], 0.0)
        m_rb = jnp.where(incl, gram[n:, :n], 0.0)
        m_rk = jnp.where(incl, gram[n:, n:], 0.0)
        pw = -l_b
        t_inv = eye_f + pw
        for _ in range(5):
            pw = _dot(pw, pw)
            t_inv = t_inv + _dot(t_inv, pw)
        lkv = _dot(l_k, v_h)
        tx = _dot(t_inv, jnp.concatenate([kk_t, lkv], axis=1))
        rhs = jnp.concatenate([jnp.concatenate([zeros, v_h], axis=1), tx], axis=0)
        top = _dot(jnp.concatenate([m_rk, -m_rb], axis=1), rhs)
        bot_t = _dot_tn(rhs, jnp.concatenate([k_e, -b_e], axis=0))
        r_eff = r_t + top[:, :RWKV_HEAD]
        y_loc = top[:, RWKV_HEAD:]
        s0 = state[h]
        ys.append(_dot_nt(r_eff, s0) + y_loc)
        state[h] = s0 * p_all[:, sl] + _dot(s0, bot_t[:RWKV_HEAD]) + bot_t[RWKV_HEAD:]
    y_ref[0] = jnp.concatenate(ys, axis=1)
    bv_ref[0] = jnp.concatenate(bvs, axis=1)


def _rwkv_chunk(d, i, nc_ctx, nc_lat):
    rev = jnp.where(i < nc_ctx, nc_ctx - 1 - i, 2 * nc_ctx + nc_lat - 1 - i)
    return jnp.where(d == 0, i, rev)


def _rwkv_scan(p, mu, vecs, wup, aup, gup, masks, bsz, n_ctx, n_lat):
    ltot = n_ctx + n_lat
    nc_ctx, nc_lat = n_ctx // CHUNK, n_lat // CHUNK
    nc = nc_ctx + nc_lat
    rows = bsz * ltot
    hb = CHUNK // 16
    n_hblk = rows // 16

    def main_map(b, d, i):
        return (b * nc + _rwkv_chunk(d, i, nc_ctx, nc_lat), 0)

    def prev_map(b, d, i):
        return (jnp.maximum((b * nc + _rwkv_chunk(d, i, nc_ctx, nc_lat)) * hb - 1, 0), 0)

    def next_map(b, d, i):
        return (jnp.minimum((b * nc + _rwkv_chunk(d, i, nc_ctx, nc_lat) + 1) * hb, n_hblk - 1), 0)

    def out_map(b, d, i):
        return (d, b * nc + _rwkv_chunk(d, i, nc_ctx, nc_lat), 0)

    out_sd = jax.ShapeDtypeStruct((2, rows, RWKV_W), F32)
    kern = functools.partial(_rwkv_kernel, nc_ctx=nc_ctx, nc_lat=nc_lat)
    return pl.pallas_call(
        kern,
        out_shape=(out_sd, out_sd, jax.ShapeDtypeStruct((2, rows, RWKV_W), BF16)),
        grid=(bsz, 2, nc),
        in_specs=[pl.BlockSpec((CHUNK, RWKV_IN), main_map),
                  pl.BlockSpec((16, RWKV_IN), prev_map),
                  pl.BlockSpec((16, RWKV_IN), next_map),
                  pl.BlockSpec((2, RWKV_IN), lambda b, d, i: (0, 0)),
                  pl.BlockSpec((1, 8, RWKV_W), lambda b, d, i: (d, 0, 0)),
                  pl.BlockSpec((1, 2 * DECAY_LORA, RWKV_W), lambda b, d, i: (d, 0, 0)),
                  pl.BlockSpec((1, 2 * AAA_LORA, RWKV_W), lambda b, d, i: (d, 0, 0)),
                  pl.BlockSpec((GATE_LORA, RWKV_W), lambda b, d, i: (0, 0)),
                  pl.BlockSpec((1, 2, CHUNK, CHUNK), lambda b, d, i: (d, 0, 0, 0))],
        out_specs=(pl.BlockSpec((1, CHUNK, RWKV_W), out_map),
                   pl.BlockSpec((1, CHUNK, RWKV_W), out_map),
                   pl.BlockSpec((1, CHUNK, RWKV_W), out_map)),
        scratch_shapes=[pltpu.VMEM((RWKV_HEADS, RWKV_HEAD, RWKV_HEAD), F32)],
        compiler_params=_cparams(("arbitrary", "arbitrary", "arbitrary")),
        name="rwkv_scan",
    )(p, p, p, mu, vecs, wup, aup, gup, masks)


def _softmax_rows(s):
    m = jnp.max(s, axis=-1, keepdims=True)
    e = jnp.exp(s - m)
    return e / jnp.sum(e, axis=-1, keepdims=True)


def _diff_kernel(q_ref, k_ref, v_ref, par_ref, o_ref, *, n_ctx):
    t = pl.program_id(2)
    lam = par_ref[0:1, :]
    gain = par_ref[1:2, :]

    def attend(nk):
        q = q_ref[...]
        k = k_ref[0:nk, :]
        v = v_ref[0:nk, :]
        s0 = _dot_nt(q[:, :DIFF_DK], k[:, :DIFF_DK]) * DIFF_SCALE
        s1 = _dot_nt(q[:, DIFF_DK:], k[:, DIFF_DK:]) * DIFF_SCALE
        a = _softmax_rows(s0) - lam[:, 0:1] * _softmax_rows(s1)
        o = _dot(a, v)
        o = o * lax.rsqrt(jnp.mean(o * o, axis=-1, keepdims=True) + 1e-5)
        o_ref[...] = (o * gain).astype(o_ref.dtype)

    @pl.when(t * SEG < n_ctx)
    def _():
        attend(n_ctx)

    @pl.when(t * SEG >= n_ctx)
    def _():
        attend(k_ref.shape[0])


def _diff_attn(q, k, p, par, bsz, n_ctx, ltot):
    rows = bsz * ltot
    nt = ltot // SEG
    voff = (P_DIFF + 2 * DIFF_W) // DIFF_DV
    return pl.pallas_call(
        functools.partial(_diff_kernel, n_ctx=n_ctx),
        out_shape=jax.ShapeDtypeStruct((rows, DIFF_W), BF16),
        grid=(bsz, DIFF_HEADS, nt),
        in_specs=[pl.BlockSpec((SEG, DIFF_DV), lambda b, h, t: (b * nt + t, h)),
                  pl.BlockSpec((ltot, DIFF_DV), lambda b, h, t: (b, h)),
                  pl.BlockSpec((ltot, DIFF_DV), lambda b, h, t: (b, voff + h)),
                  pl.BlockSpec((8, DIFF_DV), lambda b, h, t: (0, 0))],
        out_specs=pl.BlockSpec((SEG, DIFF_DV), lambda b, h, t: (b * nt + t, h)),
        compiler_params=_cparams(("arbitrary", "arbitrary", "arbitrary")),
        name="diff_attn",
    )(q, k, p, par)


def _win_kernel(sink_ref, q_ref, kp_ref, kc_ref, kn_ref, kx_ref, vp_ref, vc_ref, vn_ref, vx_ref,
                o_ref, *, n_ctx, n_lat):
    blk = pl.program_id(1)
    ncb = n_ctx // QBLK
    nq = WIN_GROUP * QBLK

    def run(keys, vals, mask):
        outs = []
        for g in range(WIN_KV_HEADS):
            ksl = slice(g * WIN_HEAD, (g + 1) * WIN_HEAD)
            qg = jnp.concatenate(
                [q_ref[:, (g * WIN_GROUP + j) * WIN_HEAD:(g * WIN_GROUP + j + 1) * WIN_HEAD]
                 for j in range(WIN_GROUP)], axis=0)
            s = _dot_nt(qg, keys[:, ksl]) * WIN_SCALE
            if mask is not None:
                s = jnp.where(mask, s, NEG_INF)
            hrow = lax.broadcasted_iota(jnp.int32, (nq, 1), 0) // QBLK
            sink = jnp.zeros((nq, 1), F32)
            for j in range(WIN_GROUP):
                sink = jnp.where(hrow == j, sink_ref[g * WIN_GROUP + j], sink)
            m = jnp.maximum(jnp.max(s, axis=-1, keepdims=True), sink)
            e = jnp.exp(s - m)
            den = jnp.sum(e, axis=-1, keepdims=True) + jnp.exp(sink - m)
            og = _dot(e / den, vals[:, ksl])
            outs.extend(og[j * QBLK:(j + 1) * QBLK] for j in range(WIN_GROUP))
        o_ref[...] = jnp.concatenate(outs, axis=1).astype(o_ref.dtype)

    @pl.when(blk < ncb)
    def _():
        run(kx_ref[...], vx_ref[...], None)

    @pl.when(blk >= ncb)
    def _():
        j = blk - ncb
        keys = jnp.concatenate([kp_ref[...], kc_ref[...], kn_ref[...], kx_ref[...]], axis=0)
        vals = jnp.concatenate([vp_ref[...], vc_ref[...], vn_ref[...], vx_ref[...]], axis=0)
        nk = 3 * QBLK + n_ctx
        qpos = j * QBLK + lax.broadcasted_iota(jnp.int32, (nq, nk), 0) % QBLK
        col = lax.broadcasted_iota(jnp.int32, (nq, nk), 1)
        kpos = (j - 1) * QBLK + col
        band = (jnp.abs(qpos - kpos) <= WINDOW) & (kpos >= 0) & (kpos < n_lat)
        run(keys, vals, band | (col >= 3 * QBLK))


def _win_attn(sink, q, k, p, bsz, n_ctx, n_lat):
    ltot = n_ctx + n_lat
    rows = bsz * ltot
    nb = ltot // QBLK
    ncb = n_ctx // QBLK
    voff = (P_WIN + WIN_W + WIN_KV_W) // WIN_KV_W

    def lat_blk(b, t, shift):
        j = jnp.clip(t - ncb + shift, 0, nb - ncb - 1)
        return b * nb + ncb + j

    kspec = lambda shift, c: pl.BlockSpec((QBLK, WIN_KV_W), lambda b, t, s: (lat_blk(b, t, shift), c))
    xspec = lambda c: pl.BlockSpec((n_ctx, WIN_KV_W), lambda b, t, s: (b * (ltot // n_ctx), c))
    return pl.pallas_call(
        functools.partial(_win_kernel, n_ctx=n_ctx, n_lat=n_lat),
        out_shape=jax.ShapeDtypeStruct((rows, WIN_W), BF16),
        grid_spec=pltpu.PrefetchScalarGridSpec(
            num_scalar_prefetch=1,
            grid=(bsz, nb),
            in_specs=[pl.BlockSpec((QBLK, WIN_W), lambda b, t, s: (b * nb + t, 0)),
                      kspec(-1, 0), kspec(0, 0), kspec(1, 0), xspec(0),
                      kspec(-1, voff), kspec(0, voff), kspec(1, voff), xspec(voff)],
            out_specs=pl.BlockSpec((QBLK, WIN_W), lambda b, t, s: (b * nb + t, 0))),
        compiler_params=_cparams(("arbitrary", "arbitrary")),
        name="win_attn",
    )(sink, q, k, k, k, k, p, p, p, p)


def _merge_kernel(ya_ref, yb_ref, yc_ref, ga_ref, gb_ref, gc_ref, wa_ref, wb_ref, wc_ref, o_ref):
    z = _sigmoid(ga_ref[...].astype(F32)) * _dot(ya_ref[...], wa_ref[...])
    z = z + _sigmoid(gb_ref[...].astype(F32)) * _dot(yb_ref[...], wb_ref[...])
    z = z + _sigmoid(gc_ref[...].astype(F32)) * _dot(yc_ref[...], wc_ref[...])
    o_ref[...] = z.astype(o_ref.dtype)


def _merge(ya, yb, yc, p, wa, wb, wc, tm, tn):
    m = ya.shape[0]
    nj = D_MODEL // tn
    goff = P_GATE // tn
    gspec = lambda br: pl.BlockSpec((tm, tn), lambda j, i: (i, goff + br * nj + j))
    return pl.pallas_call(
        _merge_kernel,
        out_shape=jax.ShapeDtypeStruct((m, D_MODEL), BF16),
        grid=(nj, m // tm),
        in_specs=[pl.BlockSpec((tm, RWKV_W), lambda j, i: (i, 0)),
                  pl.BlockSpec((tm, DIFF_W), lambda j, i: (i, 0)),
                  pl.BlockSpec((tm, WIN_W), lambda j, i: (i, 0)),
                  gspec(0), gspec(1), gspec(2),
                  pl.BlockSpec((RWKV_W, tn), lambda j, i: (0, j)),
                  pl.BlockSpec((DIFF_W, tn), lambda j, i: (0, j)),
                  pl.BlockSpec((WIN_W, tn), lambda j, i: (0, j))],
        out_specs=pl.BlockSpec((tm, tn), lambda j, i: (i, j)),
        compiler_params=_cparams(("arbitrary", "arbitrary")),
        name="merge",
    )(ya, yb, yc, p, p, p, wa, wb, wc)


def _expert_kernel(be_ref, x_ref, gw_ref, w1_ref, w3_ref, w2_ref, o_ref):
    del be_ref
    x = x_ref[...]
    h1 = _dot(x, w1_ref[0])
    h = h1 * _sigmoid(h1) * _dot(x, w3_ref[0])
    o_ref[...] = (_dot(h, w2_ref[0]) * gw_ref[...]).astype(o_ref.dtype)


def _experts(blk_expert, xb, gw, w1, w3, w2):
    rows = xb.shape[0]
    nb = rows // MOE_BLK
    return pl.pallas_call(
        _expert_kernel,
        out_shape=jax.ShapeDtypeStruct((rows, D_MODEL), F32),
        grid_spec=pltpu.PrefetchScalarGridSpec(
            num_scalar_prefetch=1,
            grid=(nb,),
            in_specs=[pl.BlockSpec((MOE_BLK, D_MODEL), lambda i, be: (i, 0)),
                      pl.BlockSpec((MOE_BLK, 1), lambda i, be: (i, 0)),
                      pl.BlockSpec((1, D_MODEL, D_EXPERT), lambda i, be: (be[i], 0, 0)),
                      pl.BlockSpec((1, D_MODEL, D_EXPERT), lambda i, be: (be[i], 0, 0)),
                      pl.BlockSpec((1, D_EXPERT, D_MODEL), lambda i, be: (be[i], 0, 0))],
            out_specs=pl.BlockSpec((MOE_BLK, D_MODEL), lambda i, be: (i, 0))),
        compiler_params=_cparams(("arbitrary",)),
        name="experts",
    )(blk_expert, xb, gw, w1, w3, w2)


def _layer_norm(x, g, b, eps):
    mu = jnp.mean(x, axis=-1, keepdims=True)
    var = jnp.mean(jnp.square(x - mu), axis=-1, keepdims=True)
    y = (x - mu) * lax.rsqrt(var + eps)
    if g is not None:
        y = y * g + b
    return y


def _rope_tables(n_ctx, n_lat):
    rows = n_lat // GRID_W
    row = jnp.repeat(jnp.arange(rows), GRID_W).astype(F32)
    col = (jnp.arange(rows * GRID_W) % GRID_W).astype(F32)
    inv = ROPE_BASE ** (-jnp.arange(ROPE_AX_FREQS, dtype=F32) / ROPE_AX_FREQS)
    ang = jnp.stack([row[:, None] * inv, col[:, None] * inv], axis=1)
    cos, sin = jnp.cos(ang), jnp.sin(ang)
    cos4 = jnp.stack([cos, cos], axis=2).reshape(n_lat, 4 * ROPE_AX_FREQS)
    sin4 = jnp.stack([-sin, sin], axis=2).reshape(n_lat, 4 * ROPE_AX_FREQS)
    cos4 = jnp.concatenate([jnp.ones((n_ctx, 64), F32), cos4], axis=0)
    sin4 = jnp.concatenate([jnp.zeros((n_ctx, 64), F32), sin4], axis=0)
    return cos4, sin4


def _rope(x, cos4, sin4):
    b, l, w = x.shape
    xf = x.astype(F32).reshape(b, l, w // 32, 2, ROPE_AX_FREQS)
    sw = xf[:, :, :, ::-1, :].reshape(b, l, w // 64, 64)
    xf = xf.reshape(b, l, w // 64, 64)
    out = xf * cos4[None, :, None, :] + sw * sin4[None, :, None, :]
    return out.reshape(b, l, w).astype(BF16)


def _route(logits, b_rg, b_re):
    t = logits.shape[0]
    pg = jax.nn.softmax(logits[:, :N_GROUPS] + b_rg, axis=-1)
    pg_top, g_idx = lax.top_k(pg, 1)
    le = (logits[:, N_GROUPS:N_GROUPS + N_EXPERTS] + b_re).reshape(t, N_GROUPS, EXPERTS_PER_GROUP)
    le_sel = jnp.take_along_axis(le, g_idx[:, :, None], axis=1)[:, 0]
    pe_top, e_loc = lax.top_k(jax.nn.softmax(le_sel, axis=-1), EXPERT_TOP_K)
    gates = pg_top * pe_top / jnp.sum(pe_top, axis=-1, keepdims=True)
    return g_idx * EXPERTS_PER_GROUP + e_loc, gates


def _moe(v, w_r, b_rg, b_re, w1, w3, w2):
    t = v.shape[0]
    logits = _router(v, w_r, SEG)
    experts, gates = _route(logits, b_rg, b_re)
    k = EXPERT_TOP_K
    a = t * k
    e_n = N_EXPERTS
    nb = -(-a // MOE_BLK) + e_n
    e_flat = experts.reshape(a).astype(jnp.int32)
    order = jnp.argsort(e_flat)
    e_s = e_flat[order]
    counts = jnp.bincount(e_flat, length=e_n)
    padded = (counts + MOE_BLK - 1) // MOE_BLK * MOE_BLK
    pad_end = jnp.cumsum(padded)
    pad_start = pad_end - padded
    start = jnp.cumsum(counts) - counts
    dest = (pad_start[e_s] + jnp.arange(a, dtype=jnp.int32) - start[e_s]).astype(jnp.int32)
    tok = (order // k).astype(jnp.int32)
    slot_tok = jnp.full((nb * MOE_BLK,), t, jnp.int32).at[dest].set(tok)
    slot_gate = jnp.zeros((nb * MOE_BLK,), F32).at[dest].set(gates.reshape(a)[order])
    blk_expert = jnp.minimum(jnp.searchsorted(pad_end, jnp.arange(nb) * MOE_BLK, side='right'),
                             e_n - 1).astype(jnp.int32)
    v_pad = jnp.concatenate([v.astype(BF16), jnp.zeros((1, D_MODEL), BF16)], axis=0)
    xb = v_pad[slot_tok]
    ys = _experts(blk_expert, xb, slot_gate[:, None], w1, w3, w2)
    pos = jnp.zeros((a,), jnp.int32).at[order].set(dest).reshape(t, k)
    return ys[pos[:, 0]] + ys[pos[:, 1]]


def kernel(x, c, ctx, c_ctx, w_mod, b_mod, w_in, rwkv_mu, rwkv_w0, rwkv_w_up, rwkv_a0, rwkv_a_up,
           rwkv_g_up, rwkv_kvec, rwkv_lnx, diff_lam, diff_subln, win_sink, w_branch, w_out, ln_g, ln_b,
           w_rg, b_rg, w_re, b_re, w1, w3, w2):
    bsz, n_lat, dm = x.shape
    n_ctx = ctx.shape[1]
    depth = w_mod.shape[0]
    ltot = n_ctx + n_lat
    rows = bsz * ltot
    dn_alpha = (2 * depth) ** 0.25
    assert dm == D_MODEL and n_ctx % SEG == 0 and n_lat % SEG == 0 and ltot % n_ctx == 0

    cos4, sin4 = _rope_tables(n_ctx, n_lat)
    fwd = jnp.tril(jnp.ones((CHUNK, CHUNK), F32))
    masks = jnp.stack([jnp.stack([fwd - jnp.eye(CHUNK, dtype=F32), fwd]),
                       jnp.stack([fwd.T - jnp.eye(CHUNK, dtype=F32), fwd.T])])

    xs = jnp.concatenate([ctx, x], axis=1)
    cvec = jnp.concatenate([c_ctx[None, :], c], axis=0)
    cpad = jnp.zeros((32, dm), F32).at[:bsz + 1].set(jax.nn.silu(cvec))

    for i in range(depth):
        last = i == depth - 1
        lam_init = 0.8 - 0.6 * math.exp(-0.3 * i)
        mod = _mm(cpad, w_mod[i], F32, 32, 1024, "mod")[:bsz + 1] + b_mod[i]
        mod_c = mod[0]
        mod_l = mod[1:]
        def seg(idx):
            mc = jnp.broadcast_to(mod_c[None, None, idx * dm:(idx + 1) * dm], (bsz, n_ctx, dm))
            ml = jnp.broadcast_to(mod_l[:, None, idx * dm:(idx + 1) * dm], (bsz, n_lat, dm))
            return jnp.concatenate([mc, ml], axis=1)

        u = (_layer_norm(xs, None, None, ADA_EPS) * (1 + seg(1)) + seg(0)).astype(BF16)
        w_in_p = jnp.zeros((dm, P_COLS), BF16)
        w_in_b = w_in[i].astype(BF16)
        w_in_p = w_in_p.at[:, :RWKV_IN].set(w_in_b[:, :RWKV_IN])
        w_in_p = w_in_p.at[:, P_GATE:P_DIFF].set(w_in_b[:, GATE_OFF:])
        w_in_p = w_in_p.at[:, P_DIFF:].set(w_in_b[:, DIFF_OFF:GATE_OFF])
        p = _mm(u.reshape(rows, dm), w_in_p, BF16, 512, 1024, "in_proj")

        vecs = jnp.zeros((2, 8, RWKV_W), F32)
        vecs = vecs.at[:, 0].set(rwkv_w0[i]).at[:, 1].set(rwkv_a0[i])
        vecs = vecs.at[:, 2:5].set(jnp.broadcast_to(rwkv_kvec[i][None], (2, 3, RWKV_W)))
        wup = jnp.zeros((2, 2 * DECAY_LORA, RWKV_W), BF16)
        aup = jnp.zeros((2, 2 * AAA_LORA, RWKV_W), BF16)
        for d in range(2):
            wup = wup.at[d, d * DECAY_LORA:(d + 1) * DECAY_LORA].set(rwkv_w_up[i, d].astype(BF16))
            aup = aup.at[d, d * AAA_LORA:(d + 1) * AAA_LORA].set(rwkv_a_up[i, d].astype(BF16))
        y2, bv2, g2 = _rwkv_scan(p, rwkv_mu[i], vecs, wup, aup, rwkv_g_up[i].astype(BF16), masks,
                                 bsz, n_ctx, n_lat)
        yh = (y2[0] + y2[1]).reshape(rows, RWKV_HEADS, RWKV_HEAD)
        mu_h = jnp.mean(yh, axis=-1, keepdims=True)
        var_h = jnp.mean(jnp.square(yh - mu_h), axis=-1, keepdims=True)
        yn = ((yh - mu_h) * lax.rsqrt(var_h + RWKV_GN_EPS)).reshape(rows, RWKV_W)
        yn = yn * rwkv_lnx[i, 0] + rwkv_lnx[i, 1]
        ya = ((yn + bv2[0] + bv2[1]) * g2[0].astype(F32)).astype(BF16)

        p3 = p.reshape(bsz, ltot, P_COLS)
        dq = _rope(p3[..., P_DIFF:P_DIFF + DIFF_W], cos4, sin4).reshape(rows, DIFF_W)
        dk = _rope(p3[..., P_DIFF + DIFF_W:P_DIFF + 2 * DIFF_W], cos4, sin4).reshape(rows, DIFF_W)
        wq = _rope(p3[..., P_WIN:P_WIN + WIN_W], cos4, sin4).reshape(rows, WIN_W)
        wk = _rope(p3[..., P_WIN + WIN_W:P_WIN + WIN_W + WIN_KV_W], cos4, sin4).reshape(rows, WIN_KV_W)

        lf = diff_lam[i]
        lam = jnp.exp(jnp.sum(lf[0] * lf[1])) - jnp.exp(jnp.sum(lf[2] * lf[3])) + lam_init
        par = jnp.zeros((8, DIFF_DV), F32).at[0].set(lam).at[1].set(diff_subln[i] * (1 - lam_init))
        yb = _diff_attn(dq, dk, p, par, bsz, n_ctx, ltot)
        yc = _win_attn(win_sink[i], wq, wk, p, bsz, n_ctx, n_lat)

        wbr = w_branch[i].astype(BF16)
        z = _merge(ya, yb, yc, p, wbr[:RWKV_W], wbr[RWKV_W:RWKV_W + DIFF_W], wbr[RWKV_W + DIFF_W:],
                   512, 1024)
        m = _mm(z, w_out[i].astype(BF16), F32, 512, 1024, "out_proj").reshape(bsz, ltot, dm)
        xs = _layer_norm(dn_alpha * xs + seg(2) * m, ln_g[i, 0], ln_b[i, 0], LN_EPS)

        vmod = _layer_norm(xs, None, None, ADA_EPS) * (1 + seg(4)) + seg(3)
        w_r = jnp.zeros((dm, LANE), F32).at[:, :N_GROUPS].set(w_rg[i])
        w_r = w_r.at[:, N_GROUPS:N_GROUPS + N_EXPERTS].set(w_re[i])
        w1b, w3b, w2b = w1[i].astype(BF16), w3[i].astype(BF16), w2[i].astype(BF16)
        if last:
            vt = vmod[:, n_ctx:].reshape(bsz * n_lat, dm)
            y = _moe(vt, w_r, b_rg[i], b_re[i], w1b, w3b, w2b).reshape(bsz, n_lat, dm)
            xl = xs[:, n_ctx:]
            return _layer_norm(dn_alpha * xl + mod_l[:, None, 5 * dm:] * y, ln_g[i, 1], ln_b[i, 1], LN_EPS)
        y = _moe(vmod.reshape(rows, dm), w_r, b_rg[i], b_re[i], w1b, w3b, w2b).reshape(bsz, ltot, dm)
        xs = _layer_norm(dn_alpha * xs + seg(5) * y, ln_g[i, 1], ln_b[i, 1], LN_EPS)
    return xs[:, n_ctx:]
```

```python
import functools
import math

import jax
import jax.numpy as jnp
from jax import lax
from jax.experimental import pallas as pl
from jax.experimental.pallas import tpu as pltpu

F32 = jnp.float32
BF16 = jnp.bfloat16

D_MODEL = 2048
GRID_W = 64
RWKV_HEADS = 12
RWKV_HEAD = 64
RWKV_W = RWKV_HEADS * RWKV_HEAD
DECAY_LORA = 64
AAA_LORA = 64
GATE_LORA = 128
RWKV_GN_EPS = 64e-5
DIFF_HEADS = 6
DIFF_DK = 64
DIFF_DV = 2 * DIFF_DK
DIFF_W = DIFF_HEADS * DIFF_DV
DIFF_SCALE = DIFF_DK ** -0.5
WIN_Q_HEADS = 8
WIN_KV_HEADS = 2
WIN_GROUP = WIN_Q_HEADS // WIN_KV_HEADS
WIN_HEAD = 64
WIN_W = WIN_Q_HEADS * WIN_HEAD
WIN_KV_W = WIN_KV_HEADS * WIN_HEAD
WIN_SCALE = WIN_HEAD ** -0.5
WINDOW = 128
QBLK = WINDOW
MIX_W = RWKV_W + DIFF_W + WIN_W
N_BRANCH = 3
ROPE_BASE = 10000.0
ROPE_AX_FREQS = 16
RWKV_IN = 3 * RWKV_W + 2 * DECAY_LORA + 2 * AAA_LORA + GATE_LORA
DIFF_IN = 3 * DIFF_W
WIN_IN = WIN_W + 2 * WIN_KV_W
DIFF_OFF = RWKV_IN
WIN_OFF = DIFF_OFF + DIFF_IN
GATE_OFF = WIN_OFF + WIN_IN
N_IN = GATE_OFF + N_BRANCH * D_MODEL
N_GROUPS = 4
EXPERTS_PER_GROUP = 8
N_EXPERTS = N_GROUPS * EXPERTS_PER_GROUP
EXPERT_TOP_K = 2
D_EXPERT = D_MODEL // 4
MOE_BLK = 256
ADA_EPS = 1e-6
LN_EPS = 1e-5
NEG_INF = -1e30

LANE = 128
SEG = 256
P_RWKV = 0
P_GATE = 3072
P_DIFF = P_GATE + N_BRANCH * D_MODEL
P_WIN = P_DIFF + DIFF_IN
P_COLS = P_WIN + WIN_IN
CHUNK = 64
VMEM_LIMIT = 56 * 1024 * 1024


def _cparams(sem):
    return pltpu.CompilerParams(dimension_semantics=sem, vmem_limit_bytes=VMEM_LIMIT)


def _dot(a, b):
    return jnp.dot(a.astype(BF16), b.astype(BF16), preferred_element_type=F32)


def _dot_nt(a, b):
    return lax.dot_general(a.astype(BF16), b.astype(BF16), (((1,), (1,)), ((), ())),
                           preferred_element_type=F32)


def _dot_tn(a, b):
    return lax.dot_general(a.astype(BF16), b.astype(BF16), (((0,), (0,)), ((), ())),
                           preferred_element_type=F32)


def _split(x):
    hi = x.astype(BF16)
    lo = (x - hi.astype(F32)).astype(BF16)
    return hi, lo


def _dot3(a, b):
    ah, al = _split(a)
    bh, bl = _split(b)
    d = functools.partial(jnp.dot, preferred_element_type=F32)
    return d(ah, bh) + (d(ah, bl) + d(al, bh))


def _sigmoid(x):
    return 1.0 / (1.0 + jnp.exp(-x))


def _mm_kernel(a_ref, w_ref, o_ref):
    o_ref[...] = _dot(a_ref[...], w_ref[...]).astype(o_ref.dtype)


def _mm(a, w, out_dtype, tm, tn, name):
    m, k = a.shape
    n = w.shape[1]
    return pl.pallas_call(
        _mm_kernel,
        out_shape=jax.ShapeDtypeStruct((m, n), out_dtype),
        grid=(n // tn, m // tm),
        in_specs=[pl.BlockSpec((tm, k), lambda j, i: (i, 0)),
                  pl.BlockSpec((k, tn), lambda j, i: (0, j))],
        out_specs=pl.BlockSpec((tm, tn), lambda j, i: (i, j)),
        compiler_params=_cparams(("arbitrary", "arbitrary")),
        name=name,
    )(a, w)


def _router_kernel(a_ref, w_ref, o_ref):
    o_ref[...] = _dot3(a_ref[...], w_ref[...])


def _router(a, w, tm):
    m, k = a.shape
    n = w.shape[1]
    return pl.pallas_call(
        _router_kernel,
        out_shape=jax.ShapeDtypeStruct((m, n), F32),
        grid=(m // tm,),
        in_specs=[pl.BlockSpec((tm, k), lambda i: (i, 0)),
                  pl.BlockSpec((k, n), lambda i: (0, 0))],
        out_specs=pl.BlockSpec((tm, n), lambda i: (i, 0)),
        compiler_params=_cparams(("arbitrary",)),
        name="router",
    )(a, w)


def _rwkv_kernel(p_ref, hp_ref, hn_ref, mu_ref, vec_ref, wup_ref, aup_ref, gup_ref, msk_ref,
                 y_ref, bv_ref, g_ref, state, *, nc_ctx, nc_lat):
    d = pl.program_id(1)
    i = pl.program_id(2)
    chunk = _rwkv_chunk(d, i, nc_ctx, nc_lat)

    @pl.when(i == 0)
    def _():
        state[...] = jnp.zeros_like(state)

    n = CHUNK
    p = p_ref[...].astype(F32)
    first = jnp.logical_or(chunk == 0, chunk == nc_ctx)
    last = jnp.logical_or(chunk == nc_ctx - 1, chunk == nc_ctx + nc_lat - 1)
    hp = jnp.where(first, 0.0, hp_ref[15:16, :].astype(F32))
    hn = jnp.where(last, 0.0, hn_ref[0:1, :].astype(F32))
    row = lax.broadcasted_iota(jnp.int32, (n, 1), 0)
    prev = jnp.where(row == 0, hp, pltpu.roll(p, 1, axis=0))
    nxt = jnp.where(row == n - 1, hn, pltpu.roll(p, n - 1, axis=0))
    ps = p + mu_ref[0:1, :] * (prev - p) + mu_ref[1:2, :] * (nxt - p)

    r = ps[:, 0:RWKV_W]
    k = ps[:, RWKV_W:2 * RWKV_W]
    v = ps[:, 2 * RWKV_W:3 * RWKV_W]
    o = 3 * RWKV_W
    wd = jnp.tanh(ps[:, o:o + 2 * DECAY_LORA])
    ad = ps[:, o + 2 * DECAY_LORA:o + 2 * DECAY_LORA + 2 * AAA_LORA]
    gd = ps[:, o + 2 * DECAY_LORA + 2 * AAA_LORA:]
    w0 = vec_ref[0, 0:1, :]
    a0 = vec_ref[0, 1:2, :]
    k_k = vec_ref[0, 2:3, :]
    k_a = vec_ref[0, 3:4, :]
    r_k = vec_ref[0, 4:5, :]
    w_log = w0 + _dot(wd, wup_ref[0])
    a = _sigmoid(a0 + _dot(ad, aup_ref[0]))
    g_ref[0] = _dot(_sigmoid(gd), gup_ref[...]).astype(g_ref.dtype)
    logw = -math.exp(-0.5) * _sigmoid(w_log)

    strict = msk_ref[0, 0] > 0.5
    incl_f = msk_ref[0, 1]
    incl = incl_f > 0.5
    lw_hi, lw_lo = _split(logw)
    incl_b = incl_f.astype(BF16)
    cl = (jnp.dot(incl_b, lw_hi, preferred_element_type=F32)
          + jnp.dot(incl_b, lw_lo, preferred_element_type=F32))
    tot = jnp.sum(logw, axis=0, keepdims=True)
    e_in = jnp.exp(cl)
    e_ex = jnp.exp(cl - logw)
    e_inv = jnp.exp(-cl)
    e_end = jnp.exp(tot - cl)
    p_all = jnp.exp(tot)

    eye = (lax.broadcasted_iota(jnp.int32, (n, n), 0) == lax.broadcasted_iota(jnp.int32, (n, n), 1))
    eye_f = eye.astype(F32)
    zeros = jnp.zeros((n, RWKV_HEAD), F32)
    heads = range(RWKV_HEADS)
    sls = [slice(h * RWKV_HEAD, (h + 1) * RWKV_HEAD) for h in heads]
    kk_n = k * k_k
    kk_sq = kk_n * kk_n
    kd_all = k * (1.0 + (a - 1.0) * k_a)
    rkd = r * kd_all * r_k
    inv_norm = [1.0 / jnp.maximum(jnp.sqrt(jnp.sum(kk_sq[:, sl], axis=-1, keepdims=True)), 1e-12) for sl in sls]
    bonus = [jnp.sum(rkd[:, sl], axis=-1, keepdims=True) for sl in sls]
    bv_ref[0] = jnp.concatenate([bonus[h] * v[:, sls[h]] for h in heads], axis=1)
    kk_all = jnp.concatenate([kk_n[:, sls[h]] * inv_norm[h] for h in heads], axis=1)
    bd_all = kk_all * a
    kk_t_all = kk_all * e_ex
    r_t_all = r * e_in
    b_i_all = bd_all * e_inv
    k_i_all = kd_all * e_inv
    k_e_all = kd_all * e_end
    b_e_all = bd_all * e_end

    v_h = [v[:, sl] for sl in sls]
    kk_t = [kk_t_all[:, sl] for sl in sls]
    r_t = [r_t_all[:, sl] for sl in sls]
    gram = [_dot_nt(jnp.concatenate([kk_t[h], r_t[h]], axis=0),
                    jnp.concatenate([b_i_all[:, sls[h]], k_i_all[:, sls[h]]], axis=0)) for h in heads]
    l_k = [jnp.where(strict, gram[h][:n, n:], 0.0) for h in heads]
    pw = [jnp.where(strict, -gram[h][:n, :n], 0.0) for h in heads]
    t_inv = [eye_f + pw[h] for h in heads]
    lkv = [_dot(l_k[h], v_h[h]) for h in heads]
    for _ in range(5):
        pw = [_dot(pw[h], pw[h]) for h in heads]
        t_inv = [t_inv[h] + _dot(t_inv[h], pw[h]) for h in heads]
    tx = [_dot(t_inv[h], jnp.concatenate([kk_t[h], lkv[h]], axis=1)) for h in heads]
    rhs = [jnp.concatenate([jnp.concatenate([zeros, v_h[h]], axis=1), tx[h]], axis=0) for h in heads]
    m_r = [jnp.concatenate([jnp.where(incl, gram[h][n:, n:], 0.0),
                            jnp.where(incl, -gram[h][n:, :n], 0.0)], axis=1) for h in heads]
    top = [_dot(m_r[h], rhs[h]) for h in heads]
    bot_t = [_dot_tn(rhs[h], jnp.concatenate([k_e_all[:, sls[h]], -b_e_all[:, sls[h]]], axis=0))
             for h in heads]
    s0 = [state[h] for h in heads]
    ys = [_dot_nt(r_t[h] + top[h][:, :RWKV_HEAD], s0[h]) + top[h][:, RWKV_HEAD:] for h in heads]
    y_ref[0] = jnp.concatenate(ys, axis=1)
    s1 = [_dot(s0[h], bot_t[h][:RWKV_HEAD]) for h in heads]
    for h in heads:
        state[h] = s0[h] * p_all[:, sls[h]] + s1[h] + bot_t[h][RWKV_HEAD:]


def _rwkv_chunk(d, i, nc_ctx, nc_lat):
    rev = jnp.where(i < nc_ctx, nc_ctx - 1 - i, 2 * nc_ctx + nc_lat - 1 - i)
    return jnp.where(d == 0, i, rev)


def _rwkv_scan(p, mu, vecs, wup, aup, gup, masks, bsz, n_ctx, n_lat):
    ltot = n_ctx + n_lat
    nc_ctx, nc_lat = n_ctx // CHUNK, n_lat // CHUNK
    nc = nc_ctx + nc_lat
    rows = bsz * ltot
    hb = CHUNK // 16
    n_hblk = rows // 16

    def main_map(b, d, i):
        return (b * nc + _rwkv_chunk(d, i, nc_ctx, nc_lat), 0)

    def prev_map(b, d, i):
        return (jnp.maximum((b * nc + _rwkv_chunk(d, i, nc_ctx, nc_lat)) * hb - 1, 0), 0)

    def next_map(b, d, i):
        return (jnp.minimum((b * nc + _rwkv_chunk(d, i, nc_ctx, nc_lat) + 1) * hb, n_hblk - 1), 0)

    def out_map(b, d, i):
        return (d, b * nc + _rwkv_chunk(d, i, nc_ctx, nc_lat), 0)

    out_sd = jax.ShapeDtypeStruct((2, rows, RWKV_W), F32)
    kern = functools.partial(_rwkv_kernel, nc_ctx=nc_ctx, nc_lat=nc_lat)
    return pl.pallas_call(
        kern,
        out_shape=(out_sd, out_sd, jax.ShapeDtypeStruct((2, rows, RWKV_W), BF16)),
        grid=(bsz, 2, nc),
        in_specs=[pl.BlockSpec((CHUNK, RWKV_IN), main_map),
                  pl.BlockSpec((16, RWKV_IN), prev_map),
                  pl.BlockSpec((16, RWKV_IN), next_map),
                  pl.BlockSpec((2, RWKV_IN), lambda b, d, i: (0, 0)),
                  pl.BlockSpec((1, 8, RWKV_W), lambda b, d, i: (d, 0, 0)),
                  pl.BlockSpec((1, 2 * DECAY_LORA, RWKV_W), lambda b, d, i: (d, 0, 0)),
                  pl.BlockSpec((1, 2 * AAA_LORA, RWKV_W), lambda b, d, i: (d, 0, 0)),
                  pl.BlockSpec((GATE_LORA, RWKV_W), lambda b, d, i: (0, 0)),
                  pl.BlockSpec((1, 2, CHUNK, CHUNK), lambda b, d, i: (d, 0, 0, 0))],
        out_specs=(pl.BlockSpec((1, CHUNK, RWKV_W), out_map),
                   pl.BlockSpec((1, CHUNK, RWKV_W), out_map),
                   pl.BlockSpec((1, CHUNK, RWKV_W), out_map)),
        scratch_shapes=[pltpu.VMEM((RWKV_HEADS, RWKV_HEAD, RWKV_HEAD), F32)],
        compiler_params=_cparams(("arbitrary", "arbitrary", "arbitrary")),
        name="rwkv_scan",
    )(p, p, p, mu, vecs, wup, aup, gup, masks)


def _softmax_rows(s):
    m = jnp.max(s, axis=-1, keepdims=True)
    e = jnp.exp(s - m)
    return e / jnp.sum(e, axis=-1, keepdims=True)


def _diff_kernel(q_ref, k_ref, v_ref, par_ref, o_ref, *, n_ctx):
    t = pl.program_id(2)
    lam = par_ref[0:1, :]
    gain = par_ref[1:2, :]

    def attend(nk):
        q = q_ref[...]
        k = k_ref[0:nk, :]
        v = v_ref[0:nk, :]
        s0 = _dot_nt(q[:, :DIFF_DK], k[:, :DIFF_DK]) * DIFF_SCALE
        s1 = _dot_nt(q[:, DIFF_DK:], k[:, DIFF_DK:]) * DIFF_SCALE
        a = _softmax_rows(s0) - lam[:, 0:1] * _softmax_rows(s1)
        o = _dot(a, v)
        o = o * lax.rsqrt(jnp.mean(o * o, axis=-1, keepdims=True) + 1e-5)
        o_ref[...] = (o * gain).astype(o_ref.dtype)

    @pl.when(t * SEG < n_ctx)
    def _():
        attend(n_ctx)

    @pl.when(t * SEG >= n_ctx)
    def _():
        attend(k_ref.shape[0])


def _diff_attn(q, k, p, par, bsz, n_ctx, ltot):
    rows = bsz * ltot
    nt = ltot // SEG
    voff = (P_DIFF + 2 * DIFF_W) // DIFF_DV
    return pl.pallas_call(
        functools.partial(_diff_kernel, n_ctx=n_ctx),
        out_shape=jax.ShapeDtypeStruct((rows, DIFF_W), BF16),
        grid=(bsz, DIFF_HEADS, nt),
        in_specs=[pl.BlockSpec((SEG, DIFF_DV), lambda b, h, t: (b * nt + t, h)),
                  pl.BlockSpec((ltot, DIFF_DV), lambda b, h, t: (b, h)),
                  pl.BlockSpec((ltot, DIFF_DV), lambda b, h, t: (b, voff + h)),
                  pl.BlockSpec((8, DIFF_DV), lambda b, h, t: (0, 0))],
        out_specs=pl.BlockSpec((SEG, DIFF_DV), lambda b, h, t: (b * nt + t, h)),
        compiler_params=_cparams(("arbitrary", "arbitrary", "arbitrary")),
        name="diff_attn",
    )(q, k, p, par)


def _win_kernel(sink_ref, q_ref, kp_ref, kc_ref, kn_ref, kx_ref, vp_ref, vc_ref, vn_ref, vx_ref,
                o_ref, *, n_ctx, n_lat):
    blk = pl.program_id(1)
    ncb = n_ctx // QBLK
    nq = WIN_GROUP * QBLK

    def run(keys, vals, mask):
        outs = []
        for g in range(WIN_KV_HEADS):
            ksl = slice(g * WIN_HEAD, (g + 1) * WIN_HEAD)
            qg = jnp.concatenate(
                [q_ref[:, (g * WIN_GROUP + j) * WIN_HEAD:(g * WIN_GROUP + j + 1) * WIN_HEAD]
                 for j in range(WIN_GROUP)], axis=0)
            s = _dot_nt(qg, keys[:, ksl]) * WIN_SCALE
            if mask is not None:
                s = jnp.where(mask, s, NEG_INF)
            hrow = lax.broadcasted_iota(jnp.int32, (nq, 1), 0) // QBLK
            sink = jnp.zeros((nq, 1), F32)
            for j in range(WIN_GROUP):
                sink = jnp.where(hrow == j, sink_ref[g * WIN_GROUP + j], sink)
            m = jnp.maximum(jnp.max(s, axis=-1, keepdims=True), sink)
            e = jnp.exp(s - m)
            den = jnp.sum(e, axis=-1, keepdims=True) + jnp.exp(sink - m)
            og = _dot(e / den, vals[:, ksl])
            outs.extend(og[j * QBLK:(j + 1) * QBLK] for j in range(WIN_GROUP))
        o_ref[...] = jnp.concatenate(outs, axis=1).astype(o_ref.dtype)

    @pl.when(blk < ncb)
    def _():
        run(kx_ref[...], vx_ref[...], None)

    @pl.when(blk >= ncb)
    def _():
        j = blk - ncb
        keys = jnp.concatenate([kp_ref[...], kc_ref[...], kn_ref[...], kx_ref[...]], axis=0)
        vals = jnp.concatenate([vp_ref[...], vc_ref[...], vn_ref[...], vx_ref[...]], axis=0)
        nk = 3 * QBLK + n_ctx
        qpos = j * QBLK + lax.broadcasted_iota(jnp.int32, (nq, nk), 0) % QBLK
        col = lax.broadcasted_iota(jnp.int32, (nq, nk), 1)
        kpos = (j - 1) * QBLK + col
        band = (jnp.abs(qpos - kpos) <= WINDOW) & (kpos >= 0) & (kpos < n_lat)
        run(keys, vals, band | (col >= 3 * QBLK))


def _win_attn(sink, q, k, p, bsz, n_ctx, n_lat):
    ltot = n_ctx + n_lat
    rows = bsz * ltot
    nb = ltot // QBLK
    ncb = n_ctx // QBLK
    voff = (P_WIN + WIN_W + WIN_KV_W) // WIN_KV_W

    def lat_blk(b, t, shift):
        j = jnp.clip(t - ncb + shift, 0, nb - ncb - 1)
        return b * nb + ncb + j

    kspec = lambda shift, c: pl.BlockSpec((QBLK, WIN_KV_W), lambda b, t, s: (lat_blk(b, t, shift), c))
    xspec = lambda c: pl.BlockSpec((n_ctx, WIN_KV_W), lambda b, t, s: (b * (ltot // n_ctx), c))
    return pl.pallas_call(
        functools.partial(_win_kernel, n_ctx=n_ctx, n_lat=n_lat),
        out_shape=jax.ShapeDtypeStruct((rows, WIN_W), BF16),
        grid_spec=pltpu.PrefetchScalarGridSpec(
            num_scalar_prefetch=1,
            grid=(bsz, nb),
            in_specs=[pl.BlockSpec((QBLK, WIN_W), lambda b, t, s: (b * nb + t, 0)),
                      kspec(-1, 0), kspec(0, 0), kspec(1, 0), xspec(0),
                      kspec(-1, voff), kspec(0, voff), kspec(1, voff), xspec(voff)],
            out_specs=pl.BlockSpec((QBLK, WIN_W), lambda b, t, s: (b * nb + t, 0))),
        compiler_params=_cparams(("arbitrary", "arbitrary")),
        name="win_attn",
    )(sink, q, k, k, k, k, p, p, p, p)


def _merge_kernel(ya_ref, yb_ref, yc_ref, ga_ref, gb_ref, gc_ref, wa_ref, wb_ref, wc_ref, o_ref):
    z = _sigmoid(ga_ref[...].astype(F32)) * _dot(ya_ref[...], wa_ref[...])
    z = z + _sigmoid(gb_ref[...].astype(F32)) * _dot(yb_ref[...], wb_ref[...])
    z = z + _sigmoid(gc_ref[...].astype(F32)) * _dot(yc_ref[...], wc_ref[...])
    o_ref[...] = z.astype(o_ref.dtype)


def _merge(ya, yb, yc, p, wa, wb, wc, tm, tn):
    m = ya.shape[0]
    nj = D_MODEL // tn
    goff = P_GATE // tn
    gspec = lambda br: pl.BlockSpec((tm, tn), lambda j, i: (i, goff + br * nj + j))
    return pl.pallas_call(
        _merge_kernel,
        out_shape=jax.ShapeDtypeStruct((m, D_MODEL), BF16),
        grid=(nj, m // tm),
        in_specs=[pl.BlockSpec((tm, RWKV_W), lambda j, i: (i, 0)),
                  pl.BlockSpec((tm, DIFF_W), lambda j, i: (i, 0)),
                  pl.BlockSpec((tm, WIN_W), lambda j, i: (i, 0)),
                  gspec(0), gspec(1), gspec(2),
                  pl.BlockSpec((RWKV_W, tn), lambda j, i: (0, j)),
                  pl.BlockSpec((DIFF_W, tn), lambda j, i: (0, j)),
                  pl.BlockSpec((WIN_W, tn), lambda j, i: (0, j))],
        out_specs=pl.BlockSpec((tm, tn), lambda j, i: (i, j)),
        compiler_params=_cparams(("arbitrary", "arbitrary")),
        name="merge",
    )(ya, yb, yc, p, p, p, wa, wb, wc)


def _expert_kernel(be_ref, nu_ref, x_ref, gw_ref, w1_ref, w3_ref, w2_ref, o_ref, w1b, w3b, w2b):
    i = pl.program_id(0)
    prev = be_ref[jnp.maximum(i - 1, 0)]

    @pl.when(jnp.logical_or(i == 0, be_ref[i] != prev))
    def _():
        w1b[...] = w1_ref[0].astype(BF16)
        w3b[...] = w3_ref[0].astype(BF16)
        w2b[...] = w2_ref[0].astype(BF16)

    @pl.when(i < nu_ref[0])
    def _():
        x = x_ref[...]
        h1 = _dot(x, w1b[...])
        h = h1 * _sigmoid(h1) * _dot(x, w3b[...])
        o_ref[...] = (_dot(h, w2b[...]) * gw_ref[...]).astype(o_ref.dtype)

    @pl.when(i >= nu_ref[0])
    def _():
        o_ref[...] = jnp.zeros_like(o_ref)


def _experts(blk_expert, n_used, xb, gw, w1, w3, w2):
    rows = xb.shape[0]
    nb = rows // MOE_BLK
    return pl.pallas_call(
        _expert_kernel,
        out_shape=jax.ShapeDtypeStruct((rows, D_MODEL), BF16),
        grid_spec=pltpu.PrefetchScalarGridSpec(
            num_scalar_prefetch=2,
            grid=(nb,),
            in_specs=[pl.BlockSpec((MOE_BLK, D_MODEL), lambda i, be, nu: (i, 0)),
                      pl.BlockSpec((MOE_BLK, 1), lambda i, be, nu: (i, 0)),
                      pl.BlockSpec((1, D_MODEL, D_EXPERT), lambda i, be, nu: (be[i], 0, 0)),
                      pl.BlockSpec((1, D_MODEL, D_EXPERT), lambda i, be, nu: (be[i], 0, 0)),
                      pl.BlockSpec((1, D_EXPERT, D_MODEL), lambda i, be, nu: (be[i], 0, 0))],
            out_specs=pl.BlockSpec((MOE_BLK, D_MODEL), lambda i, be, nu: (i, 0)),
            scratch_shapes=[pltpu.VMEM((D_MODEL, D_EXPERT), BF16),
                            pltpu.VMEM((D_MODEL, D_EXPERT), BF16),
                            pltpu.VMEM((D_EXPERT, D_MODEL), BF16)]),
        compiler_params=_cparams(("arbitrary",)),
        name="experts",
    )(blk_expert, n_used, xb, gw, w1, w3, w2)


def _layer_norm(x, g, b, eps):
    mu = jnp.mean(x, axis=-1, keepdims=True)
    var = jnp.mean(jnp.square(x - mu), axis=-1, keepdims=True)
    y = (x - mu) * lax.rsqrt(var + eps)
    if g is not None:
        y = y * g + b
    return y


def _rope_tables(n_ctx, n_lat):
    rows = n_lat // GRID_W
    row = jnp.repeat(jnp.arange(rows), GRID_W).astype(F32)
    col = (jnp.arange(rows * GRID_W) % GRID_W).astype(F32)
    inv = ROPE_BASE ** (-jnp.arange(ROPE_AX_FREQS, dtype=F32) / ROPE_AX_FREQS)
    ang = jnp.stack([row[:, None] * inv, col[:, None] * inv], axis=1)
    cos, sin = jnp.cos(ang), jnp.sin(ang)
    cos4 = jnp.stack([cos, cos], axis=2).reshape(n_lat, 4 * ROPE_AX_FREQS)
    sin4 = jnp.stack([-sin, sin], axis=2).reshape(n_lat, 4 * ROPE_AX_FREQS)
    cos4 = jnp.concatenate([jnp.ones((n_ctx, 64), F32), cos4], axis=0)
    sin4 = jnp.concatenate([jnp.zeros((n_ctx, 64), F32), sin4], axis=0)
    return cos4, sin4


def _rope(x, cos4, sin4):
    b, l, w = x.shape
    xf = x.astype(F32).reshape(b, l, w // 32, 2, ROPE_AX_FREQS)
    sw = xf[:, :, :, ::-1, :].reshape(b, l, w // 64, 64)
    xf = xf.reshape(b, l, w // 64, 64)
    out = xf * cos4[None, :, None, :] + sw * sin4[None, :, None, :]
    return out.reshape(b, l, w).astype(BF16)


def _route(logits, b_rg, b_re):
    t = logits.shape[0]
    pg = jax.nn.softmax(logits[:, :N_GROUPS] + b_rg, axis=-1)
    pg_top, g_idx = lax.top_k(pg, 1)
    le = (logits[:, N_GROUPS:N_GROUPS + N_EXPERTS] + b_re).reshape(t, N_GROUPS, EXPERTS_PER_GROUP)
    le_sel = jnp.take_along_axis(le, g_idx[:, :, None], axis=1)[:, 0]
    pe_top, e_loc = lax.top_k(jax.nn.softmax(le_sel, axis=-1), EXPERT_TOP_K)
    gates = pg_top * pe_top / jnp.sum(pe_top, axis=-1, keepdims=True)
    return g_idx * EXPERTS_PER_GROUP + e_loc, gates


def _moe(v, w_r, b_rg, b_re, w1, w3, w2):
    t = v.shape[0]
    logits = _router(v, w_r, SEG)
    experts, gates = _route(logits, b_rg, b_re)
    k = EXPERT_TOP_K
    a = t * k
    e_n = N_EXPERTS
    nb = -(-a // MOE_BLK) + e_n
    e_flat = experts.reshape(a).astype(jnp.int32)
    order = jnp.argsort(e_flat).astype(jnp.int32)
    e_s = e_flat[order]
    start = jnp.searchsorted(e_s, jnp.arange(e_n + 1, dtype=jnp.int32), side='left').astype(jnp.int32)
    counts = start[1:] - start[:-1]
    start = start[:-1]
    padded = (counts + MOE_BLK - 1) // MOE_BLK * MOE_BLK
    pad_end = jnp.cumsum(padded)
    pad_start = pad_end - padded
    dest = (pad_start[e_s] + jnp.arange(a, dtype=jnp.int32) - start[e_s]).astype(jnp.int32)
    blk_expert = jnp.minimum(jnp.searchsorted(pad_end, jnp.arange(nb) * MOE_BLK, side='right'),
                             e_n - 1).astype(jnp.int32)
    n_used = (pad_end[-1:] // MOE_BLK).astype(jnp.int32)
    slot = jnp.arange(nb * MOE_BLK, dtype=jnp.int32)
    slot_e = jnp.repeat(blk_expert, MOE_BLK)
    rank = slot - pad_start[slot_e]
    filled = jnp.logical_and(rank < counts[slot_e], slot < pad_end[-1])
    src = order[jnp.clip(start[slot_e] + rank, 0, a - 1)]
    slot_tok = jnp.where(filled, src // k, t)
    slot_gate = jnp.where(filled, gates.reshape(a)[src], 0.0)
    v_pad = jnp.concatenate([v.astype(BF16), jnp.zeros((1, D_MODEL), BF16)], axis=0)
    xb = v_pad[slot_tok]
    ys = _experts(blk_expert, n_used, xb, slot_gate[:, None], w1, w3, w2)
    pos = jnp.zeros((a,), jnp.int32).at[order].set(dest)
    return lax.optimization_barrier(jnp.take(ys, pos, axis=0)).reshape(t, k, D_MODEL)


def kernel(x, c, ctx, c_ctx, w_mod, b_mod, w_in, rwkv_mu, rwkv_w0, rwkv_w_up, rwkv_a0, rwkv_a_up,
           rwkv_g_up, rwkv_kvec, rwkv_lnx, diff_lam, diff_subln, win_sink, w_branch, w_out, ln_g, ln_b,
           w_rg, b_rg, w_re, b_re, w1, w3, w2):
    bsz, n_lat, dm = x.shape
    n_ctx = ctx.shape[1]
    depth = w_mod.shape[0]
    ltot = n_ctx + n_lat
    rows = bsz * ltot
    dn_alpha = (2 * depth) ** 0.25
    assert dm == D_MODEL and n_ctx % SEG == 0 and n_lat % SEG == 0 and ltot % n_ctx == 0

    cos4, sin4 = _rope_tables(n_ctx, n_lat)
    fwd = jnp.tril(jnp.ones((CHUNK, CHUNK), F32))
    masks = jnp.stack([jnp.stack([fwd - jnp.eye(CHUNK, dtype=F32), fwd]),
                       jnp.stack([fwd.T - jnp.eye(CHUNK, dtype=F32), fwd.T])])

    xs = jnp.concatenate([ctx, x], axis=1)
    is_ctx = (jnp.arange(ltot) < n_ctx)[None, :, None]
    cvec = jnp.concatenate([c_ctx[None, :], c], axis=0)
    cpad = jnp.zeros((32, dm), F32).at[:bsz + 1].set(jax.nn.silu(cvec))

    for i in range(depth):
        last = i == depth - 1
        lam_init = 0.8 - 0.6 * math.exp(-0.3 * i)
        mod = _mm(cpad, w_mod[i], F32, 32, 1024, "mod")[:bsz + 1] + b_mod[i]
        mod_c = mod[0]
        mod_l = mod[1:]
        def seg(idx):
            return jnp.where(is_ctx, mod_c[None, None, idx * dm:(idx + 1) * dm],
                             mod_l[:, None, idx * dm:(idx + 1) * dm])

        u = (_layer_norm(xs, None, None, ADA_EPS) * (1 + seg(1)) + seg(0)).astype(BF16)
        w_in_p = jnp.concatenate([w_in[i][:, :RWKV_IN], jnp.zeros((dm, P_GATE - RWKV_IN), F32),
                                  w_in[i][:, GATE_OFF:], w_in[i][:, DIFF_OFF:GATE_OFF]], axis=1).astype(BF16)
        p = _mm(u.reshape(rows, dm), w_in_p, BF16, 512, 1024, "in_proj")

        vecs = jnp.zeros((2, 8, RWKV_W), F32)
        vecs = vecs.at[:, 0].set(rwkv_w0[i]).at[:, 1].set(rwkv_a0[i])
        vecs = vecs.at[:, 2:5].set(jnp.broadcast_to(rwkv_kvec[i][None], (2, 3, RWKV_W)))
        wup = jnp.zeros((2, 2 * DECAY_LORA, RWKV_W), BF16)
        aup = jnp.zeros((2, 2 * AAA_LORA, RWKV_W), BF16)
        for d in range(2):
            wup = wup.at[d, d * DECAY_LORA:(d + 1) * DECAY_LORA].set(rwkv_w_up[i, d].astype(BF16))
            aup = aup.at[d, d * AAA_LORA:(d + 1) * AAA_LORA].set(rwkv_a_up[i, d].astype(BF16))
        y2, bv2, g2 = _rwkv_scan(p, rwkv_mu[i], vecs, wup, aup, rwkv_g_up[i].astype(BF16), masks,
                                 bsz, n_ctx, n_lat)
        yh = (y2[0] + y2[1]).reshape(rows, RWKV_HEADS, RWKV_HEAD)
        mu_h = jnp.mean(yh, axis=-1, keepdims=True)
        var_h = jnp.mean(jnp.square(yh - mu_h), axis=-1, keepdims=True)
        yn = ((yh - mu_h) * lax.rsqrt(var_h + RWKV_GN_EPS)).reshape(rows, RWKV_W)
        yn = yn * rwkv_lnx[i, 0] + rwkv_lnx[i, 1]
        ya = ((yn + bv2[0] + bv2[1]) * g2[0].astype(F32)).astype(BF16)

        p3 = p.reshape(bsz, ltot, P_COLS)
        dq = _rope(p3[..., P_DIFF:P_DIFF + DIFF_W], cos4, sin4).reshape(rows, DIFF_W)
        dk = _rope(p3[..., P_DIFF + DIFF_W:P_DIFF + 2 * DIFF_W], cos4, sin4).reshape(rows, DIFF_W)
        wq = _rope(p3[..., P_WIN:P_WIN + WIN_W], cos4, sin4).reshape(rows, WIN_W)
        wk = _rope(p3[..., P_WIN + WIN_W:P_WIN + WIN_W + WIN_KV_W], cos4, sin4).reshape(rows, WIN_KV_W)

        lf = diff_lam[i]
        lam = jnp.exp(jnp.sum(lf[0] * lf[1])) - jnp.exp(jnp.sum(lf[2] * lf[3])) + lam_init
        par = jnp.zeros((8, DIFF_DV), F32).at[0].set(lam).at[1].set(diff_subln[i] * (1 - lam_init))
        yb = _diff_attn(dq, dk, p, par, bsz, n_ctx, ltot)
        yc = _win_attn(win_sink[i], wq, wk, p, bsz, n_ctx, n_lat)

        wbr = w_branch[i].astype(BF16)
        z = _merge(ya, yb, yc, p, wbr[:RWKV_W], wbr[RWKV_W:RWKV_W + DIFF_W], wbr[RWKV_W + DIFF_W:],
                   512, 1024)
        m = _mm(z, w_out[i].astype(BF16), F32, 512, 1024, "out_proj").reshape(bsz, ltot, dm)
        xs = _layer_norm(dn_alpha * xs + seg(2) * m, ln_g[i, 0], ln_b[i, 0], LN_EPS)

        vmod = _layer_norm(xs, None, None, ADA_EPS) * (1 + seg(4)) + seg(3)
        w_r = jnp.zeros((dm, LANE), F32).at[:, :N_GROUPS].set(w_rg[i])
        w_r = w_r.at[:, N_GROUPS:N_GROUPS + N_EXPERTS].set(w_re[i])
        if last:
            vt = vmod[:, n_ctx:].reshape(bsz * n_lat, dm)
            y = _moe(vt, w_r, b_rg[i], b_re[i], w1[i], w3[i], w2[i]).reshape(bsz, n_lat, EXPERT_TOP_K, dm)
            y = y[:, :, 0].astype(F32) + y[:, :, 1].astype(F32)
            xl = xs[:, n_ctx:]
            return _layer_norm(dn_alpha * xl + mod_l[:, None, 5 * dm:] * y, ln_g[i, 1], ln_b[i, 1], LN_EPS)
        y = _moe(vmod.reshape(rows, dm), w_r, b_rg[i], b_re[i], w1[i], w3[i], w2[i])
        y = y.reshape(bsz, ltot, EXPERT_TOP_K, dm)
        y = y[:, :, 0].astype(F32) + y[:, :, 1].astype(F32)
        xs = _layer_norm(dn_alpha * xs + seg(5) * y, ln_g[i, 1], ln_b[i, 1], LN_EPS)
    return xs[:, n_ctx:]
```

```python
import functools
import math

import jax
import jax.numpy as jnp
from jax import lax
from jax.experimental import pallas as pl
from jax.experimental.pallas import tpu as pltpu

F32 = jnp.float32
BF16 = jnp.bfloat16

D_MODEL = 2048
GRID_W = 64
RWKV_HEADS = 12
RWKV_HEAD = 64
RWKV_W = RWKV_HEADS * RWKV_HEAD
DECAY_LORA = 64
AAA_LORA = 64
GATE_LORA = 128
RWKV_GN_EPS = 64e-5
DIFF_HEADS = 6
DIFF_DK = 64
DIFF_DV = 2 * DIFF_DK
DIFF_W = DIFF_HEADS * DIFF_DV
DIFF_SCALE = DIFF_DK ** -0.5
WIN_Q_HEADS = 8
WIN_KV_HEADS = 2
WIN_GROUP = WIN_Q_HEADS // WIN_KV_HEADS
WIN_HEAD = 64
WIN_W = WIN_Q_HEADS * WIN_HEAD
WIN_KV_W = WIN_KV_HEADS * WIN_HEAD
WIN_SCALE = WIN_HEAD ** -0.5
WINDOW = 128
QBLK = WINDOW
MIX_W = RWKV_W + DIFF_W + WIN_W
N_BRANCH = 3
ROPE_BASE = 10000.0
ROPE_AX_FREQS = 16
RWKV_IN = 3 * RWKV_W + 2 * DECAY_LORA + 2 * AAA_LORA + GATE_LORA
DIFF_IN = 3 * DIFF_W
WIN_IN = WIN_W + 2 * WIN_KV_W
DIFF_OFF = RWKV_IN
WIN_OFF = DIFF_OFF + DIFF_IN
GATE_OFF = WIN_OFF + WIN_IN
N_IN = GATE_OFF + N_BRANCH * D_MODEL
N_GROUPS = 4
EXPERTS_PER_GROUP = 8
N_EXPERTS = N_GROUPS * EXPERTS_PER_GROUP
EXPERT_TOP_K = 2
D_EXPERT = D_MODEL // 4
MOE_BLK = 256
ADA_EPS = 1e-6
LN_EPS = 1e-5
NEG_INF = -1e30
LOG2E = math.log2(math.e)

LANE = 128
SEG = 256
P_RWKV = 0
P_GATE = 3072
P_DIFF = P_GATE + N_BRANCH * D_MODEL
P_WIN = P_DIFF + DIFF_IN
P_COLS = P_WIN + WIN_IN
CHUNK = 64
VMEM_LIMIT = 56 * 1024 * 1024


def _cparams(sem):
    return pltpu.CompilerParams(dimension_semantics=sem, vmem_limit_bytes=VMEM_LIMIT)


def _dot(a, b):
    return jnp.dot(a.astype(BF16), b.astype(BF16), preferred_element_type=F32)


def _dot_nt(a, b):
    return lax.dot_general(a.astype(BF16), b.astype(BF16), (((1,), (1,)), ((), ())),
                           preferred_element_type=F32)


def _dot_tn(a, b):
    return lax.dot_general(a.astype(BF16), b.astype(BF16), (((0,), (0,)), ((), ())),
                           preferred_element_type=F32)


def _split(x):
    hi = x.astype(BF16)
    lo = (x - hi.astype(F32)).astype(BF16)
    return hi, lo


def _dot3(a, b):
    ah, al = _split(a)
    bh, bl = _split(b)
    d = functools.partial(jnp.dot, preferred_element_type=F32)
    return d(ah, bh) + (d(ah, bl) + d(al, bh))


def _sigmoid(x):
    return 1.0 / (1.0 + jnp.exp(-x))


def _mm_kernel(a_ref, w_ref, o_ref):
    o_ref[...] = _dot(a_ref[...], w_ref[...]).astype(o_ref.dtype)


def _mm(a, w, layer, out_dtype, tm, tn, name):
    m, k = a.shape
    n = w.shape[2]
    return pl.pallas_call(
        _mm_kernel,
        out_shape=jax.ShapeDtypeStruct((m, n), out_dtype),
        grid=(n // tn, m // tm),
        in_specs=[pl.BlockSpec((tm, k), lambda j, i: (i, 0)),
                  pl.BlockSpec((None, k, tn), lambda j, i: (layer, 0, j))],
        out_specs=pl.BlockSpec((tm, tn), lambda j, i: (i, j)),
        compiler_params=_cparams(("arbitrary", "arbitrary")),
        name=name,
    )(a, w)


def _norm_rows(x, eps):
    mu = jnp.mean(x, axis=-1, keepdims=True)
    xc = x - mu
    return xc * lax.rsqrt(jnp.mean(xc * xc, axis=-1, keepdims=True) + eps)


def _route_rows(logits):
    col = lax.broadcasted_iota(jnp.int32, logits.shape, 1).astype(F32)
    big = float(LANE)
    is_g = col < N_GROUPS
    lg = jnp.where(is_g, logits, NEG_INF)
    g_max = jnp.max(lg, axis=-1, keepdims=True)
    g_sum = jnp.sum(jnp.where(is_g, jnp.exp(lg - g_max), 0.0), axis=-1, keepdims=True)
    pg_top = 1.0 / g_sum
    g_idx = jnp.min(jnp.where(is_g & (lg == g_max), col, big), axis=-1, keepdims=True)
    lo = N_GROUPS + EXPERTS_PER_GROUP * g_idx
    sel = (col >= lo) & (col < lo + EXPERTS_PER_GROUP)
    le = jnp.where(sel, logits, NEG_INF)
    m1 = jnp.max(le, axis=-1, keepdims=True)
    den = jnp.sum(jnp.where(sel, jnp.exp(le - m1), 0.0), axis=-1, keepdims=True)
    i1 = jnp.min(jnp.where(sel & (le == m1), col, big), axis=-1, keepdims=True)
    rest = sel & (col != i1)
    le2 = jnp.where(rest, logits, NEG_INF)
    m2 = jnp.max(le2, axis=-1, keepdims=True)
    i2 = jnp.min(jnp.where(rest & (le2 == m2), col, big), axis=-1, keepdims=True)
    p1 = 1.0 / den
    p2 = jnp.exp(m2 - m1) / den
    tot = p1 + p2
    out = jnp.where(col == 0.0, i1 - N_GROUPS, 0.0)
    out = jnp.where(col == 1.0, i2 - N_GROUPS, out)
    out = jnp.where(col == 2.0, pg_top * p1 / tot, out)
    return jnp.where(col == 3.0, pg_top * p2 / tot, out)


def _mod_kernel(x_ref, tab_ref, u_ref):
    u_ref[...] = (_norm_rows(x_ref[...], ADA_EPS) * (1.0 + tab_ref[0, 2:3, :]) + tab_ref[0, 1:2, :]).astype(u_ref.dtype)


def _modulate0(x, tab, nt):
    rows, dm = x.shape
    return pl.pallas_call(
        _mod_kernel,
        out_shape=jax.ShapeDtypeStruct((rows, dm), BF16),
        grid=(rows // SEG,),
        in_specs=[pl.BlockSpec((SEG, dm), lambda i: (i, 0)),
                  pl.BlockSpec((1, 8, dm), lambda i: (2 * (i // nt) + jnp.minimum(i % nt, 1), 0, 0))],
        out_specs=pl.BlockSpec((SEG, dm), lambda i: (i, 0)),
        compiler_params=_cparams(("arbitrary",)),
        name="modulate0",
    )(x, tab)


def _mix_out_kernel(z_ref, x_ref, w_ref, tab_ref, lnp_ref, wr_ref, br_ref, xo_ref, vo_ref, ro_ref, *, alpha):
    m = _dot(z_ref[...], w_ref[...])
    xn = _norm_rows(alpha * x_ref[...] + tab_ref[0, 0:1, :] * m, LN_EPS) * lnp_ref[0:1, :] + lnp_ref[1:2, :]
    xo_ref[...] = xn
    v = _norm_rows(xn, ADA_EPS) * (1.0 + tab_ref[0, 2:3, :]) + tab_ref[0, 1:2, :]
    vo_ref[...] = v.astype(vo_ref.dtype)
    ro_ref[...] = _route_rows(_dot3(v, wr_ref[...]) + br_ref[0:1, :])


def _mix_out(z, x, w_out, layer, tab, lnp, w_r, b_r, nt, alpha):
    rows, dm = x.shape
    row = lambda i: (i, 0)
    const = lambda i: (0, 0)
    return pl.pallas_call(
        functools.partial(_mix_out_kernel, alpha=alpha),
        out_shape=(jax.ShapeDtypeStruct((rows, dm), F32), jax.ShapeDtypeStruct((rows, dm), BF16),
                   jax.ShapeDtypeStruct((rows, LANE), F32)),
        grid=(rows // SEG,),
        in_specs=[pl.BlockSpec((SEG, dm), row), pl.BlockSpec((SEG, dm), row),
                  pl.BlockSpec((None, dm, dm), lambda i: (layer, 0, 0)),
                  pl.BlockSpec((1, 8, dm), lambda i: (2 * (i // nt) + jnp.minimum(i % nt, 1), 0, 0)),
                  pl.BlockSpec((8, dm), const), pl.BlockSpec((dm, LANE), const), pl.BlockSpec((8, LANE), const)],
        out_specs=(pl.BlockSpec((SEG, dm), row), pl.BlockSpec((SEG, dm), row), pl.BlockSpec((SEG, LANE), row)),
        compiler_params=_cparams(("arbitrary",)),
        name="mix_out",
    )(z, x, w_out, tab, lnp, w_r, b_r)


def _moe_out_kernel(x_ref, y0_ref, y1_ref, tab_ref, lnp_ref, xo_ref, *u_ref, alpha):
    y = y0_ref[0].astype(F32) + y1_ref[0].astype(F32)
    xn = _norm_rows(alpha * x_ref[...] + tab_ref[0, 0:1, :] * y, LN_EPS) * lnp_ref[0:1, :] + lnp_ref[1:2, :]
    xo_ref[...] = xn
    if u_ref:
        u_ref[0][...] = (_norm_rows(xn, ADA_EPS) * (1.0 + tab_ref[0, 2:3, :]) + tab_ref[0, 1:2, :]).astype(BF16)


def _moe_out(x, yg, tab, lnp, bsz, nt, skip, alpha, emit_u):
    dm = x.shape[1]
    nk = nt - skip
    rows = bsz * nk * SEG
    xmap = lambda b, t: (b * nt + skip + t, 0)
    omap = lambda b, t: (b * nk + t, 0)
    out_shape = [jax.ShapeDtypeStruct((rows, dm), F32)]
    out_specs = [pl.BlockSpec((SEG, dm), omap)]
    if emit_u:
        out_shape.append(jax.ShapeDtypeStruct((rows, dm), BF16))
        out_specs.append(pl.BlockSpec((SEG, dm), omap))
    return pl.pallas_call(
        functools.partial(_moe_out_kernel, alpha=alpha),
        out_shape=tuple(out_shape),
        grid=(bsz, nk),
        in_specs=[pl.BlockSpec((SEG, dm), xmap),
                  pl.BlockSpec((1, SEG, dm), lambda b, t: (0, b * nk + t, 0)),
                  pl.BlockSpec((1, SEG, dm), lambda b, t: (1, b * nk + t, 0)),
                  pl.BlockSpec((1, 8, dm), lambda b, t: (2 * b + jnp.minimum(skip + t, 1), 0, 0)),
                  pl.BlockSpec((8, dm), lambda b, t: (0, 0))],
        out_specs=tuple(out_specs),
        compiler_params=_cparams(("arbitrary", "arbitrary")),
        name="moe_out",
    )(x, yg, yg, tab, lnp)


def _rope_kernel(pd_ref, pw_ref, cs_ref, dqk_ref, wqk_ref):
    cos = cs_ref[0]
    sin = cs_ref[1]
    first_half = lax.broadcasted_iota(jnp.int32, cos.shape, 1) % (2 * ROPE_AX_FREQS) < ROPE_AX_FREQS

    def rot(x, scale):
        x = x.astype(F32)
        sw = jnp.where(first_half, pltpu.roll(x, LANE - ROPE_AX_FREQS, axis=1), pltpu.roll(x, ROPE_AX_FREQS, axis=1))
        return ((x * cos + sw * sin) * scale).astype(BF16)

    for c in range(2 * DIFF_W // LANE):
        scale = DIFF_SCALE * LOG2E if c < DIFF_W // LANE else 1.0
        dqk_ref[:, c * LANE:(c + 1) * LANE] = rot(pd_ref[:, c * LANE:(c + 1) * LANE], scale)
    for c in range((WIN_W + WIN_KV_W) // LANE):
        scale = WIN_SCALE * LOG2E if c < WIN_W // LANE else 1.0
        wqk_ref[:, c * LANE:(c + 1) * LANE] = rot(pw_ref[:, c * LANE:(c + 1) * LANE], scale)


def _rope(p, cs, nt):
    rows = p.shape[0]
    wd, ww = 2 * DIFF_W, WIN_W + WIN_KV_W
    return pl.pallas_call(
        _rope_kernel,
        out_shape=(jax.ShapeDtypeStruct((rows, wd), BF16), jax.ShapeDtypeStruct((rows, ww), BF16)),
        grid=(rows // SEG,),
        in_specs=[pl.BlockSpec((SEG, wd), lambda i: (i, P_DIFF // wd)),
                  pl.BlockSpec((SEG, ww), lambda i: (i, P_WIN // ww)),
                  pl.BlockSpec((2, SEG, LANE), lambda i: (0, i % nt, 0))],
        out_specs=(pl.BlockSpec((SEG, wd), lambda i: (i, 0)), pl.BlockSpec((SEG, ww), lambda i: (i, 0))),
        compiler_params=_cparams(("arbitrary",)),
        name="rope",
    )(p, p, cs)


def _readout_kernel(y_ref, bv_ref, g_ref, lnx_ref, avg_ref, o_ref):
    avg = avg_ref[...]

    def head_mean(t):
        hi, lo = _split(t)
        return jnp.dot(hi, avg, preferred_element_type=F32) + jnp.dot(lo, avg, preferred_element_type=F32)

    y = y_ref[0] + y_ref[1]
    dev = y - head_mean(y)
    yn = dev * lax.rsqrt(head_mean(dev * dev) + RWKV_GN_EPS) * lnx_ref[0:1, :] + lnx_ref[1:2, :]
    bonus = bv_ref[0].astype(F32) + bv_ref[1].astype(F32)
    o_ref[...] = ((yn + bonus) * g_ref[0].astype(F32)).astype(o_ref.dtype)


def _readout(y2, bv2, g2, lnx, avg, tm):
    rows = y2.shape[1]
    pair = lambda i: (0, i, 0)
    return pl.pallas_call(
        _readout_kernel,
        out_shape=jax.ShapeDtypeStruct((rows, RWKV_W), BF16),
        grid=(rows // tm,),
        in_specs=[pl.BlockSpec((2, tm, RWKV_W), pair), pl.BlockSpec((2, tm, RWKV_W), pair),
                  pl.BlockSpec((1, tm, RWKV_W), pair),
                  pl.BlockSpec((2, RWKV_W), lambda i: (0, 0)),
                  pl.BlockSpec((RWKV_W, RWKV_W), lambda i: (0, 0))],
        out_specs=pl.BlockSpec((tm, RWKV_W), lambda i: (i, 0)),
        compiler_params=_cparams(("arbitrary",)),
        name="rwkv_readout",
    )(y2, bv2, g2, lnx, avg)


def _rwkv_kernel(p_ref, hp_ref, hn_ref, mu_ref, vec_ref, wup_ref, aup_ref, gup_ref, msk_ref,
                 y_ref, bv_ref, g_ref, state, *, nc_ctx, nc_lat):
    d = pl.program_id(1)
    i = pl.program_id(2)
    chunk = _rwkv_chunk(d, i, nc_ctx, nc_lat)

    @pl.when(i == 0)
    def _():
        state[...] = jnp.zeros_like(state)

    n = CHUNK
    p = p_ref[...].astype(F32)
    first = jnp.logical_or(chunk == 0, chunk == nc_ctx)
    last = jnp.logical_or(chunk == nc_ctx - 1, chunk == nc_ctx + nc_lat - 1)
    hp = jnp.where(first, 0.0, hp_ref[15:16, :].astype(F32))
    hn = jnp.where(last, 0.0, hn_ref[0:1, :].astype(F32))
    row = lax.broadcasted_iota(jnp.int32, (n, 1), 0)
    prev = jnp.where(row == 0, hp, pltpu.roll(p, 1, axis=0))
    nxt = jnp.where(row == n - 1, hn, pltpu.roll(p, n - 1, axis=0))
    ps = p + mu_ref[0:1, :] * (prev - p) + mu_ref[1:2, :] * (nxt - p)

    r = ps[:, 0:RWKV_W]
    k = ps[:, RWKV_W:2 * RWKV_W]
    v = ps[:, 2 * RWKV_W:3 * RWKV_W]
    o = 3 * RWKV_W
    wd = jnp.tanh(ps[:, o:o + 2 * DECAY_LORA])
    ad = ps[:, o + 2 * DECAY_LORA:o + 2 * DECAY_LORA + 2 * AAA_LORA]
    gd = ps[:, o + 2 * DECAY_LORA + 2 * AAA_LORA:]
    w0 = vec_ref[0, 0:1, :]
    a0 = vec_ref[0, 1:2, :]
    k_k = vec_ref[0, 2:3, :]
    k_a = vec_ref[0, 3:4, :]
    r_k = vec_ref[0, 4:5, :]
    w_log = w0 + _dot(wd, wup_ref[0])
    a = _sigmoid(a0 + _dot(ad, aup_ref[0]))
    g_ref[0] = _dot(_sigmoid(gd), gup_ref[...]).astype(g_ref.dtype)
    logw = -math.exp(-0.5) * _sigmoid(w_log)

    strict = msk_ref[0, 0] > 0.5
    incl_f = msk_ref[0, 1]
    incl = incl_f > 0.5
    lw_hi, lw_lo = _split(logw)
    incl_b = incl_f.astype(BF16)
    cl = (jnp.dot(incl_b, lw_hi, preferred_element_type=F32)
          + jnp.dot(incl_b, lw_lo, preferred_element_type=F32))
    tot = jnp.sum(logw, axis=0, keepdims=True)
    e_in = jnp.exp(cl)
    e_ex = jnp.exp(cl - logw)
    e_inv = jnp.exp(-cl)
    e_end = jnp.exp(tot - cl)
    p_all = jnp.exp(tot)

    eye = (lax.broadcasted_iota(jnp.int32, (n, n), 0) == lax.broadcasted_iota(jnp.int32, (n, n), 1))
    eye_f = eye.astype(F32)
    zeros = jnp.zeros((n, RWKV_HEAD), F32)
    heads = range(RWKV_HEADS)
    sls = [slice(h * RWKV_HEAD, (h + 1) * RWKV_HEAD) for h in heads]
    kk_n = k * k_k
    kk_sq = kk_n * kk_n
    kd_all = k * (1.0 + (a - 1.0) * k_a)
    rkd = r * kd_all * r_k
    inv_norm = [1.0 / jnp.maximum(jnp.sqrt(jnp.sum(kk_sq[:, sl], axis=-1, keepdims=True)), 1e-12) for sl in sls]
    bonus = [jnp.sum(rkd[:, sl], axis=-1, keepdims=True) for sl in sls]
    bv_ref[0] = jnp.concatenate([bonus[h] * v[:, sls[h]] for h in heads], axis=1).astype(bv_ref.dtype)
    kk_all = jnp.concatenate([kk_n[:, sls[h]] * inv_norm[h] for h in heads], axis=1)
    bd_all = kk_all * a
    kk_t_all = kk_all * e_ex
    r_t_all = r * e_in
    b_i_all = bd_all * e_inv
    k_i_all = kd_all * e_inv
    k_e_all = kd_all * e_end
    b_e_all = bd_all * e_end

    v_h = [v[:, sl] for sl in sls]
    kk_t = [kk_t_all[:, sl] for sl in sls]
    r_t = [r_t_all[:, sl] for sl in sls]
    gram = [_dot_nt(jnp.concatenate([kk_t[h], r_t[h]], axis=0),
                    jnp.concatenate([b_i_all[:, sls[h]], k_i_all[:, sls[h]]], axis=0)) for h in heads]
    l_k = [jnp.where(strict, gram[h][:n, n:], 0.0) for h in heads]
    pw = [jnp.where(strict, -gram[h][:n, :n], 0.0) for h in heads]
    t_inv = [eye_f + pw[h] for h in heads]
    lkv = [_dot(l_k[h], v_h[h]) for h in heads]
    for _ in range(5):
        pw = [_dot(pw[h], pw[h]) for h in heads]
        t_inv = [t_inv[h] + _dot(t_inv[h], pw[h]) for h in heads]
    tx = [_dot(t_inv[h], jnp.concatenate([kk_t[h], lkv[h]], axis=1)) for h in heads]
    rhs = [jnp.concatenate([jnp.concatenate([zeros, v_h[h]], axis=1), tx[h]], axis=0) for h in heads]
    m_r = [jnp.concatenate([jnp.where(incl, gram[h][n:, n:], 0.0),
                            jnp.where(incl, -gram[h][n:, :n], 0.0)], axis=1) for h in heads]
    top = [_dot(m_r[h], rhs[h]) for h in heads]
    bot_t = [_dot_tn(rhs[h], jnp.concatenate([k_e_all[:, sls[h]], -b_e_all[:, sls[h]]], axis=0))
             for h in heads]
    s0 = [state[h] for h in heads]
    ys = [_dot_nt(r_t[h] + top[h][:, :RWKV_HEAD], s0[h]) + top[h][:, RWKV_HEAD:] for h in heads]
    y_ref[0] = jnp.concatenate(ys, axis=1)
    s1 = [_dot(s0[h], bot_t[h][:RWKV_HEAD]) for h in heads]
    for h in heads:
        state[h] = s0[h] * p_all[:, sls[h]] + s1[h] + bot_t[h][RWKV_HEAD:]


def _rwkv_chunk(d, i, nc_ctx, nc_lat):
    rev = jnp.where(i < nc_ctx, nc_ctx - 1 - i, 2 * nc_ctx + nc_lat - 1 - i)
    return jnp.where(d == 0, i, rev)


def _rwkv_scan(p, mu, vecs, wup, aup, gup, masks, bsz, n_ctx, n_lat):
    ltot = n_ctx + n_lat
    nc_ctx, nc_lat = n_ctx // CHUNK, n_lat // CHUNK
    nc = nc_ctx + nc_lat
    rows = bsz * ltot
    hb = CHUNK // 16
    n_hblk = rows // 16

    def main_map(b, d, i):
        return (b * nc + _rwkv_chunk(d, i, nc_ctx, nc_lat), 0)

    def prev_map(b, d, i):
        return (jnp.maximum((b * nc + _rwkv_chunk(d, i, nc_ctx, nc_lat)) * hb - 1, 0), 0)

    def next_map(b, d, i):
        return (jnp.minimum((b * nc + _rwkv_chunk(d, i, nc_ctx, nc_lat) + 1) * hb, n_hblk - 1), 0)

    def out_map(b, d, i):
        return (d, b * nc + _rwkv_chunk(d, i, nc_ctx, nc_lat), 0)

    out_bf = jax.ShapeDtypeStruct((2, rows, RWKV_W), BF16)
    kern = functools.partial(_rwkv_kernel, nc_ctx=nc_ctx, nc_lat=nc_lat)
    return pl.pallas_call(
        kern,
        out_shape=(jax.ShapeDtypeStruct((2, rows, RWKV_W), F32), out_bf, out_bf),
        grid=(bsz, 2, nc),
        in_specs=[pl.BlockSpec((CHUNK, RWKV_IN), main_map),
                  pl.BlockSpec((16, RWKV_IN), prev_map),
                  pl.BlockSpec((16, RWKV_IN), next_map),
                  pl.BlockSpec((2, RWKV_IN), lambda b, d, i: (0, 0)),
                  pl.BlockSpec((1, 8, RWKV_W), lambda b, d, i: (d, 0, 0)),
                  pl.BlockSpec((1, 2 * DECAY_LORA, RWKV_W), lambda b, d, i: (d, 0, 0)),
                  pl.BlockSpec((1, 2 * AAA_LORA, RWKV_W), lambda b, d, i: (d, 0, 0)),
                  pl.BlockSpec((GATE_LORA, RWKV_W), lambda b, d, i: (0, 0)),
                  pl.BlockSpec((1, 2, CHUNK, CHUNK), lambda b, d, i: (d, 0, 0, 0))],
        out_specs=(pl.BlockSpec((1, CHUNK, RWKV_W), out_map),
                   pl.BlockSpec((1, CHUNK, RWKV_W), out_map),
                   pl.BlockSpec((1, CHUNK, RWKV_W), out_map)),
        scratch_shapes=[pltpu.VMEM((RWKV_HEADS, RWKV_HEAD, RWKV_HEAD), F32)],
        compiler_params=_cparams(("arbitrary", "arbitrary", "arbitrary")),
        name="rwkv_scan",
    )(p, p, p, mu, vecs, wup, aup, gup, masks)


def _diff_kernel(q_ref, k_ref, v_ref, par_ref, o_ref, *, n_ctx):
    t = pl.program_id(2)
    lam = par_ref[0:1, :]
    gain = par_ref[1:2, :]

    def attend(nk):
        q = q_ref[...]
        k = k_ref[0:nk, :]
        v = v_ref[0:nk, :]

        def softmax_v(sl):
            s = _dot_nt(q[:, sl], k[:, sl])
            e = jnp.exp2(s - jnp.max(s, axis=-1, keepdims=True))
            return _dot(e, v) / jnp.sum(e, axis=-1, keepdims=True)

        o = softmax_v(slice(0, DIFF_DK)) - lam * softmax_v(slice(DIFF_DK, 2 * DIFF_DK))
        o = o * lax.rsqrt(jnp.mean(o * o, axis=-1, keepdims=True) + 1e-5)
        o_ref[...] = (o * gain).astype(o_ref.dtype)

    @pl.when(t * SEG < n_ctx)
    def _():
        attend(n_ctx)

    @pl.when(t * SEG >= n_ctx)
    def _():
        attend(k_ref.shape[0])


def _diff_attn(qk, p, par, bsz, n_ctx, ltot):
    rows = bsz * ltot
    nt = ltot // SEG
    voff = (P_DIFF + 2 * DIFF_W) // DIFF_DV
    return pl.pallas_call(
        functools.partial(_diff_kernel, n_ctx=n_ctx),
        out_shape=jax.ShapeDtypeStruct((rows, DIFF_W), BF16),
        grid=(bsz, DIFF_HEADS, nt),
        in_specs=[pl.BlockSpec((SEG, DIFF_DV), lambda b, h, t: (b * nt + t, h)),
                  pl.BlockSpec((ltot, DIFF_DV), lambda b, h, t: (b, DIFF_HEADS + h)),
                  pl.BlockSpec((ltot, DIFF_DV), lambda b, h, t: (b, voff + h)),
                  pl.BlockSpec((8, DIFF_DV), lambda b, h, t: (0, 0))],
        out_specs=pl.BlockSpec((SEG, DIFF_DV), lambda b, h, t: (b * nt + t, h)),
        compiler_params=_cparams(("arbitrary", "arbitrary", "arbitrary")),
        name="diff_attn",
    )(qk, qk, p, par)


def _win_kernel(sink_ref, q_ref, kp_ref, kc_ref, kn_ref, kx_ref, vp_ref, vc_ref, vn_ref, vx_ref,
                o_ref, *, n_ctx, n_lat):
    blk = pl.program_id(1)
    ncb = n_ctx // QBLK
    nq = WIN_GROUP * QBLK

    def run(keys, vals, mask):
        outs = []
        for g in range(WIN_KV_HEADS):
            ksl = slice(g * WIN_HEAD, (g + 1) * WIN_HEAD)
            qg = jnp.concatenate(
                [q_ref[:, (g * WIN_GROUP + j) * WIN_HEAD:(g * WIN_GROUP + j + 1) * WIN_HEAD]
                 for j in range(WIN_GROUP)], axis=0)
            s = _dot_nt(qg, keys[:, ksl])
            if mask is not None:
                s = jnp.where(mask, s, NEG_INF)
            hrow = lax.broadcasted_iota(jnp.int32, (nq, 1), 0) // QBLK
            sink = jnp.zeros((nq, 1), F32)
            for j in range(WIN_GROUP):
                sink = jnp.where(hrow == j, sink_ref[g * WIN_GROUP + j] * LOG2E, sink)
            m = jnp.maximum(jnp.max(s, axis=-1, keepdims=True), sink)
            e = jnp.exp2(s - m)
            den = jnp.sum(e, axis=-1, keepdims=True) + jnp.exp2(sink - m)
            og = _dot(e, vals[:, ksl]) / den
            outs.extend(og[j * QBLK:(j + 1) * QBLK] for j in range(WIN_GROUP))
        o_ref[...] = jnp.concatenate(outs, axis=1).astype(o_ref.dtype)

    @pl.when(blk < ncb)
    def _():
        run(kx_ref[...], vx_ref[...], None)

    @pl.when(blk >= ncb)
    def _():
        j = blk - ncb
        keys = jnp.concatenate([kp_ref[...], kc_ref[...], kn_ref[...], kx_ref[...]], axis=0)
        vals = jnp.concatenate([vp_ref[...], vc_ref[...], vn_ref[...], vx_ref[...]], axis=0)
        nk = 3 * QBLK + n_ctx
        qpos = j * QBLK + lax.broadcasted_iota(jnp.int32, (nq, nk), 0) % QBLK
        col = lax.broadcasted_iota(jnp.int32, (nq, nk), 1)
        kpos = (j - 1) * QBLK + col
        band = (jnp.abs(qpos - kpos) <= WINDOW) & (kpos >= 0) & (kpos < n_lat)
        run(keys, vals, band | (col >= 3 * QBLK))


def _win_attn(sink, qk, p, bsz, n_ctx, n_lat):
    ltot = n_ctx + n_lat
    rows = bsz * ltot
    nb = ltot // QBLK
    ncb = n_ctx // QBLK
    koff = WIN_W // WIN_KV_W
    voff = (P_WIN + WIN_W + WIN_KV_W) // WIN_KV_W

    def lat_blk(b, t, shift):
        j = jnp.clip(t - ncb + shift, 0, nb - ncb - 1)
        return b * nb + ncb + j

    kspec = lambda shift, c: pl.BlockSpec((QBLK, WIN_KV_W), lambda b, t, s: (lat_blk(b, t, shift), c))
    xspec = lambda c: pl.BlockSpec((n_ctx, WIN_KV_W), lambda b, t, s: (b * (ltot // n_ctx), c))
    return pl.pallas_call(
        functools.partial(_win_kernel, n_ctx=n_ctx, n_lat=n_lat),
        out_shape=jax.ShapeDtypeStruct((rows, WIN_W), BF16),
        grid_spec=pltpu.PrefetchScalarGridSpec(
            num_scalar_prefetch=1,
            grid=(bsz, nb),
            in_specs=[pl.BlockSpec((QBLK, WIN_W), lambda b, t, s: (b * nb + t, 0)),
                      kspec(-1, koff), kspec(0, koff), kspec(1, koff), xspec(koff),
                      kspec(-1, voff), kspec(0, voff), kspec(1, voff), xspec(voff)],
            out_specs=pl.BlockSpec((QBLK, WIN_W), lambda b, t, s: (b * nb + t, 0))),
        compiler_params=_cparams(("arbitrary", "arbitrary")),
        name="win_attn",
    )(sink, qk, qk, qk, qk, qk, p, p, p, p)


def _merge_kernel(ya_ref, yb_ref, yc_ref, ga_ref, gb_ref, gc_ref, wa_ref, wb_ref, wc_ref, o_ref):
    z = _sigmoid(ga_ref[...].astype(F32)) * _dot(ya_ref[...], wa_ref[...])
    z = z + _sigmoid(gb_ref[...].astype(F32)) * _dot(yb_ref[...], wb_ref[...])
    z = z + _sigmoid(gc_ref[...].astype(F32)) * _dot(yc_ref[...], wc_ref[...])
    o_ref[...] = z.astype(o_ref.dtype)


def _merge(ya, yb, yc, p, w_branch, layer, tm, tn):
    m = ya.shape[0]
    assert RWKV_W == DIFF_W and (RWKV_W + DIFF_W) % WIN_W == 0
    nj = D_MODEL // tn
    goff = P_GATE // tn
    gspec = lambda br: pl.BlockSpec((tm, tn), lambda j, i: (i, goff + br * nj + j))
    return pl.pallas_call(
        _merge_kernel,
        out_shape=jax.ShapeDtypeStruct((m, D_MODEL), BF16),
        grid=(nj, m // tm),
        in_specs=[pl.BlockSpec((tm, RWKV_W), lambda j, i: (i, 0)),
                  pl.BlockSpec((tm, DIFF_W), lambda j, i: (i, 0)),
                  pl.BlockSpec((tm, WIN_W), lambda j, i: (i, 0)),
                  gspec(0), gspec(1), gspec(2),
                  pl.BlockSpec((None, RWKV_W, tn), lambda j, i: (layer, 0, j)),
                  pl.BlockSpec((None, DIFF_W, tn), lambda j, i: (layer, 1, j)),
                  pl.BlockSpec((None, WIN_W, tn), lambda j, i: (layer, (RWKV_W + DIFF_W) // WIN_W, j))],
        out_specs=pl.BlockSpec((tm, tn), lambda j, i: (i, j)),
        compiler_params=_cparams(("arbitrary", "arbitrary")),
        name="merge",
    )(ya, yb, yc, p, p, p, w_branch, w_branch, w_branch)


def _expert_kernel(be_ref, nu_ref, x_ref, gw_ref, w1_ref, w3_ref, w2_ref, o_ref, w1b, w3b, w2b):
    i = pl.program_id(0)
    prev = be_ref[jnp.maximum(i - 1, 0)]

    @pl.when(jnp.logical_or(i == 0, be_ref[i] != prev))
    def _():
        w1b[...] = w1_ref[...].astype(BF16)
        w3b[...] = w3_ref[...].astype(BF16)
        w2b[...] = w2_ref[...].astype(BF16)

    @pl.when(i < nu_ref[0])
    def _():
        x = x_ref[...]
        h1 = _dot(x, w1b[...])
        h = h1 * _sigmoid(h1) * _dot(x, w3b[...])
        o_ref[...] = (_dot(h, w2b[...]) * gw_ref[...]).astype(o_ref.dtype)

    @pl.when(i >= nu_ref[0])
    def _():
        o_ref[...] = jnp.zeros_like(o_ref)


def _experts(blk_expert, n_used, xb, gw, w1, w3, w2, layer):
    rows = xb.shape[0]
    nb = rows // MOE_BLK
    return pl.pallas_call(
        _expert_kernel,
        out_shape=jax.ShapeDtypeStruct((rows, D_MODEL), BF16),
        grid_spec=pltpu.PrefetchScalarGridSpec(
            num_scalar_prefetch=2,
            grid=(nb,),
            in_specs=[pl.BlockSpec((MOE_BLK, D_MODEL), lambda i, be, nu: (i, 0)),
                      pl.BlockSpec((MOE_BLK, 1), lambda i, be, nu: (i, 0)),
                      pl.BlockSpec((None, None, D_MODEL, D_EXPERT), lambda i, be, nu: (layer, be[i], 0, 0)),
                      pl.BlockSpec((None, None, D_MODEL, D_EXPERT), lambda i, be, nu: (layer, be[i], 0, 0)),
                      pl.BlockSpec((None, None, D_EXPERT, D_MODEL), lambda i, be, nu: (layer, be[i], 0, 0))],
            out_specs=pl.BlockSpec((MOE_BLK, D_MODEL), lambda i, be, nu: (i, 0)),
            scratch_shapes=[pltpu.VMEM((D_MODEL, D_EXPERT), BF16),
                            pltpu.VMEM((D_MODEL, D_EXPERT), BF16),
                            pltpu.VMEM((D_EXPERT, D_MODEL), BF16)]),
        compiler_params=_cparams(("arbitrary",)),
        name="experts",
    )(blk_expert, n_used, xb, gw, w1, w3, w2)


def _rope_tables(n_ctx, n_lat):
    rows = n_lat // GRID_W
    row = jnp.repeat(jnp.arange(rows), GRID_W).astype(F32)
    col = (jnp.arange(rows * GRID_W) % GRID_W).astype(F32)
    inv = ROPE_BASE ** (-jnp.arange(ROPE_AX_FREQS, dtype=F32) / ROPE_AX_FREQS)
    ang = jnp.stack([row[:, None] * inv, col[:, None] * inv], axis=1)
    cos, sin = jnp.cos(ang), jnp.sin(ang)
    cos4 = jnp.stack([cos, cos], axis=2).reshape(n_lat, 4 * ROPE_AX_FREQS)
    sin4 = jnp.stack([-sin, sin], axis=2).reshape(n_lat, 4 * ROPE_AX_FREQS)
    cos4 = jnp.concatenate([jnp.ones((n_ctx, 64), F32), cos4], axis=0)
    sin4 = jnp.concatenate([jnp.zeros((n_ctx, 64), F32), sin4], axis=0)
    return jnp.stack([jnp.tile(cos4, (1, LANE // 64)), jnp.tile(sin4, (1, LANE // 64))])


def _moe(route, v, w1, w3, w2, layer):
    t = v.shape[0]
    experts = route[:, :EXPERT_TOP_K].astype(jnp.int32)
    gates = route[:, EXPERT_TOP_K:2 * EXPERT_TOP_K]
    k = EXPERT_TOP_K
    a = t * k
    e_n = N_EXPERTS
    nb = -(-a // MOE_BLK) + e_n
    e_flat = experts.reshape(a)
    order = jnp.argsort(e_flat).astype(jnp.int32)
    e_s = e_flat[order]
    start = jnp.searchsorted(e_s, jnp.arange(e_n + 1, dtype=jnp.int32), side='left').astype(jnp.int32)
    counts = start[1:] - start[:-1]
    start = start[:-1]
    padded = (counts + MOE_BLK - 1) // MOE_BLK * MOE_BLK
    pad_end = jnp.cumsum(padded)
    pad_start = pad_end - padded
    dest = (pad_start[e_s] + jnp.arange(a, dtype=jnp.int32) - start[e_s]).astype(jnp.int32)
    blk_expert = jnp.minimum(jnp.searchsorted(pad_end, jnp.arange(nb) * MOE_BLK, side='right'),
                             e_n - 1).astype(jnp.int32)
    n_used = (pad_end[-1:] // MOE_BLK).astype(jnp.int32)
    slot = jnp.arange(nb * MOE_BLK, dtype=jnp.int32)
    slot_e = jnp.repeat(blk_expert, MOE_BLK)
    rank = slot - pad_start[slot_e]
    filled = jnp.logical_and(rank < counts[slot_e], slot < pad_end[-1])
    src = order[jnp.clip(start[slot_e] + rank, 0, a - 1)]
    slot_tok = jnp.where(filled, src // k, 0)
    slot_gate = jnp.where(filled, gates.reshape(a)[src], 0.0)
    xb = v.at[slot_tok].get(mode="promise_in_bounds")
    ys = _experts(blk_expert, n_used, xb, slot_gate[:, None], w1, w3, w2, layer)
    pos = jnp.zeros((a,), jnp.int32).at[order].set(dest, unique_indices=True, mode="promise_in_bounds")
    pos = pos.reshape(t, k).T.reshape(a)
    yg = lax.optimization_barrier(ys.at[pos].get(mode="promise_in_bounds"))
    return yg.reshape(k, t, D_MODEL)


def kernel(x, c, ctx, c_ctx, w_mod, b_mod, w_in, rwkv_mu, rwkv_w0, rwkv_w_up, rwkv_a0, rwkv_a_up,
           rwkv_g_up, rwkv_kvec, rwkv_lnx, diff_lam, diff_subln, win_sink, w_branch, w_out, ln_g, ln_b,
           w_rg, b_rg, w_re, b_re, w1, w3, w2):
    bsz, n_lat, dm = x.shape
    n_ctx = ctx.shape[1]
    depth = w_mod.shape[0]
    ltot = n_ctx + n_lat
    rows = bsz * ltot
    dn_alpha = (2 * depth) ** 0.25
    assert dm == D_MODEL and n_ctx % SEG == 0 and n_lat % SEG == 0 and ltot % n_ctx == 0

    nt = ltot // SEG
    cs = _rope_tables(n_ctx, n_lat)
    fwd = jnp.tril(jnp.ones((CHUNK, CHUNK), F32))
    masks = jnp.stack([jnp.stack([fwd - jnp.eye(CHUNK, dtype=F32), fwd]),
                       jnp.stack([fwd.T - jnp.eye(CHUNK, dtype=F32), fwd.T])])
    head_id = jnp.arange(RWKV_W) // RWKV_HEAD
    head_avg = ((head_id[:, None] == head_id[None, :]).astype(F32) / RWKV_HEAD).astype(BF16)

    xs = jnp.concatenate([ctx, x], axis=1).reshape(rows, dm)
    cvec = jnp.concatenate([c_ctx[None, :], c], axis=0)
    cpad = jnp.zeros((32, dm), F32).at[:bsz + 1].set(jax.nn.silu(cvec))
    w_in_p = jnp.concatenate([w_in[:, :, :RWKV_IN], jnp.zeros((depth, dm, P_GATE - RWKV_IN), F32),
                              w_in[:, :, GATE_OFF:], w_in[:, :, DIFF_OFF:GATE_OFF]], axis=2).astype(BF16)
    w_branch_b = w_branch.astype(BF16)
    w_out_b = w_out.astype(BF16)
    w_r = jnp.zeros((depth, dm, LANE), F32).at[:, :, :N_GROUPS].set(w_rg)
    w_r = w_r.at[:, :, N_GROUPS:N_GROUPS + N_EXPERTS].set(w_re)
    b_r = jnp.zeros((depth, 8, LANE), F32).at[:, :, :N_GROUPS].set(b_rg[:, None, :])
    b_r = b_r.at[:, :, N_GROUPS:N_GROUPS + N_EXPERTS].set(b_re[:, None, :])

    mods = [_mm(cpad, w_mod, i, F32, 32, 1024, "mod")[:bsz + 1] + b_mod[i] for i in range(depth)]

    def table(gate, shift, scale):
        def both(m, j):
            v = m[:, j * dm:(j + 1) * dm]
            return jnp.stack([jnp.broadcast_to(v[0], (bsz, dm)), v[1:]], axis=1)
        t = jnp.stack([both(*gate), both(*shift), both(*scale)], axis=2)
        return jnp.pad(t, ((0, 0), (0, 0), (0, 5), (0, 0))).reshape(2 * bsz, 8, dm)

    u = _modulate0(xs, table((mods[0], 0), (mods[0], 0), (mods[0], 1)), nt)
    for i in range(depth):
        last = i == depth - 1
        lam_init = 0.8 - 0.6 * math.exp(-0.3 * i)
        mod = mods[i]
        p = _mm(u, w_in_p, i, BF16, 512, 1024, "in_proj")

        vecs = jnp.zeros((2, 8, RWKV_W), F32)
        vecs = vecs.at[:, 0].set(rwkv_w0[i]).at[:, 1].set(rwkv_a0[i])
        vecs = vecs.at[:, 2:5].set(jnp.broadcast_to(rwkv_kvec[i][None], (2, 3, RWKV_W)))
        wup = jnp.zeros((2, 2 * DECAY_LORA, RWKV_W), BF16)
        aup = jnp.zeros((2, 2 * AAA_LORA, RWKV_W), BF16)
        for d in range(2):
            wup = wup.at[d, d * DECAY_LORA:(d + 1) * DECAY_LORA].set(rwkv_w_up[i, d].astype(BF16))
            aup = aup.at[d, d * AAA_LORA:(d + 1) * AAA_LORA].set(rwkv_a_up[i, d].astype(BF16))
        y2, bv2, g2 = _rwkv_scan(p, rwkv_mu[i], vecs, wup, aup, rwkv_g_up[i].astype(BF16), masks,
                                 bsz, n_ctx, n_lat)
        ya = _readout(y2, bv2, g2, rwkv_lnx[i], head_avg, 512)

        dqk, wqk = _rope(p, cs, nt)
        lf = diff_lam[i]
        lam = jnp.exp(jnp.sum(lf[0] * lf[1])) - jnp.exp(jnp.sum(lf[2] * lf[3])) + lam_init
        par = jnp.zeros((8, DIFF_DV), F32).at[0].set(lam).at[1].set(diff_subln[i] * (1 - lam_init))
        yb = _diff_attn(dqk, p, par, bsz, n_ctx, ltot)
        yc = _win_attn(win_sink[i], wqk, p, bsz, n_ctx, n_lat)
        z = _merge(ya, yb, yc, p, w_branch_b, i, 512, 1024)

        lnp = lambda j: jnp.zeros((8, dm), F32).at[0].set(ln_g[i, j]).at[1].set(ln_b[i, j])
        xs, v, route = _mix_out(z, xs, w_out_b, i, table((mod, 2), (mod, 3), (mod, 4)), lnp(0),
                                w_r[i], b_r[i], nt, dn_alpha)

        if last:
            lat = lambda t: t.reshape(bsz, ltot, -1)[:, n_ctx:].reshape(bsz * n_lat, -1)
            yg = _moe(lat(route), lat(v), w1, w3, w2, i)
            tab = table((mod, 5), (mod, 0), (mod, 1))
            (out,) = _moe_out(xs, yg, tab, lnp(1), bsz, nt, n_ctx // SEG, dn_alpha, False)
            return out.reshape(bsz, n_lat, dm)
        yg = _moe(route, v, w1, w3, w2, i)
        tab = table((mod, 5), (mods[i + 1], 0), (mods[i + 1], 1))
        xs, u = _moe_out(xs, yg, tab, lnp(1), bsz, nt, 0, dn_alpha, True)
    return None
```

```python
import functools
import math

import jax
import jax.numpy as jnp
from jax import lax
from jax.experimental import pallas as pl
from jax.experimental.pallas import tpu as pltpu

F32 = jnp.float32
BF16 = jnp.bfloat16

D_MODEL = 2048
GRID_W = 64
RWKV_HEADS = 12
RWKV_HEAD = 64
RWKV_W = RWKV_HEADS * RWKV_HEAD
DECAY_LORA = 64
AAA_LORA = 64
GATE_LORA = 128
RWKV_GN_EPS = 64e-5
DIFF_HEADS = 6
DIFF_DK = 64
DIFF_DV = 2 * DIFF_DK
DIFF_W = DIFF_HEADS * DIFF_DV
DIFF_SCALE = DIFF_DK ** -0.5
WIN_Q_HEADS = 8
WIN_KV_HEADS = 2
WIN_GROUP = WIN_Q_HEADS // WIN_KV_HEADS
WIN_HEAD = 64
WIN_W = WIN_Q_HEADS * WIN_HEAD
WIN_KV_W = WIN_KV_HEADS * WIN_HEAD
WIN_SCALE = WIN_HEAD ** -0.5
WINDOW = 128
QBLK = WINDOW
MIX_W = RWKV_W + DIFF_W + WIN_W
N_BRANCH = 3
ROPE_BASE = 10000.0
ROPE_AX_FREQS = 16
RWKV_IN = 3 * RWKV_W + 2 * DECAY_LORA + 2 * AAA_LORA + GATE_LORA
DIFF_IN = 3 * DIFF_W
WIN_IN = WIN_W + 2 * WIN_KV_W
DIFF_OFF = RWKV_IN
WIN_OFF = DIFF_OFF + DIFF_IN
GATE_OFF = WIN_OFF + WIN_IN
N_IN = GATE_OFF + N_BRANCH * D_MODEL
N_GROUPS = 4
EXPERTS_PER_GROUP = 8
N_EXPERTS = N_GROUPS * EXPERTS_PER_GROUP
EXPERT_TOP_K = 2
D_EXPERT = D_MODEL // 4
MOE_BLK = 256
ADA_EPS = 1e-6
LN_EPS = 1e-5
NEG_INF = -1e30
LOG2E = math.log2(math.e)

LANE = 128
SEG = 256
P_RWKV = 0
P_GATE = 3072
P_DIFF = P_GATE + N_BRANCH * D_MODEL
P_WIN = P_DIFF + DIFF_IN
P_COLS = P_WIN + WIN_IN
CHUNK = 64
VMEM_LIMIT = 56 * 1024 * 1024


def _cparams(sem):
    return pltpu.CompilerParams(dimension_semantics=sem, vmem_limit_bytes=VMEM_LIMIT)


def _dot(a, b):
    return jnp.dot(a.astype(BF16), b.astype(BF16), preferred_element_type=F32)


def _dot_nt(a, b):
    return lax.dot_general(a.astype(BF16), b.astype(BF16), (((1,), (1,)), ((), ())),
                           preferred_element_type=F32)


def _dot_tn(a, b):
    return lax.dot_general(a.astype(BF16), b.astype(BF16), (((0,), (0,)), ((), ())),
                           preferred_element_type=F32)


def _split(x):
    hi = x.astype(BF16)
    lo = (x - hi.astype(F32)).astype(BF16)
    return hi, lo


def _dot3(a, b):
    ah, al = _split(a)
    bh, bl = _split(b)
    d = functools.partial(jnp.dot, preferred_element_type=F32)
    return d(ah, bh) + (d(ah, bl) + d(al, bh))


def _sigmoid(x):
    return 1.0 / (1.0 + jnp.exp(-x))


def _mm_kernel(a_ref, w_ref, o_ref):
    o_ref[...] = _dot(a_ref[...], w_ref[...]).astype(o_ref.dtype)


def _mm(a, w, layer, out_dtype, tm, tn, name):
    m, k = a.shape
    n = w.shape[2]
    return pl.pallas_call(
        _mm_kernel,
        out_shape=jax.ShapeDtypeStruct((m, n), out_dtype),
        grid=(n // tn, m // tm),
        in_specs=[pl.BlockSpec((tm, k), lambda j, i: (i, 0)),
                  pl.BlockSpec((None, k, tn), lambda j, i: (layer, 0, j))],
        out_specs=pl.BlockSpec((tm, tn), lambda j, i: (i, j)),
        compiler_params=_cparams(("arbitrary", "arbitrary")),
        name=name,
    )(a, w)


def _norm_rows(x, eps):
    mu = jnp.mean(x, axis=-1, keepdims=True)
    xc = x - mu
    return xc * lax.rsqrt(jnp.mean(xc * xc, axis=-1, keepdims=True) + eps)


def _route_rows(logits):
    col = lax.broadcasted_iota(jnp.int32, logits.shape, 1).astype(F32)
    big = float(LANE)
    is_g = col < N_GROUPS
    lg = jnp.where(is_g, logits, NEG_INF)
    g_max = jnp.max(lg, axis=-1, keepdims=True)
    g_sum = jnp.sum(jnp.where(is_g, jnp.exp(lg - g_max), 0.0), axis=-1, keepdims=True)
    pg_top = 1.0 / g_sum
    g_idx = jnp.min(jnp.where(is_g & (lg == g_max), col, big), axis=-1, keepdims=True)
    lo = N_GROUPS + EXPERTS_PER_GROUP * g_idx
    sel = (col >= lo) & (col < lo + EXPERTS_PER_GROUP)
    le = jnp.where(sel, logits, NEG_INF)
    m1 = jnp.max(le, axis=-1, keepdims=True)
    den = jnp.sum(jnp.where(sel, jnp.exp(le - m1), 0.0), axis=-1, keepdims=True)
    i1 = jnp.min(jnp.where(sel & (le == m1), col, big), axis=-1, keepdims=True)
    rest = sel & (col != i1)
    le2 = jnp.where(rest, logits, NEG_INF)
    m2 = jnp.max(le2, axis=-1, keepdims=True)
    i2 = jnp.min(jnp.where(rest & (le2 == m2), col, big), axis=-1, keepdims=True)
    p1 = 1.0 / den
    p2 = jnp.exp(m2 - m1) / den
    tot = p1 + p2
    out = jnp.where(col == 0.0, i1 - N_GROUPS, 0.0)
    out = jnp.where(col == 1.0, i2 - N_GROUPS, out)
    out = jnp.where(col == 2.0, pg_top * p1 / tot, out)
    return jnp.where(col == 3.0, pg_top * p2 / tot, out)


def _mod_kernel(x_ref, tab_ref, u_ref):
    u_ref[...] = (_norm_rows(x_ref[...], ADA_EPS) * (1.0 + tab_ref[0, 2:3, :]) + tab_ref[0, 1:2, :]).astype(u_ref.dtype)


def _modulate0(x, tab, nt):
    rows, dm = x.shape
    return pl.pallas_call(
        _mod_kernel,
        out_shape=jax.ShapeDtypeStruct((rows, dm), BF16),
        grid=(rows // SEG,),
        in_specs=[pl.BlockSpec((SEG, dm), lambda i: (i, 0)),
                  pl.BlockSpec((1, 8, dm), lambda i: (2 * (i // nt) + jnp.minimum(i % nt, 1), 0, 0))],
        out_specs=pl.BlockSpec((SEG, dm), lambda i: (i, 0)),
        compiler_params=_cparams(("arbitrary",)),
        name="modulate0",
    )(x, tab)


def _mix_out_kernel(z_ref, x_ref, w_ref, tab_ref, lnp_ref, wr_ref, br_ref, xo_ref, vo_ref, ro_ref, *, alpha):
    m = _dot(z_ref[...], w_ref[...])
    xn = _norm_rows(alpha * x_ref[...] + tab_ref[0, 0:1, :] * m, LN_EPS) * lnp_ref[0:1, :] + lnp_ref[1:2, :]
    xo_ref[...] = xn
    v = _norm_rows(xn, ADA_EPS) * (1.0 + tab_ref[0, 2:3, :]) + tab_ref[0, 1:2, :]
    vo_ref[...] = v.astype(vo_ref.dtype)
    ro_ref[...] = _route_rows(_dot3(v, wr_ref[...]) + br_ref[0:1, :])


def _mix_out(z, x, w_out, layer, tab, lnp, w_r, b_r, nt, alpha):
    rows, dm = x.shape
    row = lambda i: (i, 0)
    const = lambda i: (0, 0)
    return pl.pallas_call(
        functools.partial(_mix_out_kernel, alpha=alpha),
        out_shape=(jax.ShapeDtypeStruct((rows, dm), F32), jax.ShapeDtypeStruct((rows, dm), BF16),
                   jax.ShapeDtypeStruct((rows, LANE), F32)),
        grid=(rows // SEG,),
        in_specs=[pl.BlockSpec((SEG, dm), row), pl.BlockSpec((SEG, dm), row),
                  pl.BlockSpec((None, dm, dm), lambda i: (layer, 0, 0)),
                  pl.BlockSpec((1, 8, dm), lambda i: (2 * (i // nt) + jnp.minimum(i % nt, 1), 0, 0)),
                  pl.BlockSpec((8, dm), const), pl.BlockSpec((dm, LANE), const), pl.BlockSpec((8, LANE), const)],
        out_specs=(pl.BlockSpec((SEG, dm), row), pl.BlockSpec((SEG, dm), row), pl.BlockSpec((SEG, LANE), row)),
        compiler_params=_cparams(("arbitrary",)),
        name="mix_out",
    )(z, x, w_out, tab, lnp, w_r, b_r)


def _moe_out_kernel(x_ref, y0_ref, y1_ref, tab_ref, lnp_ref, xo_ref, *u_ref, alpha):
    y = y0_ref[0].astype(F32) + y1_ref[0].astype(F32)
    xn = _norm_rows(alpha * x_ref[...] + tab_ref[0, 0:1, :] * y, LN_EPS) * lnp_ref[0:1, :] + lnp_ref[1:2, :]
    xo_ref[...] = xn
    if u_ref:
        u_ref[0][...] = (_norm_rows(xn, ADA_EPS) * (1.0 + tab_ref[0, 2:3, :]) + tab_ref[0, 1:2, :]).astype(BF16)


def _moe_out(x, yg, tab, lnp, bsz, nt, skip, alpha, emit_u):
    dm = x.shape[1]
    nk = nt - skip
    rows = bsz * nk * SEG
    xmap = lambda b, t: (b * nt + skip + t, 0)
    omap = lambda b, t: (b * nk + t, 0)
    out_shape = [jax.ShapeDtypeStruct((rows, dm), F32)]
    out_specs = [pl.BlockSpec((SEG, dm), omap)]
    if emit_u:
        out_shape.append(jax.ShapeDtypeStruct((rows, dm), BF16))
        out_specs.append(pl.BlockSpec((SEG, dm), omap))
    return pl.pallas_call(
        functools.partial(_moe_out_kernel, alpha=alpha),
        out_shape=tuple(out_shape),
        grid=(bsz, nk),
        in_specs=[pl.BlockSpec((SEG, dm), xmap),
                  pl.BlockSpec((1, SEG, dm), lambda b, t: (0, b * nk + t, 0)),
                  pl.BlockSpec((1, SEG, dm), lambda b, t: (1, b * nk + t, 0)),
                  pl.BlockSpec((1, 8, dm), lambda b, t: (2 * b + jnp.minimum(skip + t, 1), 0, 0)),
                  pl.BlockSpec((8, dm), lambda b, t: (0, 0))],
        out_specs=tuple(out_specs),
        compiler_params=_cparams(("arbitrary", "arbitrary")),
        name="moe_out",
    )(x, yg, yg, tab, lnp)


def _rope_kernel(pd_ref, pw_ref, cs_ref, dqk_ref, wqk_ref):
    cos = cs_ref[0]
    sin = cs_ref[1]
    first_half = lax.broadcasted_iota(jnp.int32, cos.shape, 1) % (2 * ROPE_AX_FREQS) < ROPE_AX_FREQS

    def rot(x, scale):
        x = x.astype(F32)
        sw = jnp.where(first_half, pltpu.roll(x, LANE - ROPE_AX_FREQS, axis=1), pltpu.roll(x, ROPE_AX_FREQS, axis=1))
        return ((x * cos + sw * sin) * scale).astype(BF16)

    for c in range(2 * DIFF_W // LANE):
        scale = DIFF_SCALE * LOG2E if c < DIFF_W // LANE else 1.0
        dqk_ref[:, c * LANE:(c + 1) * LANE] = rot(pd_ref[:, c * LANE:(c + 1) * LANE], scale)
    for c in range((WIN_W + WIN_KV_W) // LANE):
        scale = WIN_SCALE * LOG2E if c < WIN_W // LANE else 1.0
        wqk_ref[:, c * LANE:(c + 1) * LANE] = rot(pw_ref[:, c * LANE:(c + 1) * LANE], scale)


def _rope(p, cs, nt):
    rows = p.shape[0]
    wd, ww = 2 * DIFF_W, WIN_W + WIN_KV_W
    return pl.pallas_call(
        _rope_kernel,
        out_shape=(jax.ShapeDtypeStruct((rows, wd), BF16), jax.ShapeDtypeStruct((rows, ww), BF16)),
        grid=(rows // SEG,),
        in_specs=[pl.BlockSpec((SEG, wd), lambda i: (i, P_DIFF // wd)),
                  pl.BlockSpec((SEG, ww), lambda i: (i, P_WIN // ww)),
                  pl.BlockSpec((2, SEG, LANE), lambda i: (0, i % nt, 0))],
        out_specs=(pl.BlockSpec((SEG, wd), lambda i: (i, 0)), pl.BlockSpec((SEG, ww), lambda i: (i, 0))),
        compiler_params=_cparams(("arbitrary",)),
        name="rope",
    )(p, p, cs)


def _readout_kernel(y0_ref, y1_ref, bv0_ref, bv1_ref, g_ref, lnx_ref, avg_ref, o_ref):
    avg = avg_ref[...]

    def head_mean(t):
        hi, lo = _split(t)
        return jnp.dot(hi, avg, preferred_element_type=F32) + jnp.dot(lo, avg, preferred_element_type=F32)

    y = y0_ref[...] + y1_ref[...]
    dev = y - head_mean(y)
    yn = dev * lax.rsqrt(head_mean(dev * dev) + RWKV_GN_EPS) * lnx_ref[0:1, :] + lnx_ref[1:2, :]
    bonus = bv0_ref[...].astype(F32) + bv1_ref[...].astype(F32)
    o_ref[...] = ((yn + bonus) * g_ref[...].astype(F32)).astype(o_ref.dtype)


def _readout(y0, y1, bv0, bv1, g, lnx, avg, tm):
    rows = y0.shape[0]
    row = pl.BlockSpec((tm, RWKV_W), lambda i: (i, 0))
    return pl.pallas_call(
        _readout_kernel,
        out_shape=jax.ShapeDtypeStruct((rows, RWKV_W), BF16),
        grid=(rows // tm,),
        in_specs=[row, row, row, row, row,
                  pl.BlockSpec((2, RWKV_W), lambda i: (0, 0)),
                  pl.BlockSpec((RWKV_W, RWKV_W), lambda i: (0, 0))],
        out_specs=row,
        compiler_params=_cparams(("arbitrary",)),
        name="rwkv_readout",
    )(y0, y1, bv0, bv1, g, lnx, avg)


def _rwkv_features(d, chunk, p_ref, hp_ref, hn_ref, mu_ref, vec_ref, wup_ref, aup_ref, gup_ref, msk_ref,
                   bv_ref, g_ref, nc_ctx, nc_lat):
    n = CHUNK
    p = p_ref[...].astype(F32)
    first = jnp.logical_or(chunk == 0, chunk == nc_ctx)
    last = jnp.logical_or(chunk == nc_ctx - 1, chunk == nc_ctx + nc_lat - 1)
    hp = jnp.where(first, 0.0, hp_ref[15:16, :].astype(F32))
    hn = jnp.where(last, 0.0, hn_ref[0:1, :].astype(F32))
    row = lax.broadcasted_iota(jnp.int32, (n, 1), 0)
    prev = jnp.where(row == 0, hp, pltpu.roll(p, 1, axis=0))
    nxt = jnp.where(row == n - 1, hn, pltpu.roll(p, n - 1, axis=0))
    ps = p + mu_ref[0:1, :] * (prev - p) + mu_ref[1:2, :] * (nxt - p)

    r = ps[:, 0:RWKV_W]
    k = ps[:, RWKV_W:2 * RWKV_W]
    v = ps[:, 2 * RWKV_W:3 * RWKV_W]
    o = 3 * RWKV_W
    wd = jnp.tanh(ps[:, o:o + 2 * DECAY_LORA])
    ad = ps[:, o + 2 * DECAY_LORA:o + 2 * DECAY_LORA + 2 * AAA_LORA]
    gd = ps[:, o + 2 * DECAY_LORA + 2 * AAA_LORA:]
    w0 = vec_ref[d, 0:1, :]
    a0 = vec_ref[d, 1:2, :]
    k_k = vec_ref[d, 2:3, :]
    k_a = vec_ref[d, 3:4, :]
    r_k = vec_ref[d, 4:5, :]
    w_log = w0 + _dot(wd, wup_ref[d])
    a = _sigmoid(a0 + _dot(ad, aup_ref[d]))
    if g_ref is not None:
        g_ref[...] = _dot(_sigmoid(gd), gup_ref[...]).astype(g_ref.dtype)
    logw = -math.exp(-0.5) * _sigmoid(w_log)

    strict = msk_ref[d, 0] > 0.5
    incl_f = msk_ref[d, 1]
    incl = incl_f > 0.5
    lw_hi, lw_lo = _split(logw)
    incl_b = incl_f.astype(BF16)
    cl = (jnp.dot(incl_b, lw_hi, preferred_element_type=F32)
          + jnp.dot(incl_b, lw_lo, preferred_element_type=F32))
    tot = jnp.sum(logw, axis=0, keepdims=True)
    e_in = jnp.exp(cl)
    e_ex = jnp.exp(cl - logw)
    e_inv = jnp.exp(-cl)
    e_end = jnp.exp(tot - cl)
    p_all = jnp.exp(tot)

    sls = [slice(h * RWKV_HEAD, (h + 1) * RWKV_HEAD) for h in range(RWKV_HEADS)]
    kk_n = k * k_k
    kk_sq = kk_n * kk_n
    kd_all = k * (1.0 + (a - 1.0) * k_a)
    rkd = r * kd_all * r_k
    inv_norm = [1.0 / jnp.maximum(jnp.sqrt(jnp.sum(kk_sq[:, sl], axis=-1, keepdims=True)), 1e-12) for sl in sls]
    bonus = [jnp.sum(rkd[:, sl], axis=-1, keepdims=True) for sl in sls]
    bv_ref[...] = jnp.concatenate([bonus[h] * v[:, sl] for h, sl in enumerate(sls)], axis=1).astype(bv_ref.dtype)
    kk_all = jnp.concatenate([kk_n[:, sl] * inv_norm[h] for h, sl in enumerate(sls)], axis=1)
    bd_all = kk_all * a
    return dict(
        v=v, strict=strict, incl=incl, p_all=p_all,
        kk_t=kk_all * e_ex,
        r_t=r * e_in,
        b_i=bd_all * e_inv, k_i=kd_all * e_inv,
        k_e=kd_all * e_end,
        b_e=bd_all * e_end)


def _rwkv_kernel(p0_ref, hp0_ref, hn0_ref, p1_ref, hp1_ref, hn1_ref, mu_ref, vec_ref, wup_ref, aup_ref,
                 gup_ref, msk_ref, y0_ref, y1_ref, bv0_ref, bv1_ref, g_ref, state, *, nc_ctx, nc_lat):
    i = pl.program_id(1)

    @pl.when(i == 0)
    def _():
        state[...] = jnp.zeros_like(state)

    n = CHUNK
    shared = (mu_ref, vec_ref, wup_ref, aup_ref, gup_ref, msk_ref)
    f = [_rwkv_features(0, i, p0_ref, hp0_ref, hn0_ref, *shared, bv0_ref, g_ref, nc_ctx, nc_lat),
         _rwkv_features(1, _rwkv_mirror(i, nc_ctx, nc_lat), p1_ref, hp1_ref, hn1_ref, *shared, bv1_ref, None,
                        nc_ctx, nc_lat)]

    eye_f = (lax.broadcasted_iota(jnp.int32, (n, n), 0) == lax.broadcasted_iota(jnp.int32, (n, n), 1)).astype(F32)
    zeros = jnp.zeros((n, RWKV_HEAD), F32)
    items = [(d, h) for d in range(2) for h in range(RWKV_HEADS)]
    idx = range(len(items))
    sl = lambda j: slice(items[j][1] * RWKV_HEAD, (items[j][1] + 1) * RWKV_HEAD)
    get = lambda name, j: f[items[j][0]][name][:, sl(j)]
    strict = [f[d]["strict"] for d, _ in items]
    incl = [f[d]["incl"] for d, _ in items]

    v_h = [get("v", j) for j in idx]
    kk_t = [get("kk_t", j) for j in idx]
    r_t = [get("r_t", j) for j in idx]
    gram = [_dot_nt(jnp.concatenate([kk_t[j], r_t[j]], axis=0),
                    jnp.concatenate([get("b_i", j), get("k_i", j)], axis=0)) for j in idx]
    l_k = [jnp.where(strict[j], gram[j][:n, n:], 0.0) for j in idx]
    pw = [jnp.where(strict[j], -gram[j][:n, :n], 0.0) for j in idx]
    t_inv = [eye_f + pw[j] for j in idx]
    lkv = [_dot(l_k[j], v_h[j]) for j in idx]
    pw = [_dot(pw[j], pw[j]) for j in idx]
    for _ in range(4):
        both = [_dot(jnp.concatenate([t_inv[j], pw[j]], axis=0), pw[j]) for j in idx]
        t_inv = [t_inv[j] + both[j][:n] for j in idx]
        pw = [both[j][n:] for j in idx]
    t_inv = [t_inv[j] + _dot(t_inv[j], pw[j]) for j in idx]
    tx = [_dot(t_inv[j], jnp.concatenate([kk_t[j], lkv[j]], axis=1)) for j in idx]
    rhs = [jnp.concatenate([jnp.concatenate([zeros, v_h[j]], axis=1), tx[j]], axis=0) for j in idx]
    m_r = [jnp.concatenate([jnp.where(incl[j], gram[j][n:, n:], 0.0),
                            jnp.where(incl[j], -gram[j][n:, :n], 0.0)], axis=1) for j in idx]
    top = [_dot(m_r[j], rhs[j]) for j in idx]
    bot_t = [_dot_tn(rhs[j], jnp.concatenate([get("k_e", j), -get("b_e", j)], axis=0)) for j in idx]
    s0 = [state[d, h] for d, h in items]
    ys = [_dot_nt(r_t[j] + top[j][:, :RWKV_HEAD], s0[j]) + top[j][:, RWKV_HEAD:] for j in idx]
    y0_ref[...] = jnp.concatenate(ys[:RWKV_HEADS], axis=1)
    y1_ref[...] = jnp.concatenate(ys[RWKV_HEADS:], axis=1)
    s1 = [_dot(s0[j], bot_t[j][:RWKV_HEAD]) for j in idx]
    for j, (d, h) in enumerate(items):
        state[d, h] = s0[j] * get("p_all", j) + s1[j] + bot_t[j][RWKV_HEAD:]


def _rwkv_mirror(i, nc_ctx, nc_lat):
    return jnp.where(i < nc_ctx, nc_ctx - 1 - i, 2 * nc_ctx + nc_lat - 1 - i)


def _rwkv_scan(p, mu, vecs, wup, aup, gup, masks, bsz, n_ctx, n_lat):
    ltot = n_ctx + n_lat
    nc_ctx, nc_lat = n_ctx // CHUNK, n_lat // CHUNK
    nc = nc_ctx + nc_lat
    rows = bsz * ltot
    hb = CHUNK // 16
    n_hblk = rows // 16
    chunk_of = (lambda i: i, lambda i: _rwkv_mirror(i, nc_ctx, nc_lat))

    def specs(d):
        main = lambda b, i: (b * nc + chunk_of[d](i), 0)
        prev = lambda b, i: (jnp.maximum((b * nc + chunk_of[d](i)) * hb - 1, 0), 0)
        nxt = lambda b, i: (jnp.minimum((b * nc + chunk_of[d](i) + 1) * hb, n_hblk - 1), 0)
        return main, [pl.BlockSpec((CHUNK, RWKV_IN), main), pl.BlockSpec((16, RWKV_IN), prev),
                      pl.BlockSpec((16, RWKV_IN), nxt)]

    (main0, in0), (main1, in1) = specs(0), specs(1)
    whole = lambda shape: pl.BlockSpec(shape, lambda b, i: (0,) * len(shape))
    out_f = jax.ShapeDtypeStruct((rows, RWKV_W), F32)
    out_bf = jax.ShapeDtypeStruct((rows, RWKV_W), BF16)
    ospec = lambda m: pl.BlockSpec((CHUNK, RWKV_W), m)
    kern = functools.partial(_rwkv_kernel, nc_ctx=nc_ctx, nc_lat=nc_lat)
    return pl.pallas_call(
        kern,
        out_shape=(out_f, out_f, out_bf, out_bf, out_bf),
        grid=(bsz, nc),
        in_specs=in0 + in1 + [whole((2, RWKV_IN)), whole((2, 8, RWKV_W)),
                              whole((2, 2 * DECAY_LORA, RWKV_W)), whole((2, 2 * AAA_LORA, RWKV_W)),
                              whole((GATE_LORA, RWKV_W)), whole((2, 2, CHUNK, CHUNK))],
        out_specs=(ospec(main0), ospec(main1), ospec(main0), ospec(main1), ospec(main0)),
        scratch_shapes=[pltpu.VMEM((2, RWKV_HEADS, RWKV_HEAD, RWKV_HEAD), F32)],
        compiler_params=_cparams(("arbitrary", "arbitrary")),
        name="rwkv_scan",
    )(p, p, p, p, p, p, mu, vecs, wup, aup, gup, masks)


def _diff_kernel(q_ref, k_ref, v_ref, par_ref, o_ref, *, n_ctx):
    t = pl.program_id(2)
    lam = par_ref[0:1, :]
    gain = par_ref[1:2, :]

    def attend(nk):
        q = q_ref[...]
        k = k_ref[0:nk, :]
        v = v_ref[0:nk, :]

        def softmax_v(sl):
            s = _dot_nt(q[:, sl], k[:, sl])
            e = jnp.exp2(s - jnp.max(s, axis=-1, keepdims=True))
            return _dot(e, v) / jnp.sum(e, axis=-1, keepdims=True)

        o = softmax_v(slice(0, DIFF_DK)) - lam * softmax_v(slice(DIFF_DK, 2 * DIFF_DK))
        o = o * lax.rsqrt(jnp.mean(o * o, axis=-1, keepdims=True) + 1e-5)
        o_ref[...] = (o * gain).astype(o_ref.dtype)

    @pl.when(t * SEG < n_ctx)
    def _():
        attend(n_ctx)

    @pl.when(t * SEG >= n_ctx)
    def _():
        attend(k_ref.shape[0])


def _diff_attn(qk, p, par, bsz, n_ctx, ltot):
    rows = bsz * ltot
    nt = ltot // SEG
    voff = (P_DIFF + 2 * DIFF_W) // DIFF_DV
    return pl.pallas_call(
        functools.partial(_diff_kernel, n_ctx=n_ctx),
        out_shape=jax.ShapeDtypeStruct((rows, DIFF_W), BF16),
        grid=(bsz, DIFF_HEADS, nt),
        in_specs=[pl.BlockSpec((SEG, DIFF_DV), lambda b, h, t: (b * nt + t, h)),
                  pl.BlockSpec((ltot, DIFF_DV), lambda b, h, t: (b, DIFF_HEADS + h)),
                  pl.BlockSpec((ltot, DIFF_DV), lambda b, h, t: (b, voff + h)),
                  pl.BlockSpec((8, DIFF_DV), lambda b, h, t: (0, 0))],
        out_specs=pl.BlockSpec((SEG, DIFF_DV), lambda b, h, t: (b * nt + t, h)),
        compiler_params=_cparams(("arbitrary", "arbitrary", "arbitrary")),
        name="diff_attn",
    )(qk, qk, p, par)


def _win_kernel(sink_ref, q_ref, kp_ref, kc_ref, kn_ref, kx_ref, vp_ref, vc_ref, vn_ref, vx_ref,
                o_ref, *, n_ctx, n_lat):
    blk = pl.program_id(1)
    ncb = n_ctx // QBLK
    nq = WIN_GROUP * QBLK

    def run(keys, vals, mask):
        outs = []
        for g in range(WIN_KV_HEADS):
            ksl = slice(g * WIN_HEAD, (g + 1) * WIN_HEAD)
            qg = jnp.concatenate(
                [q_ref[:, (g * WIN_GROUP + j) * WIN_HEAD:(g * WIN_GROUP + j + 1) * WIN_HEAD]
                 for j in range(WIN_GROUP)], axis=0)
            s = _dot_nt(qg, keys[:, ksl])
            if mask is not None:
                s = jnp.where(mask, s, NEG_INF)
            hrow = lax.broadcasted_iota(jnp.int32, (nq, 1), 0) // QBLK
            sink = jnp.zeros((nq, 1), F32)
            for j in range(WIN_GROUP):
                sink = jnp.where(hrow == j, sink_ref[g * WIN_GROUP + j] * LOG2E, sink)
            m = jnp.maximum(jnp.max(s, axis=-1, keepdims=True), sink)
            e = jnp.exp2(s - m)
            den = jnp.sum(e, axis=-1, keepdims=True) + jnp.exp2(sink - m)
            og = _dot(e, vals[:, ksl]) / den
            outs.extend(og[j * QBLK:(j + 1) * QBLK] for j in range(WIN_GROUP))
        o_ref[...] = jnp.concatenate(outs, axis=1).astype(o_ref.dtype)

    @pl.when(blk < ncb)
    def _():
        run(kx_ref[...], vx_ref[...], None)

    @pl.when(blk >= ncb)
    def _():
        j = blk - ncb
        keys = jnp.concatenate([kp_ref[...], kc_ref[...], kn_ref[...], kx_ref[...]], axis=0)
        vals = jnp.concatenate([vp_ref[...], vc_ref[...], vn_ref[...], vx_ref[...]], axis=0)
        nk = 3 * QBLK + n_ctx
        qpos = j * QBLK + lax.broadcasted_iota(jnp.int32, (nq, nk), 0) % QBLK
        col = lax.broadcasted_iota(jnp.int32, (nq, nk), 1)
        kpos = (j - 1) * QBLK + col
        band = (jnp.abs(qpos - kpos) <= WINDOW) & (kpos >= 0) & (kpos < n_lat)
        run(keys, vals, band | (col >= 3 * QBLK))


def _win_attn(sink, qk, p, bsz, n_ctx, n_lat):
    ltot = n_ctx + n_lat
    rows = bsz * ltot
    nb = ltot // QBLK
    ncb = n_ctx // QBLK
    koff = WIN_W // WIN_KV_W
    voff = (P_WIN + WIN_W + WIN_KV_W) // WIN_KV_W

    def lat_blk(b, t, shift):
        j = jnp.clip(t - ncb + shift, 0, nb - ncb - 1)
        return b * nb + ncb + j

    kspec = lambda shift, c: pl.BlockSpec((QBLK, WIN_KV_W), lambda b, t, s: (lat_blk(b, t, shift), c))
    xspec = lambda c: pl.BlockSpec((n_ctx, WIN_KV_W), lambda b, t, s: (b * (ltot // n_ctx), c))
    return pl.pallas_call(
        functools.partial(_win_kernel, n_ctx=n_ctx, n_lat=n_lat),
        out_shape=jax.ShapeDtypeStruct((rows, WIN_W), BF16),
        grid_spec=pltpu.PrefetchScalarGridSpec(
            num_scalar_prefetch=1,
            grid=(bsz, nb),
            in_specs=[pl.BlockSpec((QBLK, WIN_W), lambda b, t, s: (b * nb + t, 0)),
                      kspec(-1, koff), kspec(0, koff), kspec(1, koff), xspec(koff),
                      kspec(-1, voff), kspec(0, voff), kspec(1, voff), xspec(voff)],
            out_specs=pl.BlockSpec((QBLK, WIN_W), lambda b, t, s: (b * nb + t, 0))),
        compiler_params=_cparams(("arbitrary", "arbitrary")),
        name="win_attn",
    )(sink, qk, qk, qk, qk, qk, p, p, p, p)


def _merge_kernel(ya_ref, yb_ref, yc_ref, ga_ref, gb_ref, gc_ref, wa_ref, wb_ref, wc_ref, o_ref):
    z = _sigmoid(ga_ref[...].astype(F32)) * _dot(ya_ref[...], wa_ref[...])
    z = z + _sigmoid(gb_ref[...].astype(F32)) * _dot(yb_ref[...], wb_ref[...])
    z = z + _sigmoid(gc_ref[...].astype(F32)) * _dot(yc_ref[...], wc_ref[...])
    o_ref[...] = z.astype(o_ref.dtype)


def _merge(ya, yb, yc, p, w_branch, layer, tm, tn):
    m = ya.shape[0]
    assert RWKV_W == DIFF_W and (RWKV_W + DIFF_W) % WIN_W == 0
    nj = D_MODEL // tn
    goff = P_GATE // tn
    gspec = lambda br: pl.BlockSpec((tm, tn), lambda j, i: (i, goff + br * nj + j))
    return pl.pallas_call(
        _merge_kernel,
        out_shape=jax.ShapeDtypeStruct((m, D_MODEL), BF16),
        grid=(nj, m // tm),
        in_specs=[pl.BlockSpec((tm, RWKV_W), lambda j, i: (i, 0)),
                  pl.BlockSpec((tm, DIFF_W), lambda j, i: (i, 0)),
                  pl.BlockSpec((tm, WIN_W), lambda j, i: (i, 0)),
                  gspec(0), gspec(1), gspec(2),
                  pl.BlockSpec((None, RWKV_W, tn), lambda j, i: (layer, 0, j)),
                  pl.BlockSpec((None, DIFF_W, tn), lambda j, i: (layer, 1, j)),
                  pl.BlockSpec((None, WIN_W, tn), lambda j, i: (layer, (RWKV_W + DIFF_W) // WIN_W, j))],
        out_specs=pl.BlockSpec((tm, tn), lambda j, i: (i, j)),
        compiler_params=_cparams(("arbitrary", "arbitrary")),
        name="merge",
    )(ya, yb, yc, p, p, p, w_branch, w_branch, w_branch)


def _expert_kernel(be_ref, nu_ref, x_ref, gw_ref, w1_ref, w3_ref, w2_ref, o_ref, w1b, w3b, w2b):
    i = pl.program_id(0)
    prev = be_ref[jnp.maximum(i - 1, 0)]

    @pl.when(jnp.logical_or(i == 0, be_ref[i] != prev))
    def _():
        w1b[...] = w1_ref[...].astype(BF16)
        w3b[...] = w3_ref[...].astype(BF16)
        w2b[...] = w2_ref[...].astype(BF16)

    @pl.when(i < nu_ref[0])
    def _():
        x = x_ref[...]
        h1 = _dot(x, w1b[...])
        h = h1 * _sigmoid(h1) * _dot(x, w3b[...])
        o_ref[...] = (_dot(h, w2b[...]) * gw_ref[...]).astype(o_ref.dtype)

    @pl.when(i >= nu_ref[0])
    def _():
        o_ref[...] = jnp.zeros_like(o_ref)


def _experts(blk_expert, n_used, xb, gw, w1, w3, w2, layer):
    rows = xb.shape[0]
    nb = rows // MOE_BLK
    return pl.pallas_call(
        _expert_kernel,
        out_shape=jax.ShapeDtypeStruct((rows, D_MODEL), BF16),
        grid_spec=pltpu.PrefetchScalarGridSpec(
            num_scalar_prefetch=2,
            grid=(nb,),
            in_specs=[pl.BlockSpec((MOE_BLK, D_MODEL), lambda i, be, nu: (i, 0)),
                      pl.BlockSpec((MOE_BLK, 1), lambda i, be, nu: (i, 0)),
                      pl.BlockSpec((None, None, D_MODEL, D_EXPERT), lambda i, be, nu: (layer, be[i], 0, 0)),
                      pl.BlockSpec((None, None, D_MODEL, D_EXPERT), lambda i, be, nu: (layer, be[i], 0, 0)),
                      pl.BlockSpec((None, None, D_EXPERT, D_MODEL), lambda i, be, nu: (layer, be[i], 0, 0))],
            out_specs=pl.BlockSpec((MOE_BLK, D_MODEL), lambda i, be, nu: (i, 0)),
            scratch_shapes=[pltpu.VMEM((D_MODEL, D_EXPERT), BF16),
                            pltpu.VMEM((D_MODEL, D_EXPERT), BF16),
                            pltpu.VMEM((D_EXPERT, D_MODEL), BF16)]),
        compiler_params=_cparams(("arbitrary",)),
        name="experts",
    )(blk_expert, n_used, xb, gw, w1, w3, w2)


def _rope_tables(n_ctx, n_lat):
    rows = n_lat // GRID_W
    row = jnp.repeat(jnp.arange(rows), GRID_W).astype(F32)
    col = (jnp.arange(rows * GRID_W) % GRID_W).astype(F32)
    inv = ROPE_BASE ** (-jnp.arange(ROPE_AX_FREQS, dtype=F32) / ROPE_AX_FREQS)
    ang = jnp.stack([row[:, None] * inv, col[:, None] * inv], axis=1)
    cos, sin = jnp.cos(ang), jnp.sin(ang)
    cos4 = jnp.stack([cos, cos], axis=2).reshape(n_lat, 4 * ROPE_AX_FREQS)
    sin4 = jnp.stack([-sin, sin], axis=2).reshape(n_lat, 4 * ROPE_AX_FREQS)
    cos4 = jnp.concatenate([jnp.ones((n_ctx, 64), F32), cos4], axis=0)
    sin4 = jnp.concatenate([jnp.zeros((n_ctx, 64), F32), sin4], axis=0)
    return jnp.stack([jnp.tile(cos4, (1, LANE // 64)), jnp.tile(sin4, (1, LANE // 64))])


def _moe(route, v, w1, w3, w2, layer):
    t = v.shape[0]
    experts = route[:, :EXPERT_TOP_K].astype(jnp.int32)
    gates = route[:, EXPERT_TOP_K:2 * EXPERT_TOP_K]
    k = EXPERT_TOP_K
    a = t * k
    e_n = N_EXPERTS
    nb = -(-a // MOE_BLK) + e_n
    e_flat = experts.T.reshape(a)
    iota = jnp.arange(a, dtype=jnp.int32)
    e_s, order, g_s = lax.sort((e_flat, iota, gates.T.reshape(a)), num_keys=1, is_stable=True)
    ids = jnp.arange(e_n, dtype=jnp.int32)
    start = jnp.sum(e_s[None, :] < ids[:, None], axis=1, dtype=jnp.int32)
    counts = jnp.sum(e_s[None, :] == ids[:, None], axis=1, dtype=jnp.int32)
    padded = (counts + MOE_BLK - 1) // MOE_BLK * MOE_BLK
    pad_end = jnp.cumsum(padded)
    pad_start = pad_end - padded
    shift = jnp.sum(jnp.where(e_s[:, None] == ids[None, :], (pad_start - start)[None, :], 0), axis=1)
    pos = lax.sort((order, iota + shift), num_keys=1)[1]
    blk_first = jnp.arange(nb, dtype=jnp.int32) * MOE_BLK
    blk_expert = jnp.minimum(jnp.sum(pad_end[None, :] <= blk_first[:, None], axis=1, dtype=jnp.int32), e_n - 1)
    n_used = (pad_end[-1:] // MOE_BLK).astype(jnp.int32)
    rank = blk_first[:, None] + jnp.arange(MOE_BLK, dtype=jnp.int32)[None, :] - pad_start[blk_expert][:, None]
    filled = (rank < counts[blk_expert][:, None]).reshape(nb * MOE_BLK)
    sidx = jnp.clip(start[blk_expert][:, None] + rank, 0, a - 1).reshape(nb * MOE_BLK)
    slot_tok = jnp.where(filled, order.at[sidx].get(mode="promise_in_bounds") % t, 0)
    slot_gate = jnp.where(filled, g_s.at[sidx].get(mode="promise_in_bounds"), 0.0)
    xb = v.at[slot_tok].get(mode="promise_in_bounds")
    ys = _experts(blk_expert, n_used, xb, slot_gate[:, None], w1, w3, w2, layer)
    yg = lax.optimization_barrier(ys.at[pos].get(mode="promise_in_bounds"))
    return yg.reshape(k, t, D_MODEL)


def kernel(x, c, ctx, c_ctx, w_mod, b_mod, w_in, rwkv_mu, rwkv_w0, rwkv_w_up, rwkv_a0, rwkv_a_up,
           rwkv_g_up, rwkv_kvec, rwkv_lnx, diff_lam, diff_subln, win_sink, w_branch, w_out, ln_g, ln_b,
           w_rg, b_rg, w_re, b_re, w1, w3, w2):
    bsz, n_lat, dm = x.shape
    n_ctx = ctx.shape[1]
    depth = w_mod.shape[0]
    ltot = n_ctx + n_lat
    rows = bsz * ltot
    dn_alpha = (2 * depth) ** 0.25
    assert dm == D_MODEL and n_ctx % SEG == 0 and n_lat % SEG == 0 and ltot % n_ctx == 0

    nt = ltot // SEG
    cs = _rope_tables(n_ctx, n_lat)
    fwd = jnp.tril(jnp.ones((CHUNK, CHUNK), F32))
    masks = jnp.stack([jnp.stack([fwd - jnp.eye(CHUNK, dtype=F32), fwd]),
                       jnp.stack([fwd.T - jnp.eye(CHUNK, dtype=F32), fwd.T])])
    head_id = jnp.arange(RWKV_W) // RWKV_HEAD
    head_avg = ((head_id[:, None] == head_id[None, :]).astype(F32) / RWKV_HEAD).astype(BF16)

    xs = jnp.concatenate([ctx, x], axis=1).reshape(rows, dm)
    cvec = jnp.concatenate([c_ctx[None, :], c], axis=0)
    cpad = jnp.zeros((32, dm), F32).at[:bsz + 1].set(jax.nn.silu(cvec))
    w_in_p = jnp.concatenate([w_in[:, :, :RWKV_IN], jnp.zeros((depth, dm, P_GATE - RWKV_IN), F32),
                              w_in[:, :, GATE_OFF:], w_in[:, :, DIFF_OFF:GATE_OFF]], axis=2).astype(BF16)
    w_branch_b = w_branch.astype(BF16)
    w_out_b = w_out.astype(BF16)
    w_r = jnp.zeros((depth, dm, LANE), F32).at[:, :, :N_GROUPS].set(w_rg)
    w_r = w_r.at[:, :, N_GROUPS:N_GROUPS + N_EXPERTS].set(w_re)
    b_r = jnp.zeros((depth, 8, LANE), F32).at[:, :, :N_GROUPS].set(b_rg[:, None, :])
    b_r = b_r.at[:, :, N_GROUPS:N_GROUPS + N_EXPERTS].set(b_re[:, None, :])

    mods = [_mm(cpad, w_mod, i, F32, 32, 1024, "mod")[:bsz + 1] + b_mod[i] for i in range(depth)]

    def table(gate, shift, scale):
        def both(m, j):
            v = m[:, j * dm:(j + 1) * dm]
            return jnp.stack([jnp.broadcast_to(v[0], (bsz, dm)), v[1:]], axis=1)
        t = jnp.stack([both(*gate), both(*shift), both(*scale)], axis=2)
        return jnp.pad(t, ((0, 0), (0, 0), (0, 5), (0, 0))).reshape(2 * bsz, 8, dm)

    u = _modulate0(xs, table((mods[0], 0), (mods[0], 0), (mods[0], 1)), nt)
    for i in range(depth):
        last = i == depth - 1
        lam_init = 0.8 - 0.6 * math.exp(-0.3 * i)
        mod = mods[i]
        p = _mm(u, w_in_p, i, BF16, 512, 1024, "in_proj")

        vecs = jnp.zeros((2, 8, RWKV_W), F32)
        vecs = vecs.at[:, 0].set(rwkv_w0[i]).at[:, 1].set(rwkv_a0[i])
        vecs = vecs.at[:, 2:5].set(jnp.broadcast_to(rwkv_kvec[i][None], (2, 3, RWKV_W)))
        wup = jnp.zeros((2, 2 * DECAY_LORA, RWKV_W), BF16)
        aup = jnp.zeros((2, 2 * AAA_LORA, RWKV_W), BF16)
        for d in range(2):
            wup = wup.at[d, d * DECAY_LORA:(d + 1) * DECAY_LORA].set(rwkv_w_up[i, d].astype(BF16))
            aup = aup.at[d, d * AAA_LORA:(d + 1) * AAA_LORA].set(rwkv_a_up[i, d].astype(BF16))
        scan = _rwkv_scan(p, rwkv_mu[i], vecs, wup, aup, rwkv_g_up[i].astype(BF16), masks, bsz, n_ctx, n_lat)
        ya = _readout(*scan, rwkv_lnx[i], head_avg, 512)

        dqk, wqk = _rope(p, cs, nt)
        lf = diff_lam[i]
        lam = jnp.exp(jnp.sum(lf[0] * lf[1])) - jnp.exp(jnp.sum(lf[2] * lf[3])) + lam_init
        par = jnp.zeros((8, DIFF_DV), F32).at[0].set(lam).at[1].set(diff_subln[i] * (1 - lam_init))
        yb = _diff_attn(dqk, p, par, bsz, n_ctx, ltot)
        yc = _win_attn(win_sink[i], wqk, p, bsz, n_ctx, n_lat)
        z = _merge(ya, yb, yc, p, w_branch_b, i, 512, 1024)

        lnp = lambda j: jnp.zeros((8, dm), F32).at[0].set(ln_g[i, j]).at[1].set(ln_b[i, j])
        xs, v, route = _mix_out(z, xs, w_out_b, i, table((mod, 2), (mod, 3), (mod, 4)), lnp(0),
                                w_r[i], b_r[i], nt, dn_alpha)

        if last:
            lat = lambda t: t.reshape(bsz, ltot, -1)[:, n_ctx:].reshape(bsz * n_lat, -1)
            yg = _moe(lat(route), lat(v), w1, w3, w2, i)
            tab = table((mod, 5), (mod, 0), (mod, 1))
            (out,) = _moe_out(xs, yg, tab, lnp(1), bsz, nt, n_ctx // SEG, dn_alpha, False)
            return out.reshape(bsz, n_lat, dm)
        yg = _moe(route, v, w1, w3, w2, i)
        tab = table((mod, 5), (mods[i + 1], 0), (mods[i + 1], 1))
        xs, u = _moe_out(xs, yg, tab, lnp(1), bsz, nt, 0, dn_alpha, True)
    return None
```

```python
import functools
import math

import jax
import jax.numpy as jnp
from jax import lax
from jax.experimental import pallas as pl
from jax.experimental.pallas import tpu as pltpu

F32 = jnp.float32
BF16 = jnp.bfloat16

D_MODEL = 2048
GRID_W = 64
RWKV_HEADS = 12
RWKV_HEAD = 64
RWKV_W = RWKV_HEADS * RWKV_HEAD
DECAY_LORA = 64
AAA_LORA = 64
GATE_LORA = 128
RWKV_GN_EPS = 64e-5
DIFF_HEADS = 6
DIFF_DK = 64
DIFF_DV = 2 * DIFF_DK
DIFF_W = DIFF_HEADS * DIFF_DV
DIFF_SCALE = DIFF_DK ** -0.5
WIN_Q_HEADS = 8
WIN_KV_HEADS = 2
WIN_GROUP = WIN_Q_HEADS // WIN_KV_HEADS
WIN_HEAD = 64
WIN_W = WIN_Q_HEADS * WIN_HEAD
WIN_KV_W = WIN_KV_HEADS * WIN_HEAD
WIN_SCALE = WIN_HEAD ** -0.5
WINDOW = 128
QBLK = WINDOW
MIX_W = RWKV_W + DIFF_W + WIN_W
N_BRANCH = 3
ROPE_BASE = 10000.0
ROPE_AX_FREQS = 16
RWKV_IN = 3 * RWKV_W + 2 * DECAY_LORA + 2 * AAA_LORA + GATE_LORA
DIFF_IN = 3 * DIFF_W
WIN_IN = WIN_W + 2 * WIN_KV_W
DIFF_OFF = RWKV_IN
WIN_OFF = DIFF_OFF + DIFF_IN
GATE_OFF = WIN_OFF + WIN_IN
N_IN = GATE_OFF + N_BRANCH * D_MODEL
N_GROUPS = 4
EXPERTS_PER_GROUP = 8
N_EXPERTS = N_GROUPS * EXPERTS_PER_GROUP
EXPERT_TOP_K = 2
D_EXPERT = D_MODEL // 4
MOE_BLK = 256
ADA_EPS = 1e-6
LN_EPS = 1e-5
NEG_INF = -1e30
LOG2E = math.log2(math.e)

LANE = 128
SEG = 256
P_RWKV = 0
P_GATE = 3072
P_DIFF = P_GATE + N_BRANCH * D_MODEL
P_WIN = P_DIFF + DIFF_IN
P_COLS = P_WIN + WIN_IN
CHUNK = 64
VMEM_LIMIT = 56 * 1024 * 1024


def _cparams(sem):
    return pltpu.CompilerParams(dimension_semantics=sem, vmem_limit_bytes=VMEM_LIMIT)


def _dot(a, b):
    return jnp.dot(a.astype(BF16), b.astype(BF16), preferred_element_type=F32)


def _dot_nt(a, b):
    return lax.dot_general(a.astype(BF16), b.astype(BF16), (((1,), (1,)), ((), ())),
                           preferred_element_type=F32)


def _dot_tn(a, b):
    return lax.dot_general(a.astype(BF16), b.astype(BF16), (((0,), (0,)), ((), ())),
                           preferred_element_type=F32)


def _split(x):
    hi = x.astype(BF16)
    lo = (x - hi.astype(F32)).astype(BF16)
    return hi, lo


def _dot3(a, b):
    ah, al = _split(a)
    bh, bl = _split(b)
    d = functools.partial(jnp.dot, preferred_element_type=F32)
    return d(ah, bh) + (d(ah, bl) + d(al, bh))


def _sigmoid(x):
    return 1.0 / (1.0 + jnp.exp(-x))


def _mm_kernel(a_ref, w_ref, o_ref):
    o_ref[...] = _dot(a_ref[...], w_ref[...]).astype(o_ref.dtype)


def _mm(a, w, layer, out_dtype, tm, tn, name):
    m, k = a.shape
    n = w.shape[2]
    return pl.pallas_call(
        _mm_kernel,
        out_shape=jax.ShapeDtypeStruct((m, n), out_dtype),
        grid=(n // tn, m // tm),
        in_specs=[pl.BlockSpec((tm, k), lambda j, i: (i, 0)),
                  pl.BlockSpec((None, k, tn), lambda j, i: (layer, 0, j))],
        out_specs=pl.BlockSpec((tm, tn), lambda j, i: (i, j)),
        compiler_params=_cparams(("arbitrary", "arbitrary")),
        name=name,
    )(a, w)


def _norm_rows(x, eps):
    mu = jnp.mean(x, axis=-1, keepdims=True)
    xc = x - mu
    return xc * lax.rsqrt(jnp.mean(xc * xc, axis=-1, keepdims=True) + eps)


def _route_rows(logits):
    col = lax.broadcasted_iota(jnp.int32, logits.shape, 1).astype(F32)
    big = float(LANE)
    is_g = col < N_GROUPS
    lg = jnp.where(is_g, logits, NEG_INF)
    g_max = jnp.max(lg, axis=-1, keepdims=True)
    g_sum = jnp.sum(jnp.where(is_g, jnp.exp(lg - g_max), 0.0), axis=-1, keepdims=True)
    pg_top = 1.0 / g_sum
    g_idx = jnp.min(jnp.where(is_g & (lg == g_max), col, big), axis=-1, keepdims=True)
    lo = N_GROUPS + EXPERTS_PER_GROUP * g_idx
    sel = (col >= lo) & (col < lo + EXPERTS_PER_GROUP)
    le = jnp.where(sel, logits, NEG_INF)
    m1 = jnp.max(le, axis=-1, keepdims=True)
    den = jnp.sum(jnp.where(sel, jnp.exp(le - m1), 0.0), axis=-1, keepdims=True)
    i1 = jnp.min(jnp.where(sel & (le == m1), col, big), axis=-1, keepdims=True)
    rest = sel & (col != i1)
    le2 = jnp.where(rest, logits, NEG_INF)
    m2 = jnp.max(le2, axis=-1, keepdims=True)
    i2 = jnp.min(jnp.where(rest & (le2 == m2), col, big), axis=-1, keepdims=True)
    p1 = 1.0 / den
    p2 = jnp.exp(m2 - m1) / den
    tot = p1 + p2
    out = jnp.where(col == 0.0, i1 - N_GROUPS, 0.0)
    out = jnp.where(col == 1.0, i2 - N_GROUPS, out)
    out = jnp.where(col == 2.0, pg_top * p1 / tot, out)
    return jnp.where(col == 3.0, pg_top * p2 / tot, out)


def _mod_kernel(x_ref, tab_ref, u_ref):
    u_ref[...] = (_norm_rows(x_ref[...], ADA_EPS) * (1.0 + tab_ref[0, 2:3, :]) + tab_ref[0, 1:2, :]).astype(u_ref.dtype)


def _modulate0(x, tab, nt):
    rows, dm = x.shape
    return pl.pallas_call(
        _mod_kernel,
        out_shape=jax.ShapeDtypeStruct((rows, dm), BF16),
        grid=(rows // SEG,),
        in_specs=[pl.BlockSpec((SEG, dm), lambda i: (i, 0)),
                  pl.BlockSpec((1, 8, dm), lambda i: (2 * (i // nt) + jnp.minimum(i % nt, 1), 0, 0))],
        out_specs=pl.BlockSpec((SEG, dm), lambda i: (i, 0)),
        compiler_params=_cparams(("arbitrary",)),
        name="modulate0",
    )(x, tab)


def _mix_out_kernel(z_ref, x_ref, w_ref, tab_ref, lnp_ref, wr_ref, br_ref, xo_ref, vo_ref, ro_ref, *, alpha):
    m = _dot(z_ref[...], w_ref[...])
    xn = _norm_rows(alpha * x_ref[...] + tab_ref[0, 0:1, :] * m, LN_EPS) * lnp_ref[0:1, :] + lnp_ref[1:2, :]
    xo_ref[...] = xn
    v = _norm_rows(xn, ADA_EPS) * (1.0 + tab_ref[0, 2:3, :]) + tab_ref[0, 1:2, :]
    vo_ref[...] = v.astype(vo_ref.dtype)
    ro_ref[...] = _route_rows(_dot3(v, wr_ref[...]) + br_ref[0:1, :])


def _mix_out(z, x, w_out, layer, tab, lnp, w_r, b_r, nt, alpha):
    rows, dm = x.shape
    row = lambda i: (i, 0)
    const = lambda i: (0, 0)
    return pl.pallas_call(
        functools.partial(_mix_out_kernel, alpha=alpha),
        out_shape=(jax.ShapeDtypeStruct((rows, dm), F32), jax.ShapeDtypeStruct((rows, dm), BF16),
                   jax.ShapeDtypeStruct((rows, LANE), F32)),
        grid=(rows // SEG,),
        in_specs=[pl.BlockSpec((SEG, dm), row), pl.BlockSpec((SEG, dm), row),
                  pl.BlockSpec((None, dm, dm), lambda i: (layer, 0, 0)),
                  pl.BlockSpec((1, 8, dm), lambda i: (2 * (i // nt) + jnp.minimum(i % nt, 1), 0, 0)),
                  pl.BlockSpec((8, dm), const), pl.BlockSpec((dm, LANE), const), pl.BlockSpec((8, LANE), const)],
        out_specs=(pl.BlockSpec((SEG, dm), row), pl.BlockSpec((SEG, dm), row), pl.BlockSpec((SEG, LANE), row)),
        compiler_params=_cparams(("arbitrary",)),
        name="mix_out",
    )(z, x, w_out, tab, lnp, w_r, b_r)


def _moe_out_kernel(x_ref, y0_ref, y1_ref, tab_ref, lnp_ref, xo_ref, *u_ref, alpha):
    y = y0_ref[0].astype(F32) + y1_ref[0].astype(F32)
    xn = _norm_rows(alpha * x_ref[...] + tab_ref[0, 0:1, :] * y, LN_EPS) * lnp_ref[0:1, :] + lnp_ref[1:2, :]
    xo_ref[...] = xn
    if u_ref:
        u_ref[0][...] = (_norm_rows(xn, ADA_EPS) * (1.0 + tab_ref[0, 2:3, :]) + tab_ref[0, 1:2, :]).astype(BF16)


def _moe_out(x, yg, tab, lnp, bsz, nt, skip, alpha, emit_u):
    dm = x.shape[1]
    nk = nt - skip
    rows = bsz * nk * SEG
    xmap = lambda b, t: (b * nt + skip + t, 0)
    omap = lambda b, t: (b * nk + t, 0)
    out_shape = [jax.ShapeDtypeStruct((rows, dm), F32)]
    out_specs = [pl.BlockSpec((SEG, dm), omap)]
    if emit_u:
        out_shape.append(jax.ShapeDtypeStruct((rows, dm), BF16))
        out_specs.append(pl.BlockSpec((SEG, dm), omap))
    return pl.pallas_call(
        functools.partial(_moe_out_kernel, alpha=alpha),
        out_shape=tuple(out_shape),
        grid=(bsz, nk),
        in_specs=[pl.BlockSpec((SEG, dm), xmap),
                  pl.BlockSpec((1, SEG, dm), lambda b, t: (0, b * nk + t, 0)),
                  pl.BlockSpec((1, SEG, dm), lambda b, t: (1, b * nk + t, 0)),
                  pl.BlockSpec((1, 8, dm), lambda b, t: (2 * b + jnp.minimum(skip + t, 1), 0, 0)),
                  pl.BlockSpec((8, dm), lambda b, t: (0, 0))],
        out_specs=tuple(out_specs),
        compiler_params=_cparams(("arbitrary", "arbitrary")),
        name="moe_out",
    )(x, yg, yg, tab, lnp)


def _rope_kernel(pd_ref, pw_ref, cs_ref, dqk_ref, wqk_ref):
    cos = cs_ref[0]
    sin = cs_ref[1]
    first_half = lax.broadcasted_iota(jnp.int32, cos.shape, 1) % (2 * ROPE_AX_FREQS) < ROPE_AX_FREQS

    def rot(x, scale):
        x = x.astype(F32)
        sw = jnp.where(first_half, pltpu.roll(x, LANE - ROPE_AX_FREQS, axis=1), pltpu.roll(x, ROPE_AX_FREQS, axis=1))
        return ((x * cos + sw * sin) * scale).astype(BF16)

    for c in range(2 * DIFF_W // LANE):
        scale = DIFF_SCALE * LOG2E if c < DIFF_W // LANE else 1.0
        dqk_ref[:, c * LANE:(c + 1) * LANE] = rot(pd_ref[:, c * LANE:(c + 1) * LANE], scale)
    for c in range((WIN_W + WIN_KV_W) // LANE):
        scale = WIN_SCALE * LOG2E if c < WIN_W // LANE else 1.0
        wqk_ref[:, c * LANE:(c + 1) * LANE] = rot(pw_ref[:, c * LANE:(c + 1) * LANE], scale)


def _rope(p, cs, nt):
    rows = p.shape[0]
    wd, ww = 2 * DIFF_W, WIN_W + WIN_KV_W
    return pl.pallas_call(
        _rope_kernel,
        out_shape=(jax.ShapeDtypeStruct((rows, wd), BF16), jax.ShapeDtypeStruct((rows, ww), BF16)),
        grid=(rows // SEG,),
        in_specs=[pl.BlockSpec((SEG, wd), lambda i: (i, P_DIFF // wd)),
                  pl.BlockSpec((SEG, ww), lambda i: (i, P_WIN // ww)),
                  pl.BlockSpec((2, SEG, LANE), lambda i: (0, i % nt, 0))],
        out_specs=(pl.BlockSpec((SEG, wd), lambda i: (i, 0)), pl.BlockSpec((SEG, ww), lambda i: (i, 0))),
        compiler_params=_cparams(("arbitrary",)),
        name="rope",
    )(p, p, cs)


def _readout_kernel(y0_ref, y1_ref, bv0_ref, bv1_ref, g_ref, lnx_ref, avg_ref, o_ref):
    avg = avg_ref[...]

    def head_mean(t):
        hi, lo = _split(t)
        return jnp.dot(hi, avg, preferred_element_type=F32) + jnp.dot(lo, avg, preferred_element_type=F32)

    y = y0_ref[...] + y1_ref[...]
    dev = y - head_mean(y)
    yn = dev * lax.rsqrt(head_mean(dev * dev) + RWKV_GN_EPS) * lnx_ref[0:1, :] + lnx_ref[1:2, :]
    bonus = bv0_ref[...].astype(F32) + bv1_ref[...].astype(F32)
    o_ref[...] = ((yn + bonus) * g_ref[...].astype(F32)).astype(o_ref.dtype)


def _readout(y0, y1, bv0, bv1, g, lnx, avg, tm):
    rows = y0.shape[0]
    row = pl.BlockSpec((tm, RWKV_W), lambda i: (i, 0))
    return pl.pallas_call(
        _readout_kernel,
        out_shape=jax.ShapeDtypeStruct((rows, RWKV_W), BF16),
        grid=(rows // tm,),
        in_specs=[row, row, row, row, row,
                  pl.BlockSpec((2, RWKV_W), lambda i: (0, 0)),
                  pl.BlockSpec((RWKV_W, RWKV_W), lambda i: (0, 0))],
        out_specs=row,
        compiler_params=_cparams(("arbitrary",)),
        name="rwkv_readout",
    )(y0, y1, bv0, bv1, g, lnx, avg)


def _rwkv_features(d, chunk, p_ref, hp_ref, hn_ref, mu_ref, vec_ref, wup_ref, aup_ref, gup_ref, msk_ref,
                   bv_ref, g_ref, nc_ctx, nc_lat):
    n = CHUNK
    p = p_ref[...].astype(F32)
    first = jnp.logical_or(chunk == 0, chunk == nc_ctx)
    last = jnp.logical_or(chunk == nc_ctx - 1, chunk == nc_ctx + nc_lat - 1)
    hp = jnp.where(first, 0.0, hp_ref[15:16, :].astype(F32))
    hn = jnp.where(last, 0.0, hn_ref[0:1, :].astype(F32))
    row = lax.broadcasted_iota(jnp.int32, (n, 1), 0)
    prev = jnp.where(row == 0, hp, pltpu.roll(p, 1, axis=0))
    nxt = jnp.where(row == n - 1, hn, pltpu.roll(p, n - 1, axis=0))
    ps = p + mu_ref[0:1, :] * (prev - p) + mu_ref[1:2, :] * (nxt - p)

    r = ps[:, 0:RWKV_W]
    k = ps[:, RWKV_W:2 * RWKV_W]
    v = ps[:, 2 * RWKV_W:3 * RWKV_W]
    o = 3 * RWKV_W
    wd = jnp.tanh(ps[:, o:o + 2 * DECAY_LORA])
    ad = ps[:, o + 2 * DECAY_LORA:o + 2 * DECAY_LORA + 2 * AAA_LORA]
    gd = ps[:, o + 2 * DECAY_LORA + 2 * AAA_LORA:]
    w0 = vec_ref[d, 0:1, :]
    a0 = vec_ref[d, 1:2, :]
    k_k = vec_ref[d, 2:3, :]
    k_a = vec_ref[d, 3:4, :]
    r_k = vec_ref[d, 4:5, :]
    w_log = w0 + _dot(wd, wup_ref[d])
    a = _sigmoid(a0 + _dot(ad, aup_ref[d]))
    if g_ref is not None:
        g_ref[...] = _dot(_sigmoid(gd), gup_ref[...]).astype(g_ref.dtype)
    logw = -math.exp(-0.5) * _sigmoid(w_log)

    strict = msk_ref[d, 0] > 0.5
    incl_f = msk_ref[d, 1]
    incl = incl_f > 0.5
    lw_hi, lw_lo = _split(logw)
    incl_b = incl_f.astype(BF16)
    cl = (jnp.dot(incl_b, lw_hi, preferred_element_type=F32)
          + jnp.dot(incl_b, lw_lo, preferred_element_type=F32))
    tot = jnp.sum(logw, axis=0, keepdims=True)
    e_in = jnp.exp(cl)
    e_ex = jnp.exp(cl - logw)
    e_inv = jnp.exp(-cl)
    e_end = jnp.exp(tot - cl)
    p_all = jnp.exp(tot)

    sls = [slice(h * RWKV_HEAD, (h + 1) * RWKV_HEAD) for h in range(RWKV_HEADS)]
    kk_n = k * k_k
    kk_sq = kk_n * kk_n
    kd_all = k * (1.0 + (a - 1.0) * k_a)
    rkd = r * kd_all * r_k
    inv_norm = [1.0 / jnp.maximum(jnp.sqrt(jnp.sum(kk_sq[:, sl], axis=-1, keepdims=True)), 1e-12) for sl in sls]
    bonus = [jnp.sum(rkd[:, sl], axis=-1, keepdims=True) for sl in sls]
    bv_ref[...] = jnp.concatenate([bonus[h] * v[:, sl] for h, sl in enumerate(sls)], axis=1).astype(bv_ref.dtype)
    kk_all = jnp.concatenate([kk_n[:, sl] * inv_norm[h] for h, sl in enumerate(sls)], axis=1)
    bd_all = kk_all * a
    return dict(
        v=v, strict=strict, incl=incl, p_all=p_all,
        kk_t=kk_all * e_ex,
        r_t=r * e_in,
        b_i=bd_all * e_inv, k_i=kd_all * e_inv,
        k_e=kd_all * e_end,
        b_e=bd_all * e_end)


def _rwkv_kernel(p0_ref, hp0_ref, hn0_ref, p1_ref, hp1_ref, hn1_ref, mu_ref, vec_ref, wup_ref, aup_ref,
                 gup_ref, msk_ref, y0_ref, y1_ref, bv0_ref, bv1_ref, g_ref, state, *, nc_ctx, nc_lat):
    i = pl.program_id(1)

    @pl.when(i == 0)
    def _():
        state[...] = jnp.zeros_like(state)

    n = CHUNK
    shared = (mu_ref, vec_ref, wup_ref, aup_ref, gup_ref, msk_ref)
    f = [_rwkv_features(0, i, p0_ref, hp0_ref, hn0_ref, *shared, bv0_ref, g_ref, nc_ctx, nc_lat),
         _rwkv_features(1, _rwkv_mirror(i, nc_ctx, nc_lat), p1_ref, hp1_ref, hn1_ref, *shared, bv1_ref, None,
                        nc_ctx, nc_lat)]

    eye_f = (lax.broadcasted_iota(jnp.int32, (n, n), 0) == lax.broadcasted_iota(jnp.int32, (n, n), 1)).astype(F32)
    zeros = jnp.zeros((n, RWKV_HEAD), F32)
    items = [(d, h) for d in range(2) for h in range(RWKV_HEADS)]
    idx = range(len(items))
    sl = lambda j: slice(items[j][1] * RWKV_HEAD, (items[j][1] + 1) * RWKV_HEAD)
    get = lambda name, j: f[items[j][0]][name][:, sl(j)]
    strict = [f[d]["strict"] for d, _ in items]
    incl = [f[d]["incl"] for d, _ in items]

    v_h = [get("v", j) for j in idx]
    kk_t = [get("kk_t", j) for j in idx]
    r_t = [get("r_t", j) for j in idx]
    gram = [_dot_nt(jnp.concatenate([kk_t[j], r_t[j]], axis=0),
                    jnp.concatenate([get("b_i", j), get("k_i", j)], axis=0)) for j in idx]
    l_k = [jnp.where(strict[j], gram[j][:n, n:], 0.0) for j in idx]
    pw = [jnp.where(strict[j], -gram[j][:n, :n], 0.0) for j in idx]
    t_inv = [eye_f + pw[j] for j in idx]
    lkv = [_dot(l_k[j], v_h[j]) for j in idx]
    pw = [_dot(pw[j], pw[j]) for j in idx]
    for _ in range(4):
        both = [_dot(jnp.concatenate([t_inv[j], pw[j]], axis=0), pw[j]) for j in idx]
        t_inv = [t_inv[j] + both[j][:n] for j in idx]
        pw = [both[j][n:] for j in idx]
    t_inv = [t_inv[j] + _dot(t_inv[j], pw[j]) for j in idx]
    tx = [_dot(t_inv[j], jnp.concatenate([kk_t[j], lkv[j]], axis=1)) for j in idx]
    rhs = [jnp.concatenate([jnp.concatenate([zeros, v_h[j]], axis=1), tx[j]], axis=0) for j in idx]
    m_r = [jnp.concatenate([jnp.where(incl[j], gram[j][n:, n:], 0.0),
                            jnp.where(incl[j], -gram[j][n:, :n], 0.0)], axis=1) for j in idx]
    top = [_dot(m_r[j], rhs[j]) for j in idx]
    bot_t = [_dot_tn(rhs[j], jnp.concatenate([get("k_e", j), -get("b_e", j)], axis=0)) for j in idx]
    s0 = [state[d, h] for d, h in items]
    ys = [_dot_nt(r_t[j] + top[j][:, :RWKV_HEAD], s0[j]) + top[j][:, RWKV_HEAD:] for j in idx]
    y0_ref[...] = jnp.concatenate(ys[:RWKV_HEADS], axis=1)
    y1_ref[...] = jnp.concatenate(ys[RWKV_HEADS:], axis=1)
    s1 = [_dot(s0[j], bot_t[j][:RWKV_HEAD]) for j in idx]
    for j, (d, h) in enumerate(items):
        state[d, h] = s0[j] * get("p_all", j) + s1[j] + bot_t[j][RWKV_HEAD:]


def _rwkv_mirror(i, nc_ctx, nc_lat):
    return jnp.where(i < nc_ctx, nc_ctx - 1 - i, 2 * nc_ctx + nc_lat - 1 - i)


def _rwkv_scan(p, mu, vecs, wup, aup, gup, masks, bsz, n_ctx, n_lat):
    ltot = n_ctx + n_lat
    nc_ctx, nc_lat = n_ctx // CHUNK, n_lat // CHUNK
    nc = nc_ctx + nc_lat
    rows = bsz * ltot
    hb = CHUNK // 16
    n_hblk = rows // 16
    chunk_of = (lambda i: i, lambda i: _rwkv_mirror(i, nc_ctx, nc_lat))

    def specs(d):
        main = lambda b, i: (b * nc + chunk_of[d](i), 0)
        prev = lambda b, i: (jnp.maximum((b * nc + chunk_of[d](i)) * hb - 1, 0), 0)
        nxt = lambda b, i: (jnp.minimum((b * nc + chunk_of[d](i) + 1) * hb, n_hblk - 1), 0)
        return main, [pl.BlockSpec((CHUNK, RWKV_IN), main), pl.BlockSpec((16, RWKV_IN), prev),
                      pl.BlockSpec((16, RWKV_IN), nxt)]

    (main0, in0), (main1, in1) = specs(0), specs(1)
    whole = lambda shape: pl.BlockSpec(shape, lambda b, i: (0,) * len(shape))
    out_f = jax.ShapeDtypeStruct((rows, RWKV_W), F32)
    out_bf = jax.ShapeDtypeStruct((rows, RWKV_W), BF16)
    ospec = lambda m: pl.BlockSpec((CHUNK, RWKV_W), m)
    kern = functools.partial(_rwkv_kernel, nc_ctx=nc_ctx, nc_lat=nc_lat)
    return pl.pallas_call(
        kern,
        out_shape=(out_f, out_f, out_bf, out_bf, out_bf),
        grid=(bsz, nc),
        in_specs=in0 + in1 + [whole((2, RWKV_IN)), whole((2, 8, RWKV_W)),
                              whole((2, 2 * DECAY_LORA, RWKV_W)), whole((2, 2 * AAA_LORA, RWKV_W)),
                              whole((GATE_LORA, RWKV_W)), whole((2, 2, CHUNK, CHUNK))],
        out_specs=(ospec(main0), ospec(main1), ospec(main0), ospec(main1), ospec(main0)),
        scratch_shapes=[pltpu.VMEM((2, RWKV_HEADS, RWKV_HEAD, RWKV_HEAD), F32)],
        compiler_params=_cparams(("arbitrary", "arbitrary")),
        name="rwkv_scan",
    )(p, p, p, p, p, p, mu, vecs, wup, aup, gup, masks)


def _diff_kernel(q_ref, k_ref, v_ref, par_ref, o_ref, *, n_ctx):
    t = pl.program_id(2)
    lam = par_ref[0:1, :]
    gain = par_ref[1:2, :]

    def attend(nk):
        q = q_ref[...]
        k = k_ref[0:nk, :]
        v = v_ref[0:nk, :]

        def softmax_v(sl):
            s = _dot_nt(q[:, sl], k[:, sl])
            e = jnp.exp2(s - jnp.max(s, axis=-1, keepdims=True))
            return _dot(e, v) / jnp.sum(e, axis=-1, keepdims=True)

        o = softmax_v(slice(0, DIFF_DK)) - lam * softmax_v(slice(DIFF_DK, 2 * DIFF_DK))
        o = o * lax.rsqrt(jnp.mean(o * o, axis=-1, keepdims=True) + 1e-5)
        o_ref[...] = (o * gain).astype(o_ref.dtype)

    @pl.when(t * SEG < n_ctx)
    def _():
        attend(n_ctx)

    @pl.when(t * SEG >= n_ctx)
    def _():
        attend(k_ref.shape[0])


def _diff_attn(qk, p, par, bsz, n_ctx, ltot):
    rows = bsz * ltot
    nt = ltot // SEG
    voff = (P_DIFF + 2 * DIFF_W) // DIFF_DV
    return pl.pallas_call(
        functools.partial(_diff_kernel, n_ctx=n_ctx),
        out_shape=jax.ShapeDtypeStruct((rows, DIFF_W), BF16),
        grid=(bsz, DIFF_HEADS, nt),
        in_specs=[pl.BlockSpec((SEG, DIFF_DV), lambda b, h, t: (b * nt + t, h)),
                  pl.BlockSpec((ltot, DIFF_DV), lambda b, h, t: (b, DIFF_HEADS + h)),
                  pl.BlockSpec((ltot, DIFF_DV), lambda b, h, t: (b, voff + h)),
                  pl.BlockSpec((8, DIFF_DV), lambda b, h, t: (0, 0))],
        out_specs=pl.BlockSpec((SEG, DIFF_DV), lambda b, h, t: (b * nt + t, h)),
        compiler_params=_cparams(("arbitrary", "arbitrary", "arbitrary")),
        name="diff_attn",
    )(qk, qk, p, par)


def _win_kernel(sink_ref, q_ref, kp_ref, kc_ref, kn_ref, kx_ref, vp_ref, vc_ref, vn_ref, vx_ref,
                o_ref, *, n_ctx, n_lat):
    blk = pl.program_id(1)
    ncb = n_ctx // QBLK
    nq = WIN_GROUP * QBLK

    def run(keys, vals, mask):
        outs = []
        for g in range(WIN_KV_HEADS):
            ksl = slice(g * WIN_HEAD, (g + 1) * WIN_HEAD)
            qg = jnp.concatenate(
                [q_ref[:, (g * WIN_GROUP + j) * WIN_HEAD:(g * WIN_GROUP + j + 1) * WIN_HEAD]
                 for j in range(WIN_GROUP)], axis=0)
            s = _dot_nt(qg, keys[:, ksl])
            if mask is not None:
                s = jnp.where(mask, s, NEG_INF)
            hrow = lax.broadcasted_iota(jnp.int32, (nq, 1), 0) // QBLK
            sink = jnp.zeros((nq, 1), F32)
            for j in range(WIN_GROUP):
                sink = jnp.where(hrow == j, sink_ref[g * WIN_GROUP + j] * LOG2E, sink)
            m = jnp.maximum(jnp.max(s, axis=-1, keepdims=True), sink)
            e = jnp.exp2(s - m)
            den = jnp.sum(e, axis=-1, keepdims=True) + jnp.exp2(sink - m)
            og = _dot(e, vals[:, ksl]) / den
            outs.extend(og[j * QBLK:(j + 1) * QBLK] for j in range(WIN_GROUP))
        o_ref[...] = jnp.concatenate(outs, axis=1).astype(o_ref.dtype)

    @pl.when(blk < ncb)
    def _():
        run(kx_ref[...], vx_ref[...], None)

    @pl.when(blk >= ncb)
    def _():
        j = blk - ncb
        keys = jnp.concatenate([kp_ref[...], kc_ref[...], kn_ref[...], kx_ref[...]], axis=0)
        vals = jnp.concatenate([vp_ref[...], vc_ref[...], vn_ref[...], vx_ref[...]], axis=0)
        nk = 3 * QBLK + n_ctx
        qpos = j * QBLK + lax.broadcasted_iota(jnp.int32, (nq, nk), 0) % QBLK
        col = lax.broadcasted_iota(jnp.int32, (nq, nk), 1)
        kpos = (j - 1) * QBLK + col
        band = (jnp.abs(qpos - kpos) <= WINDOW) & (kpos >= 0) & (kpos < n_lat)
        run(keys, vals, band | (col >= 3 * QBLK))


def _win_attn(sink, qk, p, bsz, n_ctx, n_lat):
    ltot = n_ctx + n_lat
    rows = bsz * ltot
    nb = ltot // QBLK
    ncb = n_ctx // QBLK
    koff = WIN_W // WIN_KV_W
    voff = (P_WIN + WIN_W + WIN_KV_W) // WIN_KV_W

    def lat_blk(b, t, shift):
        j = jnp.clip(t - ncb + shift, 0, nb - ncb - 1)
        return b * nb + ncb + j

    kspec = lambda shift, c: pl.BlockSpec((QBLK, WIN_KV_W), lambda b, t, s: (lat_blk(b, t, shift), c))
    xspec = lambda c: pl.BlockSpec((n_ctx, WIN_KV_W), lambda b, t, s: (b * (ltot // n_ctx), c))
    return pl.pallas_call(
        functools.partial(_win_kernel, n_ctx=n_ctx, n_lat=n_lat),
        out_shape=jax.ShapeDtypeStruct((rows, WIN_W), BF16),
        grid_spec=pltpu.PrefetchScalarGridSpec(
            num_scalar_prefetch=1,
            grid=(bsz, nb),
            in_specs=[pl.BlockSpec((QBLK, WIN_W), lambda b, t, s: (b * nb + t, 0)),
                      kspec(-1, koff), kspec(0, koff), kspec(1, koff), xspec(koff),
                      kspec(-1, voff), kspec(0, voff), kspec(1, voff), xspec(voff)],
            out_specs=pl.BlockSpec((QBLK, WIN_W), lambda b, t, s: (b * nb + t, 0))),
        compiler_params=_cparams(("arbitrary", "arbitrary")),
        name="win_attn",
    )(sink, qk, qk, qk, qk, qk, p, p, p, p)


def _merge_kernel(ya_ref, yb_ref, yc_ref, ga_ref, gb_ref, gc_ref, wa_ref, wb_ref, wc_ref, o_ref):
    z = _sigmoid(ga_ref[...].astype(F32)) * _dot(ya_ref[...], wa_ref[...])
    z = z + _sigmoid(gb_ref[...].astype(F32)) * _dot(yb_ref[...], wb_ref[...])
    z = z + _sigmoid(gc_ref[...].astype(F32)) * _dot(yc_ref[...], wc_ref[...])
    o_ref[...] = z.astype(o_ref.dtype)


def _merge(ya, yb, yc, p, w_branch, layer, tm, tn):
    m = ya.shape[0]
    assert RWKV_W == DIFF_W and (RWKV_W + DIFF_W) % WIN_W == 0
    nj = D_MODEL // tn
    goff = P_GATE // tn
    gspec = lambda br: pl.BlockSpec((tm, tn), lambda j, i: (i, goff + br * nj + j))
    return pl.pallas_call(
        _merge_kernel,
        out_shape=jax.ShapeDtypeStruct((m, D_MODEL), BF16),
        grid=(nj, m // tm),
        in_specs=[pl.BlockSpec((tm, RWKV_W), lambda j, i: (i, 0)),
                  pl.BlockSpec((tm, DIFF_W), lambda j, i: (i, 0)),
                  pl.BlockSpec((tm, WIN_W), lambda j, i: (i, 0)),
                  gspec(0), gspec(1), gspec(2),
                  pl.BlockSpec((None, RWKV_W, tn), lambda j, i: (layer, 0, j)),
                  pl.BlockSpec((None, DIFF_W, tn), lambda j, i: (layer, 1, j)),
                  pl.BlockSpec((None, WIN_W, tn), lambda j, i: (layer, (RWKV_W + DIFF_W) // WIN_W, j))],
        out_specs=pl.BlockSpec((tm, tn), lambda j, i: (i, j)),
        compiler_params=_cparams(("arbitrary", "arbitrary")),
        name="merge",
    )(ya, yb, yc, p, p, p, w_branch, w_branch, w_branch)


def _expert_kernel(be_ref, nu_ref, x_ref, gw_ref, w1_ref, w3_ref, w2_ref, o_ref, w1b, w3b, w2b):
    i = pl.program_id(0)
    prev = be_ref[jnp.maximum(i - 1, 0)]

    @pl.when(jnp.logical_or(i == 0, be_ref[i] != prev))
    def _():
        w1b[...] = w1_ref[...].astype(BF16)
        w3b[...] = w3_ref[...].astype(BF16)
        w2b[...] = w2_ref[...].astype(BF16)

    @pl.when(i < nu_ref[0])
    def _():
        x = x_ref[...]
        h1 = _dot(x, w1b[...])
        h = h1 * _sigmoid(h1) * _dot(x, w3b[...])
        o_ref[...] = (_dot(h, w2b[...]) * gw_ref[...]).astype(o_ref.dtype)

    @pl.when(i >= nu_ref[0])
    def _():
        o_ref[...] = jnp.zeros_like(o_ref)


def _experts(blk_expert, n_used, xb, gw, w1, w3, w2, layer):
    rows = xb.shape[0]
    nb = rows // MOE_BLK
    return pl.pallas_call(
        _expert_kernel,
        out_shape=jax.ShapeDtypeStruct((rows, D_MODEL), BF16),
        grid_spec=pltpu.PrefetchScalarGridSpec(
            num_scalar_prefetch=2,
            grid=(nb,),
            in_specs=[pl.BlockSpec((MOE_BLK, D_MODEL), lambda i, be, nu: (i, 0)),
                      pl.BlockSpec((MOE_BLK, 1), lambda i, be, nu: (i, 0)),
                      pl.BlockSpec((None, None, D_MODEL, D_EXPERT), lambda i, be, nu: (layer, be[i], 0, 0)),
                      pl.BlockSpec((None, None, D_MODEL, D_EXPERT), lambda i, be, nu: (layer, be[i], 0, 0)),
                      pl.BlockSpec((None, None, D_EXPERT, D_MODEL), lambda i, be, nu: (layer, be[i], 0, 0))],
            out_specs=pl.BlockSpec((MOE_BLK, D_MODEL), lambda i, be, nu: (i, 0)),
            scratch_shapes=[pltpu.VMEM((D_MODEL, D_EXPERT), BF16),
                            pltpu.VMEM((D_MODEL, D_EXPERT), BF16),
                            pltpu.VMEM((D_EXPERT, D_MODEL), BF16)]),
        compiler_params=_cparams(("arbitrary",)),
        name="experts",
    )(blk_expert, n_used, xb, gw, w1, w3, w2)


def _rope_tables(n_ctx, n_lat):
    rows = n_lat // GRID_W
    row = jnp.repeat(jnp.arange(rows), GRID_W).astype(F32)
    col = (jnp.arange(rows * GRID_W) % GRID_W).astype(F32)
    inv = ROPE_BASE ** (-jnp.arange(ROPE_AX_FREQS, dtype=F32) / ROPE_AX_FREQS)
    ang = jnp.stack([row[:, None] * inv, col[:, None] * inv], axis=1)
    cos, sin = jnp.cos(ang), jnp.sin(ang)
    cos4 = jnp.stack([cos, cos], axis=2).reshape(n_lat, 4 * ROPE_AX_FREQS)
    sin4 = jnp.stack([-sin, sin], axis=2).reshape(n_lat, 4 * ROPE_AX_FREQS)
    cos4 = jnp.concatenate([jnp.ones((n_ctx, 64), F32), cos4], axis=0)
    sin4 = jnp.concatenate([jnp.zeros((n_ctx, 64), F32), sin4], axis=0)
    return jnp.stack([jnp.tile(cos4, (1, LANE // 64)), jnp.tile(sin4, (1, LANE // 64))])


def _moe(route, v, w1, w3, w2, layer):
    t = v.shape[0]
    experts = route[:, :EXPERT_TOP_K].astype(jnp.int32)
    gates = route[:, EXPERT_TOP_K:2 * EXPERT_TOP_K]
    k = EXPERT_TOP_K
    a = t * k
    e_n = N_EXPERTS
    nb = -(-a // MOE_BLK) + e_n
    e_flat = experts.T.reshape(a)
    iota = jnp.arange(a, dtype=jnp.int32)
    e_s, order, g_s = lax.sort((e_flat, iota, gates.T.reshape(a)), num_keys=1, is_stable=True)
    ids = jnp.arange(e_n, dtype=jnp.int32)
    start = jnp.sum(e_s[None, :] < ids[:, None], axis=1, dtype=jnp.int32)
    counts = jnp.sum(e_s[None, :] == ids[:, None], axis=1, dtype=jnp.int32)
    padded = (counts + MOE_BLK - 1) // MOE_BLK * MOE_BLK
    pad_end = jnp.cumsum(padded)
    pad_start = pad_end - padded
    shift = jnp.sum(jnp.where(e_s[:, None] == ids[None, :], (pad_start - start)[None, :], 0), axis=1)
    pos = lax.sort((order, iota + shift), num_keys=1)[1]
    blk_first = jnp.arange(nb, dtype=jnp.int32) * MOE_BLK
    blk_expert = jnp.minimum(jnp.sum(pad_end[None, :] <= blk_first[:, None], axis=1, dtype=jnp.int32), e_n - 1)
    n_used = (pad_end[-1:] // MOE_BLK).astype(jnp.int32)
    rank = blk_first[:, None] + jnp.arange(MOE_BLK, dtype=jnp.int32)[None, :] - pad_start[blk_expert][:, None]
    filled = (rank < counts[blk_expert][:, None]).reshape(nb * MOE_BLK)
    sidx = jnp.clip(start[blk_expert][:, None] + rank, 0, a - 1).reshape(nb * MOE_BLK)
    spread = jnp.arange(nb * MOE_BLK, dtype=jnp.int32) % t
    slot_tok = jnp.where(filled, order.at[sidx].get(mode="promise_in_bounds") % t, spread)
    slot_gate = jnp.where(filled, g_s.at[sidx].get(mode="promise_in_bounds"), 0.0)
    xb = v.at[slot_tok].get(mode="promise_in_bounds")
    ys = _experts(blk_expert, n_used, xb, slot_gate[:, None], w1, w3, w2, layer)
    yg = lax.optimization_barrier(ys.at[pos].get(mode="promise_in_bounds"))
    return yg.reshape(k, t, D_MODEL)


def kernel(x, c, ctx, c_ctx, w_mod, b_mod, w_in, rwkv_mu, rwkv_w0, rwkv_w_up, rwkv_a0, rwkv_a_up,
           rwkv_g_up, rwkv_kvec, rwkv_lnx, diff_lam, diff_subln, win_sink, w_branch, w_out, ln_g, ln_b,
           w_rg, b_rg, w_re, b_re, w1, w3, w2):
    bsz, n_lat, dm = x.shape
    n_ctx = ctx.shape[1]
    depth = w_mod.shape[0]
    ltot = n_ctx + n_lat
    rows = bsz * ltot
    dn_alpha = (2 * depth) ** 0.25
    assert dm == D_MODEL and n_ctx % SEG == 0 and n_lat % SEG == 0 and ltot % n_ctx == 0

    nt = ltot // SEG
    cs = _rope_tables(n_ctx, n_lat)
    fwd = jnp.tril(jnp.ones((CHUNK, CHUNK), F32))
    masks = jnp.stack([jnp.stack([fwd - jnp.eye(CHUNK, dtype=F32), fwd]),
                       jnp.stack([fwd.T - jnp.eye(CHUNK, dtype=F32), fwd.T])])
    head_id = jnp.arange(RWKV_W) // RWKV_HEAD
    head_avg = ((head_id[:, None] == head_id[None, :]).astype(F32) / RWKV_HEAD).astype(BF16)

    xs = jnp.concatenate([ctx, x], axis=1).reshape(rows, dm)
    cvec = jnp.concatenate([c_ctx[None, :], c], axis=0)
    cpad = jnp.zeros((32, dm), F32).at[:bsz + 1].set(jax.nn.silu(cvec))
    w_in_p = jnp.concatenate([w_in[:, :, :RWKV_IN], jnp.zeros((depth, dm, P_GATE - RWKV_IN), F32),
                              w_in[:, :, GATE_OFF:], w_in[:, :, DIFF_OFF:GATE_OFF]], axis=2).astype(BF16)
    w_branch_b = w_branch.astype(BF16)
    w_out_b = w_out.astype(BF16)
    w_r = jnp.zeros((depth, dm, LANE), F32).at[:, :, :N_GROUPS].set(w_rg)
    w_r = w_r.at[:, :, N_GROUPS:N_GROUPS + N_EXPERTS].set(w_re)
    b_r = jnp.zeros((depth, 8, LANE), F32).at[:, :, :N_GROUPS].set(b_rg[:, None, :])
    b_r = b_r.at[:, :, N_GROUPS:N_GROUPS + N_EXPERTS].set(b_re[:, None, :])

    mods = [_mm(cpad, w_mod, i, F32, 32, 1024, "mod")[:bsz + 1] + b_mod[i] for i in range(depth)]

    def table(gate, shift, scale):
        def both(m, j):
            v = m[:, j * dm:(j + 1) * dm]
            return jnp.stack([jnp.broadcast_to(v[0], (bsz, dm)), v[1:]], axis=1)
        t = jnp.stack([both(*gate), both(*shift), both(*scale)], axis=2)
        return jnp.pad(t, ((0, 0), (0, 0), (0, 5), (0, 0))).reshape(2 * bsz, 8, dm)

    u = _modulate0(xs, table((mods[0], 0), (mods[0], 0), (mods[0], 1)), nt)
    for i in range(depth):
        last = i == depth - 1
        lam_init = 0.8 - 0.6 * math.exp(-0.3 * i)
        mod = mods[i]
        p = _mm(u, w_in_p, i, BF16, 512, 2048, "in_proj")

        vecs = jnp.zeros((2, 8, RWKV_W), F32)
        vecs = vecs.at[:, 0].set(rwkv_w0[i]).at[:, 1].set(rwkv_a0[i])
        vecs = vecs.at[:, 2:5].set(jnp.broadcast_to(rwkv_kvec[i][None], (2, 3, RWKV_W)))
        wup = jnp.zeros((2, 2 * DECAY_LORA, RWKV_W), BF16)
        aup = jnp.zeros((2, 2 * AAA_LORA, RWKV_W), BF16)
        for d in range(2):
            wup = wup.at[d, d * DECAY_LORA:(d + 1) * DECAY_LORA].set(rwkv_w_up[i, d].astype(BF16))
            aup = aup.at[d, d * AAA_LORA:(d + 1) * AAA_LORA].set(rwkv_a_up[i, d].astype(BF16))
        scan = _rwkv_scan(p, rwkv_mu[i], vecs, wup, aup, rwkv_g_up[i].astype(BF16), masks, bsz, n_ctx, n_lat)
        ya = _readout(*scan, rwkv_lnx[i], head_avg, 512)

        dqk, wqk = _rope(p, cs, nt)
        lf = diff_lam[i]
        lam = jnp.exp(jnp.sum(lf[0] * lf[1])) - jnp.exp(jnp.sum(lf[2] * lf[3])) + lam_init
        par = jnp.zeros((8, DIFF_DV), F32).at[0].set(lam).at[1].set(diff_subln[i] * (1 - lam_init))
        yb = _diff_attn(dqk, p, par, bsz, n_ctx, ltot)
        yc = _win_attn(win_sink[i], wqk, p, bsz, n_ctx, n_lat)
        z = _merge(ya, yb, yc, p, w_branch_b, i, 512, 1024)

        lnp = lambda j: jnp.zeros((8, dm), F32).at[0].set(ln_g[i, j]).at[1].set(ln_b[i, j])
        xs, v, route = _mix_out(z, xs, w_out_b, i, table((mod, 2), (mod, 3), (mod, 4)), lnp(0),
                                w_r[i], b_r[i], nt, dn_alpha)

        if last:
            lat = lambda t: t.reshape(bsz, ltot, -1)[:, n_ctx:].reshape(bsz * n_lat, -1)
            yg = _moe(lat(route), lat(v), w1, w3, w2, i)
            tab = table((mod, 5), (mod, 0), (mod, 1))
            (out,) = _moe_out(xs, yg, tab, lnp(1), bsz, nt, n_ctx // SEG, dn_alpha, False)
            return out.reshape(bsz, n_lat, dm)
        yg = _moe(route, v, w1, w3, w2, i)
        tab = table((mod, 5), (mods[i + 1], 0), (mods[i + 1], 1))
        xs, u = _moe_out(xs, yg, tab, lnp(1), bsz, nt, 0, dn_alpha, True)
    return None
```

```python
import functools
import math

import jax
import jax.numpy as jnp
from jax import lax
from jax.experimental import pallas as pl
from jax.experimental.pallas import tpu as pltpu

F32 = jnp.float32
BF16 = jnp.bfloat16

D_MODEL = 2048
GRID_W = 64
RWKV_HEADS = 12
RWKV_HEAD = 64
RWKV_W = RWKV_HEADS * RWKV_HEAD
DECAY_LORA = 64
AAA_LORA = 64
GATE_LORA = 128
RWKV_GN_EPS = 64e-5
DIFF_HEADS = 6
DIFF_DK = 64
DIFF_DV = 2 * DIFF_DK
DIFF_W = DIFF_HEADS * DIFF_DV
DIFF_SCALE = DIFF_DK ** -0.5
WIN_Q_HEADS = 8
WIN_KV_HEADS = 2
WIN_GROUP = WIN_Q_HEADS // WIN_KV_HEADS
WIN_HEAD = 64
WIN_W = WIN_Q_HEADS * WIN_HEAD
WIN_KV_W = WIN_KV_HEADS * WIN_HEAD
WIN_SCALE = WIN_HEAD ** -0.5
WINDOW = 128
QBLK = WINDOW
MIX_W = RWKV_W + DIFF_W + WIN_W
N_BRANCH = 3
ROPE_BASE = 10000.0
ROPE_AX_FREQS = 16
RWKV_IN = 3 * RWKV_W + 2 * DECAY_LORA + 2 * AAA_LORA + GATE_LORA
DIFF_IN = 3 * DIFF_W
WIN_IN = WIN_W + 2 * WIN_KV_W
DIFF_OFF = RWKV_IN
WIN_OFF = DIFF_OFF + DIFF_IN
GATE_OFF = WIN_OFF + WIN_IN
N_IN = GATE_OFF + N_BRANCH * D_MODEL
N_GROUPS = 4
EXPERTS_PER_GROUP = 8
N_EXPERTS = N_GROUPS * EXPERTS_PER_GROUP
EXPERT_TOP_K = 2
D_EXPERT = D_MODEL // 4
MOE_BLK = 256
ADA_EPS = 1e-6
LN_EPS = 1e-5
NEG_INF = -1e30
LOG2E = math.log2(math.e)

LANE = 128
SEG = 256
P_RWKV = 0
P_GATE = 3072
P_DIFF = P_GATE + N_BRANCH * D_MODEL
P_WIN = P_DIFF + DIFF_IN
P_COLS = P_WIN + WIN_IN
CHUNK = 64
VMEM_LIMIT = 56 * 1024 * 1024


def _cparams(sem):
    return pltpu.CompilerParams(dimension_semantics=sem, vmem_limit_bytes=VMEM_LIMIT)


def _dot(a, b):
    return jnp.dot(a.astype(BF16), b.astype(BF16), preferred_element_type=F32)


def _dot_nt(a, b):
    return lax.dot_general(a.astype(BF16), b.astype(BF16), (((1,), (1,)), ((), ())),
                           preferred_element_type=F32)


def _dot_tn(a, b):
    return lax.dot_general(a.astype(BF16), b.astype(BF16), (((0,), (0,)), ((), ())),
                           preferred_element_type=F32)


def _split(x):
    hi = x.astype(BF16)
    lo = (x - hi.astype(F32)).astype(BF16)
    return hi, lo


def _dot3(a, b):
    ah, al = _split(a)
    bh, bl = _split(b)
    d = functools.partial(jnp.dot, preferred_element_type=F32)
    return d(ah, bh) + (d(ah, bl) + d(al, bh))


def _sigmoid(x):
    return 1.0 / (1.0 + jnp.exp(-x))


def _mm_kernel(a_ref, w_ref, o_ref):
    o_ref[...] = _dot(a_ref[...], w_ref[...]).astype(o_ref.dtype)


def _mm(a, w, layer, out_dtype, tm, tn, name):
    m, k = a.shape
    n = w.shape[2]
    return pl.pallas_call(
        _mm_kernel,
        out_shape=jax.ShapeDtypeStruct((m, n), out_dtype),
        grid=(n // tn, m // tm),
        in_specs=[pl.BlockSpec((tm, k), lambda j, i: (i, 0)),
                  pl.BlockSpec((None, k, tn), lambda j, i: (layer, 0, j))],
        out_specs=pl.BlockSpec((tm, tn), lambda j, i: (i, j)),
        compiler_params=_cparams(("arbitrary", "arbitrary")),
        name=name,
    )(a, w)


def _norm_rows(x, eps):
    mu = jnp.mean(x, axis=-1, keepdims=True)
    xc = x - mu
    return xc * lax.rsqrt(jnp.mean(xc * xc, axis=-1, keepdims=True) + eps)


def _route_rows(logits):
    col = lax.broadcasted_iota(jnp.int32, logits.shape, 1).astype(F32)
    big = float(LANE)
    is_g = col < N_GROUPS
    lg = jnp.where(is_g, logits, NEG_INF)
    g_max = jnp.max(lg, axis=-1, keepdims=True)
    g_sum = jnp.sum(jnp.where(is_g, jnp.exp(lg - g_max), 0.0), axis=-1, keepdims=True)
    pg_top = 1.0 / g_sum
    g_idx = jnp.min(jnp.where(is_g & (lg == g_max), col, big), axis=-1, keepdims=True)
    lo = N_GROUPS + EXPERTS_PER_GROUP * g_idx
    sel = (col >= lo) & (col < lo + EXPERTS_PER_GROUP)
    le = jnp.where(sel, logits, NEG_INF)
    m1 = jnp.max(le, axis=-1, keepdims=True)
    den = jnp.sum(jnp.where(sel, jnp.exp(le - m1), 0.0), axis=-1, keepdims=True)
    i1 = jnp.min(jnp.where(sel & (le == m1), col, big), axis=-1, keepdims=True)
    rest = sel & (col != i1)
    le2 = jnp.where(rest, logits, NEG_INF)
    m2 = jnp.max(le2, axis=-1, keepdims=True)
    i2 = jnp.min(jnp.where(rest & (le2 == m2), col, big), axis=-1, keepdims=True)
    p1 = 1.0 / den
    p2 = jnp.exp(m2 - m1) / den
    tot = p1 + p2
    out = jnp.where(col == 0.0, i1 - N_GROUPS, 0.0)
    out = jnp.where(col == 1.0, i2 - N_GROUPS, out)
    out = jnp.where(col == 2.0, pg_top * p1 / tot, out)
    return jnp.where(col == 3.0, pg_top * p2 / tot, out)


def _mod_kernel(x_ref, tab_ref, u_ref):
    u_ref[...] = (_norm_rows(x_ref[...], ADA_EPS) * (1.0 + tab_ref[0, 2:3, :]) + tab_ref[0, 1:2, :]).astype(u_ref.dtype)


def _modulate0(x, tab, nt):
    rows, dm = x.shape
    return pl.pallas_call(
        _mod_kernel,
        out_shape=jax.ShapeDtypeStruct((rows, dm), BF16),
        grid=(rows // SEG,),
        in_specs=[pl.BlockSpec((SEG, dm), lambda i: (i, 0)),
                  pl.BlockSpec((1, 8, dm), lambda i: (2 * (i // nt) + jnp.minimum(i % nt, 1), 0, 0))],
        out_specs=pl.BlockSpec((SEG, dm), lambda i: (i, 0)),
        compiler_params=_cparams(("arbitrary",)),
        name="modulate0",
    )(x, tab)


def _mix_out_kernel(z_ref, x_ref, w_ref, tab_ref, lnp_ref, wr_ref, br_ref, xo_ref, vo_ref, ro_ref, *, alpha):
    m = _dot(z_ref[...], w_ref[...])
    xn = _norm_rows(alpha * x_ref[...] + tab_ref[0, 0:1, :] * m, LN_EPS) * lnp_ref[0:1, :] + lnp_ref[1:2, :]
    xo_ref[...] = xn
    v = _norm_rows(xn, ADA_EPS) * (1.0 + tab_ref[0, 2:3, :]) + tab_ref[0, 1:2, :]
    vo_ref[...] = v.astype(vo_ref.dtype)
    ro_ref[...] = _route_rows(_dot3(v, wr_ref[...]) + br_ref[0:1, :])


def _mix_out(z, x, w_out, layer, tab, lnp, w_r, b_r, nt, alpha):
    rows, dm = x.shape
    row = lambda i: (i, 0)
    const = lambda i: (0, 0)
    return pl.pallas_call(
        functools.partial(_mix_out_kernel, alpha=alpha),
        out_shape=(jax.ShapeDtypeStruct((rows, dm), F32), jax.ShapeDtypeStruct((rows, dm), BF16),
                   jax.ShapeDtypeStruct((rows, LANE), F32)),
        grid=(rows // SEG,),
        in_specs=[pl.BlockSpec((SEG, dm), row), pl.BlockSpec((SEG, dm), row),
                  pl.BlockSpec((None, dm, dm), lambda i: (layer, 0, 0)),
                  pl.BlockSpec((1, 8, dm), lambda i: (2 * (i // nt) + jnp.minimum(i % nt, 1), 0, 0)),
                  pl.BlockSpec((8, dm), const), pl.BlockSpec((dm, LANE), const), pl.BlockSpec((8, LANE), const)],
        out_specs=(pl.BlockSpec((SEG, dm), row), pl.BlockSpec((SEG, dm), row), pl.BlockSpec((SEG, LANE), row)),
        compiler_params=_cparams(("arbitrary",)),
        name="mix_out",
    )(z, x, w_out, tab, lnp, w_r, b_r)


def _moe_out_kernel(x_ref, y0_ref, y1_ref, tab_ref, lnp_ref, xo_ref, *u_ref, alpha):
    y = y0_ref[0].astype(F32) + y1_ref[0].astype(F32)
    xn = _norm_rows(alpha * x_ref[...] + tab_ref[0, 0:1, :] * y, LN_EPS) * lnp_ref[0:1, :] + lnp_ref[1:2, :]
    xo_ref[...] = xn
    if u_ref:
        u_ref[0][...] = (_norm_rows(xn, ADA_EPS) * (1.0 + tab_ref[0, 2:3, :]) + tab_ref[0, 1:2, :]).astype(BF16)


def _moe_out(x, yg, tab, lnp, bsz, nt, skip, alpha, emit_u):
    dm = x.shape[1]
    nk = nt - skip
    rows = bsz * nk * SEG
    xmap = lambda b, t: (b * nt + skip + t, 0)
    omap = lambda b, t: (b * nk + t, 0)
    out_shape = [jax.ShapeDtypeStruct((rows, dm), F32)]
    out_specs = [pl.BlockSpec((SEG, dm), omap)]
    if emit_u:
        out_shape.append(jax.ShapeDtypeStruct((rows, dm), BF16))
        out_specs.append(pl.BlockSpec((SEG, dm), omap))
    return pl.pallas_call(
        functools.partial(_moe_out_kernel, alpha=alpha),
        out_shape=tuple(out_shape),
        grid=(bsz, nk),
        in_specs=[pl.BlockSpec((SEG, dm), xmap),
                  pl.BlockSpec((1, SEG, dm), lambda b, t: (0, b * nk + t, 0)),
                  pl.BlockSpec((1, SEG, dm), lambda b, t: (1, b * nk + t, 0)),
                  pl.BlockSpec((1, 8, dm), lambda b, t: (2 * b + jnp.minimum(skip + t, 1), 0, 0)),
                  pl.BlockSpec((8, dm), lambda b, t: (0, 0))],
        out_specs=tuple(out_specs),
        compiler_params=_cparams(("arbitrary", "arbitrary")),
        name="moe_out",
    )(x, yg, yg, tab, lnp)


def _rope_kernel(pd_ref, pw_ref, cs_ref, dqk_ref, wqk_ref):
    cos = cs_ref[0]
    sin = cs_ref[1]
    first_half = lax.broadcasted_iota(jnp.int32, cos.shape, 1) % (2 * ROPE_AX_FREQS) < ROPE_AX_FREQS

    def rot(x, scale):
        x = x.astype(F32)
        sw = jnp.where(first_half, pltpu.roll(x, LANE - ROPE_AX_FREQS, axis=1), pltpu.roll(x, ROPE_AX_FREQS, axis=1))
        return ((x * cos + sw * sin) * scale).astype(BF16)

    for c in range(2 * DIFF_W // LANE):
        scale = DIFF_SCALE * LOG2E if c < DIFF_W // LANE else 1.0
        dqk_ref[:, c * LANE:(c + 1) * LANE] = rot(pd_ref[:, c * LANE:(c + 1) * LANE], scale)
    for c in range((WIN_W + WIN_KV_W) // LANE):
        scale = WIN_SCALE * LOG2E if c < WIN_W // LANE else 1.0
        wqk_ref[:, c * LANE:(c + 1) * LANE] = rot(pw_ref[:, c * LANE:(c + 1) * LANE], scale)


def _rope(p, cs, nt):
    rows = p.shape[0]
    wd, ww = 2 * DIFF_W, WIN_W + WIN_KV_W
    return pl.pallas_call(
        _rope_kernel,
        out_shape=(jax.ShapeDtypeStruct((rows, wd), BF16), jax.ShapeDtypeStruct((rows, ww), BF16)),
        grid=(rows // SEG,),
        in_specs=[pl.BlockSpec((SEG, wd), lambda i: (i, P_DIFF // wd)),
                  pl.BlockSpec((SEG, ww), lambda i: (i, P_WIN // ww)),
                  pl.BlockSpec((2, SEG, LANE), lambda i: (0, i % nt, 0))],
        out_specs=(pl.BlockSpec((SEG, wd), lambda i: (i, 0)), pl.BlockSpec((SEG, ww), lambda i: (i, 0))),
        compiler_params=_cparams(("arbitrary",)),
        name="rope",
    )(p, p, cs)


def _readout_kernel(y0_ref, y1_ref, bv0_ref, bv1_ref, g_ref, lnx_ref, avg_ref, o_ref):
    avg = avg_ref[...]

    def head_mean(t):
        hi, lo = _split(t)
        return jnp.dot(hi, avg, preferred_element_type=F32) + jnp.dot(lo, avg, preferred_element_type=F32)

    y = y0_ref[...] + y1_ref[...]
    dev = y - head_mean(y)
    yn = dev * lax.rsqrt(head_mean(dev * dev) + RWKV_GN_EPS) * lnx_ref[0:1, :] + lnx_ref[1:2, :]
    bonus = bv0_ref[...].astype(F32) + bv1_ref[...].astype(F32)
    o_ref[...] = ((yn + bonus) * g_ref[...].astype(F32)).astype(o_ref.dtype)


def _readout(y0, y1, bv0, bv1, g, lnx, avg, tm):
    rows = y0.shape[0]
    row = pl.BlockSpec((tm, RWKV_W), lambda i: (i, 0))
    return pl.pallas_call(
        _readout_kernel,
        out_shape=jax.ShapeDtypeStruct((rows, RWKV_W), BF16),
        grid=(rows // tm,),
        in_specs=[row, row, row, row, row,
                  pl.BlockSpec((2, RWKV_W), lambda i: (0, 0)),
                  pl.BlockSpec((RWKV_W, RWKV_W), lambda i: (0, 0))],
        out_specs=row,
        compiler_params=_cparams(("arbitrary",)),
        name="rwkv_readout",
    )(y0, y1, bv0, bv1, g, lnx, avg)


def _rwkv_features(d, chunk, p_ref, hp_ref, hn_ref, mu_ref, vec_ref, wup_ref, aup_ref, gup_ref, msk_ref,
                   bv_ref, g_ref, nc_ctx, nc_lat):
    n = CHUNK
    p = p_ref[...].astype(F32)
    first = jnp.logical_or(chunk == 0, chunk == nc_ctx)
    last = jnp.logical_or(chunk == nc_ctx - 1, chunk == nc_ctx + nc_lat - 1)
    hp = jnp.where(first, 0.0, hp_ref[15:16, :].astype(F32))
    hn = jnp.where(last, 0.0, hn_ref[0:1, :].astype(F32))
    row = lax.broadcasted_iota(jnp.int32, (n, 1), 0)
    prev = jnp.where(row == 0, hp, pltpu.roll(p, 1, axis=0))
    nxt = jnp.where(row == n - 1, hn, pltpu.roll(p, n - 1, axis=0))
    ps = p + mu_ref[0:1, :] * (prev - p) + mu_ref[1:2, :] * (nxt - p)

    r = ps[:, 0:RWKV_W]
    k = ps[:, RWKV_W:2 * RWKV_W]
    v = ps[:, 2 * RWKV_W:3 * RWKV_W]
    o = 3 * RWKV_W
    wd = jnp.tanh(ps[:, o:o + 2 * DECAY_LORA])
    ad = ps[:, o + 2 * DECAY_LORA:o + 2 * DECAY_LORA + 2 * AAA_LORA]
    gd = ps[:, o + 2 * DECAY_LORA + 2 * AAA_LORA:]
    w0 = vec_ref[d, 0:1, :]
    a0 = vec_ref[d, 1:2, :]
    k_k = vec_ref[d, 2:3, :]
    k_a = vec_ref[d, 3:4, :]
    r_k = vec_ref[d, 4:5, :]
    w_log = w0 + _dot(wd, wup_ref[d])
    a = _sigmoid(a0 + _dot(ad, aup_ref[d]))
    if g_ref is not None:
        g_ref[...] = _dot(_sigmoid(gd), gup_ref[...]).astype(g_ref.dtype)
    logw = -math.exp(-0.5) * _sigmoid(w_log)

    incl_f = msk_ref[d, 1]
    lw_hi, lw_lo = _split(logw)
    incl_b = incl_f.astype(BF16)
    cl = (jnp.dot(incl_b, lw_hi, preferred_element_type=F32)
          + jnp.dot(incl_b, lw_lo, preferred_element_type=F32))
    tot = jnp.sum(logw, axis=0, keepdims=True)
    e_in = jnp.exp(cl)
    e_ex = jnp.exp(cl - logw)
    e_inv = jnp.exp(-cl)
    e_end = jnp.exp(tot - cl)
    p_all = jnp.exp(tot)

    sls = [slice(h * RWKV_HEAD, (h + 1) * RWKV_HEAD) for h in range(RWKV_HEADS)]
    kk_n = k * k_k
    kk_sq = kk_n * kk_n
    kd_all = k * (1.0 + (a - 1.0) * k_a)
    rkd = r * kd_all * r_k
    inv_norm = [1.0 / jnp.maximum(jnp.sqrt(jnp.sum(kk_sq[:, sl], axis=-1, keepdims=True)), 1e-12) for sl in sls]
    bonus = [jnp.sum(rkd[:, sl], axis=-1, keepdims=True) for sl in sls]
    bv_ref[...] = jnp.concatenate([bonus[h] * v[:, sl] for h, sl in enumerate(sls)], axis=1).astype(bv_ref.dtype)
    kk_all = jnp.concatenate([kk_n[:, sl] * inv_norm[h] for h, sl in enumerate(sls)], axis=1)
    bd_all = kk_all * a
    return dict(
        v=v, strict_f=msk_ref[d, 0], incl_f=incl_f, p_all=p_all,
        kk_t=kk_all * e_ex,
        r_t=r * e_in,
        b_i=bd_all * e_inv, k_i=kd_all * e_inv,
        k_e=kd_all * e_end,
        b_e=bd_all * e_end)


def _rwkv_kernel(p0_ref, hp0_ref, hn0_ref, p1_ref, hp1_ref, hn1_ref, mu_ref, vec_ref, wup_ref, aup_ref,
                 gup_ref, msk_ref, y0_ref, y1_ref, bv0_ref, bv1_ref, g_ref, state, *, nc_ctx, nc_lat):
    i = pl.program_id(1)

    @pl.when(i == 0)
    def _():
        state[...] = jnp.zeros_like(state)

    n = CHUNK
    shared = (mu_ref, vec_ref, wup_ref, aup_ref, gup_ref, msk_ref)
    f = [_rwkv_features(0, i, p0_ref, hp0_ref, hn0_ref, *shared, bv0_ref, g_ref, nc_ctx, nc_lat),
         _rwkv_features(1, _rwkv_mirror(i, nc_ctx, nc_lat), p1_ref, hp1_ref, hn1_ref, *shared, bv1_ref, None,
                        nc_ctx, nc_lat)]

    hw = 2 * RWKV_HEAD
    lane_a = lax.broadcasted_iota(jnp.int32, (n, hw), 1) < RWKV_HEAD
    sel_a = lambda x: jnp.where(lane_a, x, 0.0)
    sel_b = lambda x: jnp.where(lane_a, 0.0, x)
    pick = lambda xa, xb: jnp.where(lane_a, xa, xb)
    bdiag = lambda x: jnp.concatenate([sel_a(x), sel_b(x)], axis=0)
    same_head = ((lax.broadcasted_iota(jnp.int32, (hw, hw), 0) < RWKV_HEAD)
                 == (lax.broadcasted_iota(jnp.int32, (hw, hw), 1) < RWKV_HEAD))
    eye_f = (lax.broadcasted_iota(jnp.int32, (n, n), 0) == lax.broadcasted_iota(jnp.int32, (n, n), 1)).astype(F32)
    eye2 = jnp.concatenate([eye_f, eye_f], axis=1)
    strict2 = [jnp.concatenate([f[d]["strict_f"]] * 2, axis=1) > 0.5 for d in range(2)]
    incl2 = [jnp.concatenate([f[d]["incl_f"]] * 2, axis=1) > 0.5 for d in range(2)]

    items = [(d, q) for d in range(2) for q in range(RWKV_HEADS // 2)]
    idx = range(len(items))
    get = lambda name, j: f[items[j][0]][name][:, items[j][1] * hw:(items[j][1] + 1) * hw]
    strict = [strict2[d] for d, _ in items]
    incl = [incl2[d] for d, _ in items]

    v_p = [get("v", j) for j in idx]
    kk_t = [get("kk_t", j) for j in idx]
    r_t = [get("r_t", j) for j in idx]
    b_i = [get("b_i", j) for j in idx]
    k_i = [get("k_i", j) for j in idx]
    lhs = [jnp.concatenate([kk_t[j], r_t[j]], axis=0) for j in idx]
    gram_a = [_dot_nt(lhs[j], jnp.concatenate([sel_a(b_i[j]), sel_a(k_i[j])], axis=0)) for j in idx]
    gram_b = [_dot_nt(lhs[j], jnp.concatenate([sel_b(k_i[j]), sel_b(b_i[j])], axis=0)) for j in idx]
    pw = [jnp.where(strict[j], -pick(gram_a[j][:n], gram_b[j][:n]), 0.0) for j in idx]
    l_k = [jnp.where(strict[j], pick(gram_b[j][:n], gram_a[j][:n]), 0.0) for j in idx]
    m_rb = [jnp.where(incl[j], pick(gram_a[j][n:], gram_b[j][n:]), 0.0) for j in idx]
    m_rk = [jnp.where(incl[j], pick(gram_b[j][n:], gram_a[j][n:]), 0.0) for j in idx]
    v_x = [jnp.concatenate([sel_b(v_p[j]), sel_a(v_p[j])], axis=0) for j in idx]
    lkv = [_dot(l_k[j], v_x[j]) for j in idx]
    t_inv = [eye2 + pw[j] for j in idx]
    pw = [_dot(pw[j], bdiag(pw[j])) for j in idx]
    for _ in range(4):
        both = [_dot(jnp.concatenate([t_inv[j], pw[j]], axis=0), bdiag(pw[j])) for j in idx]
        t_inv = [t_inv[j] + both[j][:n] for j in idx]
        pw = [both[j][n:] for j in idx]
    t_inv = [t_inv[j] + _dot(t_inv[j], bdiag(pw[j])) for j in idx]
    tx = [_dot(t_inv[j], jnp.concatenate([bdiag(kk_t[j]), bdiag(lkv[j])], axis=1)) for j in idx]
    w_p = [tx[j][:, :hw] for j in idx]
    u_p = [tx[j][:, hw:] for j in idx]
    mw = [_dot(m_rb[j], jnp.concatenate([bdiag(w_p[j]), bdiag(u_p[j])], axis=1)) for j in idx]
    mv = [_dot(m_rk[j], v_x[j]) for j in idx]
    r_eff = [r_t[j] - mw[j][:, :hw] for j in idx]
    y_loc = [mv[j] - mw[j][:, hw:] for j in idx]
    b_e = [-get("b_e", j) for j in idx]
    g_a = [_dot_tn(w_p[j], b_e[j]) for j in idx]
    g_d = [_dot_tn(jnp.concatenate([v_p[j], u_p[j]], axis=0),
                   jnp.concatenate([get("k_e", j), b_e[j]], axis=0)) for j in idx]
    s0 = [state[d, q] for d, q in items]
    ys = [_dot_nt(r_eff[j], bdiag(s0[j])) + y_loc[j] for j in idx]
    half = len(items) // 2
    y0_ref[...] = jnp.concatenate(ys[:half], axis=1)
    y1_ref[...] = jnp.concatenate(ys[half:], axis=1)
    s1 = [_dot(s0[j], jnp.where(same_head, g_a[j], 0.0)) for j in idx]
    for j, (d, q) in enumerate(items):
        state[d, q] = s0[j] * get("p_all", j) + s1[j] + pick(g_d[j][:n], g_d[j][n:])


def _rwkv_mirror(i, nc_ctx, nc_lat):
    return jnp.where(i < nc_ctx, nc_ctx - 1 - i, 2 * nc_ctx + nc_lat - 1 - i)


def _rwkv_scan(p, mu, vecs, wup, aup, gup, masks, bsz, n_ctx, n_lat):
    ltot = n_ctx + n_lat
    nc_ctx, nc_lat = n_ctx // CHUNK, n_lat // CHUNK
    nc = nc_ctx + nc_lat
    rows = bsz * ltot
    hb = CHUNK // 16
    n_hblk = rows // 16
    chunk_of = (lambda i: i, lambda i: _rwkv_mirror(i, nc_ctx, nc_lat))

    def specs(d):
        main = lambda b, i: (b * nc + chunk_of[d](i), 0)
        prev = lambda b, i: (jnp.maximum((b * nc + chunk_of[d](i)) * hb - 1, 0), 0)
        nxt = lambda b, i: (jnp.minimum((b * nc + chunk_of[d](i) + 1) * hb, n_hblk - 1), 0)
        return main, [pl.BlockSpec((CHUNK, RWKV_IN), main), pl.BlockSpec((16, RWKV_IN), prev),
                      pl.BlockSpec((16, RWKV_IN), nxt)]

    (main0, in0), (main1, in1) = specs(0), specs(1)
    whole = lambda shape: pl.BlockSpec(shape, lambda b, i: (0,) * len(shape))
    out_f = jax.ShapeDtypeStruct((rows, RWKV_W), F32)
    out_bf = jax.ShapeDtypeStruct((rows, RWKV_W), BF16)
    ospec = lambda m: pl.BlockSpec((CHUNK, RWKV_W), m)
    kern = functools.partial(_rwkv_kernel, nc_ctx=nc_ctx, nc_lat=nc_lat)
    return pl.pallas_call(
        kern,
        out_shape=(out_f, out_f, out_bf, out_bf, out_bf),
        grid=(bsz, nc),
        in_specs=in0 + in1 + [whole((2, RWKV_IN)), whole((2, 8, RWKV_W)),
                              whole((2, 2 * DECAY_LORA, RWKV_W)), whole((2, 2 * AAA_LORA, RWKV_W)),
                              whole((GATE_LORA, RWKV_W)), whole((2, 2, CHUNK, CHUNK))],
        out_specs=(ospec(main0), ospec(main1), ospec(main0), ospec(main1), ospec(main0)),
        scratch_shapes=[pltpu.VMEM((2, RWKV_HEADS // 2, RWKV_HEAD, 2 * RWKV_HEAD), F32)],
        compiler_params=_cparams(("arbitrary", "arbitrary")),
        name="rwkv_scan",
    )(p, p, p, p, p, p, mu, vecs, wup, aup, gup, masks)


def _diff_kernel(q_ref, k_ref, v_ref, par_ref, o_ref, *, n_ctx):
    t = pl.program_id(2)
    lam = par_ref[0:1, :]
    gain = par_ref[1:2, :]

    def attend(nk):
        q = q_ref[...]
        k = k_ref[0:nk, :]
        v = v_ref[0:nk, :]

        def softmax_v(sl):
            s = _dot_nt(q[:, sl], k[:, sl])
            e = jnp.exp2(s - jnp.max(s, axis=-1, keepdims=True))
            return _dot(e, v) / jnp.sum(e, axis=-1, keepdims=True)

        o = softmax_v(slice(0, DIFF_DK)) - lam * softmax_v(slice(DIFF_DK, 2 * DIFF_DK))
        o = o * lax.rsqrt(jnp.mean(o * o, axis=-1, keepdims=True) + 1e-5)
        o_ref[...] = (o * gain).astype(o_ref.dtype)

    @pl.when(t * SEG < n_ctx)
    def _():
        attend(n_ctx)

    @pl.when(t * SEG >= n_ctx)
    def _():
        attend(k_ref.shape[0])


def _diff_attn(qk, p, par, bsz, n_ctx, ltot):
    rows = bsz * ltot
    nt = ltot // SEG
    voff = (P_DIFF + 2 * DIFF_W) // DIFF_DV
    return pl.pallas_call(
        functools.partial(_diff_kernel, n_ctx=n_ctx),
        out_shape=jax.ShapeDtypeStruct((rows, DIFF_W), BF16),
        grid=(bsz, DIFF_HEADS, nt),
        in_specs=[pl.BlockSpec((SEG, DIFF_DV), lambda b, h, t: (b * nt + t, h)),
                  pl.BlockSpec((ltot, DIFF_DV), lambda b, h, t: (b, DIFF_HEADS + h)),
                  pl.BlockSpec((ltot, DIFF_DV), lambda b, h, t: (b, voff + h)),
                  pl.BlockSpec((8, DIFF_DV), lambda b, h, t: (0, 0))],
        out_specs=pl.BlockSpec((SEG, DIFF_DV), lambda b, h, t: (b * nt + t, h)),
        compiler_params=_cparams(("arbitrary", "arbitrary", "arbitrary")),
        name="diff_attn",
    )(qk, qk, p, par)


def _win_kernel(sink_ref, q_ref, kp_ref, kc_ref, kn_ref, kx_ref, vp_ref, vc_ref, vn_ref, vx_ref,
                o_ref, *, n_ctx, n_lat):
    blk = pl.program_id(1)
    ncb = n_ctx // QBLK
    nq = WIN_GROUP * QBLK

    def run(keys, vals, mask):
        outs = []
        for g in range(WIN_KV_HEADS):
            ksl = slice(g * WIN_HEAD, (g + 1) * WIN_HEAD)
            qg = jnp.concatenate(
                [q_ref[:, (g * WIN_GROUP + j) * WIN_HEAD:(g * WIN_GROUP + j + 1) * WIN_HEAD]
                 for j in range(WIN_GROUP)], axis=0)
            s = _dot_nt(qg, keys[:, ksl])
            if mask is not None:
                s = jnp.where(mask, s, NEG_INF)
            hrow = lax.broadcasted_iota(jnp.int32, (nq, 1), 0) // QBLK
            sink = jnp.zeros((nq, 1), F32)
            for j in range(WIN_GROUP):
                sink = jnp.where(hrow == j, sink_ref[g * WIN_GROUP + j] * LOG2E, sink)
            m = jnp.maximum(jnp.max(s, axis=-1, keepdims=True), sink)
            e = jnp.exp2(s - m)
            den = jnp.sum(e, axis=-1, keepdims=True) + jnp.exp2(sink - m)
            og = _dot(e, vals[:, ksl]) / den
            outs.extend(og[j * QBLK:(j + 1) * QBLK] for j in range(WIN_GROUP))
        o_ref[...] = jnp.concatenate(outs, axis=1).astype(o_ref.dtype)

    @pl.when(blk < ncb)
    def _():
        run(kx_ref[...], vx_ref[...], None)

    @pl.when(blk >= ncb)
    def _():
        j = blk - ncb
        keys = jnp.concatenate([kp_ref[...], kc_ref[...], kn_ref[...], kx_ref[...]], axis=0)
        vals = jnp.concatenate([vp_ref[...], vc_ref[...], vn_ref[...], vx_ref[...]], axis=0)
        nk = 3 * QBLK + n_ctx
        qpos = j * QBLK + lax.broadcasted_iota(jnp.int32, (nq, nk), 0) % QBLK
        col = lax.broadcasted_iota(jnp.int32, (nq, nk), 1)
        kpos = (j - 1) * QBLK + col
        band = (jnp.abs(qpos - kpos) <= WINDOW) & (kpos >= 0) & (kpos < n_lat)
        run(keys, vals, band | (col >= 3 * QBLK))


def _win_attn(sink, qk, p, bsz, n_ctx, n_lat):
    ltot = n_ctx + n_lat
    rows = bsz * ltot
    nb = ltot // QBLK
    ncb = n_ctx // QBLK
    koff = WIN_W // WIN_KV_W
    voff = (P_WIN + WIN_W + WIN_KV_W) // WIN_KV_W

    def lat_blk(b, t, shift):
        j = jnp.clip(t - ncb + shift, 0, nb - ncb - 1)
        return b * nb + ncb + j

    kspec = lambda shift, c: pl.BlockSpec((QBLK, WIN_KV_W), lambda b, t, s: (lat_blk(b, t, shift), c))
    xspec = lambda c: pl.BlockSpec((n_ctx, WIN_KV_W), lambda b, t, s: (b * (ltot // n_ctx), c))
    return pl.pallas_call(
        functools.partial(_win_kernel, n_ctx=n_ctx, n_lat=n_lat),
        out_shape=jax.ShapeDtypeStruct((rows, WIN_W), BF16),
        grid_spec=pltpu.PrefetchScalarGridSpec(
            num_scalar_prefetch=1,
            grid=(bsz, nb),
            in_specs=[pl.BlockSpec((QBLK, WIN_W), lambda b, t, s: (b * nb + t, 0)),
                      kspec(-1, koff), kspec(0, koff), kspec(1, koff), xspec(koff),
                      kspec(-1, voff), kspec(0, voff), kspec(1, voff), xspec(voff)],
            out_specs=pl.BlockSpec((QBLK, WIN_W), lambda b, t, s: (b * nb + t, 0))),
        compiler_params=_cparams(("arbitrary", "arbitrary")),
        name="win_attn",
    )(sink, qk, qk, qk, qk, qk, p, p, p, p)


def _merge_kernel(ya_ref, yb_ref, yc_ref, ga_ref, gb_ref, gc_ref, wa_ref, wb_ref, wc_ref, o_ref):
    z = _sigmoid(ga_ref[...].astype(F32)) * _dot(ya_ref[...], wa_ref[...])
    z = z + _sigmoid(gb_ref[...].astype(F32)) * _dot(yb_ref[...], wb_ref[...])
    z = z + _sigmoid(gc_ref[...].astype(F32)) * _dot(yc_ref[...], wc_ref[...])
    o_ref[...] = z.astype(o_ref.dtype)


def _merge(ya, yb, yc, p, w_branch, layer, tm, tn):
    m = ya.shape[0]
    assert RWKV_W == DIFF_W and (RWKV_W + DIFF_W) % WIN_W == 0
    nj = D_MODEL // tn
    goff = P_GATE // tn
    gspec = lambda br: pl.BlockSpec((tm, tn), lambda j, i: (i, goff + br * nj + j))
    return pl.pallas_call(
        _merge_kernel,
        out_shape=jax.ShapeDtypeStruct((m, D_MODEL), BF16),
        grid=(nj, m // tm),
        in_specs=[pl.BlockSpec((tm, RWKV_W), lambda j, i: (i, 0)),
                  pl.BlockSpec((tm, DIFF_W), lambda j, i: (i, 0)),
                  pl.BlockSpec((tm, WIN_W), lambda j, i: (i, 0)),
                  gspec(0), gspec(1), gspec(2),
                  pl.BlockSpec((None, RWKV_W, tn), lambda j, i: (layer, 0, j)),
                  pl.BlockSpec((None, DIFF_W, tn), lambda j, i: (layer, 1, j)),
                  pl.BlockSpec((None, WIN_W, tn), lambda j, i: (layer, (RWKV_W + DIFF_W) // WIN_W, j))],
        out_specs=pl.BlockSpec((tm, tn), lambda j, i: (i, j)),
        compiler_params=_cparams(("arbitrary", "arbitrary")),
        name="merge",
    )(ya, yb, yc, p, p, p, w_branch, w_branch, w_branch)


def _expert_kernel(be_ref, nu_ref, x_ref, gw_ref, w1_ref, w3_ref, w2_ref, o_ref, w1b, w3b, w2b):
    i = pl.program_id(0)
    prev = be_ref[jnp.maximum(i - 1, 0)]

    @pl.when(jnp.logical_or(i == 0, be_ref[i] != prev))
    def _():
        w1b[...] = w1_ref[...].astype(BF16)
        w3b[...] = w3_ref[...].astype(BF16)
        w2b[...] = w2_ref[...].astype(BF16)

    @pl.when(i < nu_ref[0])
    def _():
        x = x_ref[...]
        h1 = _dot(x, w1b[...])
        h = h1 * _sigmoid(h1) * _dot(x, w3b[...])
        o_ref[...] = (_dot(h, w2b[...]) * gw_ref[...]).astype(o_ref.dtype)

    @pl.when(i >= nu_ref[0])
    def _():
        o_ref[...] = jnp.zeros_like(o_ref)


def _experts(blk_expert, n_used, xb, gw, w1, w3, w2, layer):
    rows = xb.shape[0]
    nb = rows // MOE_BLK
    return pl.pallas_call(
        _expert_kernel,
        out_shape=jax.ShapeDtypeStruct((rows, D_MODEL), BF16),
        grid_spec=pltpu.PrefetchScalarGridSpec(
            num_scalar_prefetch=2,
            grid=(nb,),
            in_specs=[pl.BlockSpec((MOE_BLK, D_MODEL), lambda i, be, nu: (i, 0)),
                      pl.BlockSpec((MOE_BLK, 1), lambda i, be, nu: (i, 0)),
                      pl.BlockSpec((None, None, D_MODEL, D_EXPERT), lambda i, be, nu: (layer, be[i], 0, 0)),
                      pl.BlockSpec((None, None, D_MODEL, D_EXPERT), lambda i, be, nu: (layer, be[i], 0, 0)),
                      pl.BlockSpec((None, None, D_EXPERT, D_MODEL), lambda i, be, nu: (layer, be[i], 0, 0))],
            out_specs=pl.BlockSpec((MOE_BLK, D_MODEL), lambda i, be, nu: (i, 0)),
            scratch_shapes=[pltpu.VMEM((D_MODEL, D_EXPERT), BF16),
                            pltpu.VMEM((D_MODEL, D_EXPERT), BF16),
                            pltpu.VMEM((D_EXPERT, D_MODEL), BF16)]),
        compiler_params=_cparams(("arbitrary",)),
        name="experts",
    )(blk_expert, n_used, xb, gw, w1, w3, w2)


def _rope_tables(n_ctx, n_lat):
    rows = n_lat // GRID_W
    row = jnp.repeat(jnp.arange(rows), GRID_W).astype(F32)
    col = (jnp.arange(rows * GRID_W) % GRID_W).astype(F32)
    inv = ROPE_BASE ** (-jnp.arange(ROPE_AX_FREQS, dtype=F32) / ROPE_AX_FREQS)
    ang = jnp.stack([row[:, None] * inv, col[:, None] * inv], axis=1)
    cos, sin = jnp.cos(ang), jnp.sin(ang)
    cos4 = jnp.stack([cos, cos], axis=2).reshape(n_lat, 4 * ROPE_AX_FREQS)
    sin4 = jnp.stack([-sin, sin], axis=2).reshape(n_lat, 4 * ROPE_AX_FREQS)
    cos4 = jnp.concatenate([jnp.ones((n_ctx, 64), F32), cos4], axis=0)
    sin4 = jnp.concatenate([jnp.zeros((n_ctx, 64), F32), sin4], axis=0)
    return jnp.stack([jnp.tile(cos4, (1, LANE // 64)), jnp.tile(sin4, (1, LANE // 64))])


def _moe(route, v, w1, w3, w2, layer):
    t = v.shape[0]
    experts = route[:, :EXPERT_TOP_K].astype(jnp.int32)
    gates = route[:, EXPERT_TOP_K:2 * EXPERT_TOP_K]
    k = EXPERT_TOP_K
    a = t * k
    e_n = N_EXPERTS
    nb = -(-a // MOE_BLK) + e_n
    e_flat = experts.T.reshape(a)
    iota = jnp.arange(a, dtype=jnp.int32)
    e_s, order, g_s = lax.sort((e_flat, iota, gates.T.reshape(a)), num_keys=1, is_stable=True)
    ids = jnp.arange(e_n, dtype=jnp.int32)
    start = jnp.sum(e_s[None, :] < ids[:, None], axis=1, dtype=jnp.int32)
    counts = jnp.sum(e_s[None, :] == ids[:, None], axis=1, dtype=jnp.int32)
    padded = (counts + MOE_BLK - 1) // MOE_BLK * MOE_BLK
    pad_end = jnp.cumsum(padded)
    pad_start = pad_end - padded
    shift = jnp.sum(jnp.where(e_s[:, None] == ids[None, :], (pad_start - start)[None, :], 0), axis=1)
    pos = lax.sort((order, iota + shift), num_keys=1)[1]
    blk_first = jnp.arange(nb, dtype=jnp.int32) * MOE_BLK
    blk_expert = jnp.minimum(jnp.sum(pad_end[None, :] <= blk_first[:, None], axis=1, dtype=jnp.int32), e_n - 1)
    n_used = (pad_end[-1:] // MOE_BLK).astype(jnp.int32)
    rank = blk_first[:, None] + jnp.arange(MOE_BLK, dtype=jnp.int32)[None, :] - pad_start[blk_expert][:, None]
    filled = (rank < counts[blk_expert][:, None]).reshape(nb * MOE_BLK)
    sidx = jnp.clip(start[blk_expert][:, None] + rank, 0, a - 1).reshape(nb * MOE_BLK)
    spread = jnp.arange(nb * MOE_BLK, dtype=jnp.int32) % t
    slot_tok = jnp.where(filled, order.at[sidx].get(mode="promise_in_bounds") % t, spread)
    slot_gate = jnp.where(filled, g_s.at[sidx].get(mode="promise_in_bounds"), 0.0)
    xb = v.at[slot_tok].get(mode="promise_in_bounds")
    ys = _experts(blk_expert, n_used, xb, slot_gate[:, None], w1, w3, w2, layer)
    yg = lax.optimization_barrier(ys.at[pos].get(mode="promise_in_bounds"))
    return yg.reshape(k, t, D_MODEL)


def kernel(x, c, ctx, c_ctx, w_mod, b_mod, w_in, rwkv_mu, rwkv_w0, rwkv_w_up, rwkv_a0, rwkv_a_up,
           rwkv_g_up, rwkv_kvec, rwkv_lnx, diff_lam, diff_subln, win_sink, w_branch, w_out, ln_g, ln_b,
           w_rg, b_rg, w_re, b_re, w1, w3, w2):
    bsz, n_lat, dm = x.shape
    n_ctx = ctx.shape[1]
    depth = w_mod.shape[0]
    ltot = n_ctx + n_lat
    rows = bsz * ltot
    dn_alpha = (2 * depth) ** 0.25
    assert dm == D_MODEL and n_ctx % SEG == 0 and n_lat % SEG == 0 and ltot % n_ctx == 0

    nt = ltot // SEG
    cs = _rope_tables(n_ctx, n_lat)
    fwd = jnp.tril(jnp.ones((CHUNK, CHUNK), F32))
    masks = jnp.stack([jnp.stack([fwd - jnp.eye(CHUNK, dtype=F32), fwd]),
                       jnp.stack([fwd.T - jnp.eye(CHUNK, dtype=F32), fwd.T])])
    head_id = jnp.arange(RWKV_W) // RWKV_HEAD
    head_avg = ((head_id[:, None] == head_id[None, :]).astype(F32) / RWKV_HEAD).astype(BF16)

    xs = jnp.concatenate([ctx, x], axis=1).reshape(rows, dm)
    cvec = jnp.concatenate([c_ctx[None, :], c], axis=0)
    cpad = jnp.zeros((32, dm), F32).at[:bsz + 1].set(jax.nn.silu(cvec))
    w_in_p = jnp.concatenate([w_in[:, :, :RWKV_IN], jnp.zeros((depth, dm, P_GATE - RWKV_IN), F32),
                              w_in[:, :, GATE_OFF:], w_in[:, :, DIFF_OFF:GATE_OFF]], axis=2).astype(BF16)
    w_branch_b = w_branch.astype(BF16)
    w_out_b = w_out.astype(BF16)
    w_r = jnp.zeros((depth, dm, LANE), F32).at[:, :, :N_GROUPS].set(w_rg)
    w_r = w_r.at[:, :, N_GROUPS:N_GROUPS + N_EXPERTS].set(w_re)
    b_r = jnp.zeros((depth, 8, LANE), F32).at[:, :, :N_GROUPS].set(b_rg[:, None, :])
    b_r = b_r.at[:, :, N_GROUPS:N_GROUPS + N_EXPERTS].set(b_re[:, None, :])

    mods = [_mm(cpad, w_mod, i, F32, 32, 1024, "mod")[:bsz + 1] + b_mod[i] for i in range(depth)]

    def table(gate, shift, scale):
        def both(m, j):
            v = m[:, j * dm:(j + 1) * dm]
            return jnp.stack([jnp.broadcast_to(v[0], (bsz, dm)), v[1:]], axis=1)
        t = jnp.stack([both(*gate), both(*shift), both(*scale)], axis=2)
        return jnp.pad(t, ((0, 0), (0, 0), (0, 5), (0, 0))).reshape(2 * bsz, 8, dm)

    u = _modulate0(xs, table((mods[0], 0), (mods[0], 0), (mods[0], 1)), nt)
    for i in range(depth):
        last = i == depth - 1
        lam_init = 0.8 - 0.6 * math.exp(-0.3 * i)
        mod = mods[i]
        p = _mm(u, w_in_p, i, BF16, 512, 2048, "in_proj")

        vecs = jnp.zeros((2, 8, RWKV_W), F32)
        vecs = vecs.at[:, 0].set(rwkv_w0[i]).at[:, 1].set(rwkv_a0[i])
        vecs = vecs.at[:, 2:5].set(jnp.broadcast_to(rwkv_kvec[i][None], (2, 3, RWKV_W)))
        wup = jnp.zeros((2, 2 * DECAY_LORA, RWKV_W), BF16)
        aup = jnp.zeros((2, 2 * AAA_LORA, RWKV_W), BF16)
        for d in range(2):
            wup = wup.at[d, d * DECAY_LORA:(d + 1) * DECAY_LORA].set(rwkv_w_up[i, d].astype(BF16))
            aup = aup.at[d, d * AAA_LORA:(d + 1) * AAA_LORA].set(rwkv_a_up[i, d].astype(BF16))
        scan = _rwkv_scan(p, rwkv_mu[i], vecs, wup, aup, rwkv_g_up[i].astype(BF16), masks, bsz, n_ctx, n_lat)
        ya = _readout(*scan, rwkv_lnx[i], head_avg, 512)

        dqk, wqk = _rope(p, cs, nt)
        lf = diff_lam[i]
        lam = jnp.exp(jnp.sum(lf[0] * lf[1])) - jnp.exp(jnp.sum(lf[2] * lf[3])) + lam_init
        par = jnp.zeros((8, DIFF_DV), F32).at[0].set(lam).at[1].set(diff_subln[i] * (1 - lam_init))
        yb = _diff_attn(dqk, p, par, bsz, n_ctx, ltot)
        yc = _win_attn(win_sink[i], wqk, p, bsz, n_ctx, n_lat)
        z = _merge(ya, yb, yc, p, w_branch_b, i, 512, 1024)

        lnp = lambda j: jnp.zeros((8, dm), F32).at[0].set(ln_g[i, j]).at[1].set(ln_b[i, j])
        xs, v, route = _mix_out(z, xs, w_out_b, i, table((mod, 2), (mod, 3), (mod, 4)), lnp(0),
                                w_r[i], b_r[i], nt, dn_alpha)

        if last:
            lat = lambda t: t.reshape(bsz, ltot, -1)[:, n_ctx:].reshape(bsz * n_lat, -1)
            yg = _moe(lat(route), lat(v), w1, w3, w2, i)
            tab = table((mod, 5), (mod, 0), (mod, 1))
            (out,) = _moe_out(xs, yg, tab, lnp(1), bsz, nt, n_ctx // SEG, dn_alpha, False)
            return out.reshape(bsz, n_lat, dm)
        yg = _moe(route, v, w1, w3, w2, i)
        tab = table((mod, 5), (mods[i + 1], 0), (mods[i + 1], 1))
        xs, u = _moe_out(xs, yg, tab, lnp(1), bsz, nt, 0, dn_alpha, True)
    return None
```

```python
import functools
import math

import jax
import jax.numpy as jnp
from jax import lax
from jax.experimental import pallas as pl
from jax.experimental.pallas import tpu as pltpu

F32 = jnp.float32
BF16 = jnp.bfloat16

D_MODEL = 2048
GRID_W = 64
RWKV_HEADS = 12
RWKV_HEAD = 64
RWKV_W = RWKV_HEADS * RWKV_HEAD
DECAY_LORA = 64
AAA_LORA = 64
GATE_LORA = 128
RWKV_GN_EPS = 64e-5
DIFF_HEADS = 6
DIFF_DK = 64
DIFF_DV = 2 * DIFF_DK
DIFF_W = DIFF_HEADS * DIFF_DV
DIFF_SCALE = DIFF_DK ** -0.5
WIN_Q_HEADS = 8
WIN_KV_HEADS = 2
WIN_GROUP = WIN_Q_HEADS // WIN_KV_HEADS
WIN_HEAD = 64
WIN_W = WIN_Q_HEADS * WIN_HEAD
WIN_KV_W = WIN_KV_HEADS * WIN_HEAD
WIN_SCALE = WIN_HEAD ** -0.5
WINDOW = 128
QBLK = WINDOW
MIX_W = RWKV_W + DIFF_W + WIN_W
N_BRANCH = 3
ROPE_BASE = 10000.0
ROPE_AX_FREQS = 16
RWKV_IN = 3 * RWKV_W + 2 * DECAY_LORA + 2 * AAA_LORA + GATE_LORA
DIFF_IN = 3 * DIFF_W
WIN_IN = WIN_W + 2 * WIN_KV_W
DIFF_OFF = RWKV_IN
WIN_OFF = DIFF_OFF + DIFF_IN
GATE_OFF = WIN_OFF + WIN_IN
N_IN = GATE_OFF + N_BRANCH * D_MODEL
N_GROUPS = 4
EXPERTS_PER_GROUP = 8
N_EXPERTS = N_GROUPS * EXPERTS_PER_GROUP
EXPERT_TOP_K = 2
D_EXPERT = D_MODEL // 4
MOE_BLK = 256
ADA_EPS = 1e-6
LN_EPS = 1e-5
NEG_INF = -1e30
LOG2E = math.log2(math.e)

LANE = 128
SEG = 256
P_RWKV = 0
P_GATE = 3072
P_DIFF = P_GATE + N_BRANCH * D_MODEL
P_WIN = P_DIFF + DIFF_IN
P_COLS = P_WIN + WIN_IN
CHUNK = 64
VMEM_LIMIT = 56 * 1024 * 1024


def _cparams(sem):
    return pltpu.CompilerParams(dimension_semantics=sem, vmem_limit_bytes=VMEM_LIMIT)


def _dot(a, b):
    return jnp.dot(a.astype(BF16), b.astype(BF16), preferred_element_type=F32)


def _dot_nt(a, b):
    return lax.dot_general(a.astype(BF16), b.astype(BF16), (((1,), (1,)), ((), ())),
                           preferred_element_type=F32)


def _dot_tn(a, b):
    return lax.dot_general(a.astype(BF16), b.astype(BF16), (((0,), (0,)), ((), ())),
                           preferred_element_type=F32)


def _split(x):
    hi = x.astype(BF16)
    lo = (x - hi.astype(F32)).astype(BF16)
    return hi, lo


def _dot3(a, b):
    ah, al = _split(a)
    bh, bl = _split(b)
    d = functools.partial(jnp.dot, preferred_element_type=F32)
    return d(ah, bh) + (d(ah, bl) + d(al, bh))


def _sigmoid(x):
    return 1.0 / (1.0 + jnp.exp(-x))


def _mm_kernel(a_ref, w_ref, o_ref):
    o_ref[...] = _dot(a_ref[...], w_ref[...]).astype(o_ref.dtype)


def _mm(a, w, layer, out_dtype, tm, tn, name):
    m, k = a.shape
    n = w.shape[2]
    return pl.pallas_call(
        _mm_kernel,
        out_shape=jax.ShapeDtypeStruct((m, n), out_dtype),
        grid=(n // tn, m // tm),
        in_specs=[pl.BlockSpec((tm, k), lambda j, i: (i, 0)),
                  pl.BlockSpec((None, k, tn), lambda j, i: (layer, 0, j))],
        out_specs=pl.BlockSpec((tm, tn), lambda j, i: (i, j)),
        compiler_params=_cparams(("arbitrary", "arbitrary")),
        name=name,
    )(a, w)


def _norm_rows(x, eps):
    mu = jnp.mean(x, axis=-1, keepdims=True)
    xc = x - mu
    return xc * lax.rsqrt(jnp.mean(xc * xc, axis=-1, keepdims=True) + eps)


def _route_rows(logits):
    col = lax.broadcasted_iota(jnp.int32, logits.shape, 1).astype(F32)
    big = float(LANE)
    is_g = col < N_GROUPS
    lg = jnp.where(is_g, logits, NEG_INF)
    g_max = jnp.max(lg, axis=-1, keepdims=True)
    g_sum = jnp.sum(jnp.where(is_g, jnp.exp(lg - g_max), 0.0), axis=-1, keepdims=True)
    pg_top = 1.0 / g_sum
    g_idx = jnp.min(jnp.where(is_g & (lg == g_max), col, big), axis=-1, keepdims=True)
    lo = N_GROUPS + EXPERTS_PER_GROUP * g_idx
    sel = (col >= lo) & (col < lo + EXPERTS_PER_GROUP)
    le = jnp.where(sel, logits, NEG_INF)
    m1 = jnp.max(le, axis=-1, keepdims=True)
    den = jnp.sum(jnp.where(sel, jnp.exp(le - m1), 0.0), axis=-1, keepdims=True)
    i1 = jnp.min(jnp.where(sel & (le == m1), col, big), axis=-1, keepdims=True)
    rest = sel & (col != i1)
    le2 = jnp.where(rest, logits, NEG_INF)
    m2 = jnp.max(le2, axis=-1, keepdims=True)
    i2 = jnp.min(jnp.where(rest & (le2 == m2), col, big), axis=-1, keepdims=True)
    p1 = 1.0 / den
    p2 = jnp.exp(m2 - m1) / den
    tot = p1 + p2
    out = jnp.where(col == 0.0, i1 - N_GROUPS, 0.0)
    out = jnp.where(col == 1.0, i2 - N_GROUPS, out)
    out = jnp.where(col == 2.0, pg_top * p1 / tot, out)
    return jnp.where(col == 3.0, pg_top * p2 / tot, out)


def _mod_kernel(x_ref, tab_ref, u_ref):
    u_ref[...] = (_norm_rows(x_ref[...], ADA_EPS) * (1.0 + tab_ref[0, 2:3, :]) + tab_ref[0, 1:2, :]).astype(u_ref.dtype)


def _modulate0(x, tab, nt):
    rows, dm = x.shape
    return pl.pallas_call(
        _mod_kernel,
        out_shape=jax.ShapeDtypeStruct((rows, dm), BF16),
        grid=(rows // SEG,),
        in_specs=[pl.BlockSpec((SEG, dm), lambda i: (i, 0)),
                  pl.BlockSpec((1, 8, dm), lambda i: (2 * (i // nt) + jnp.minimum(i % nt, 1), 0, 0))],
        out_specs=pl.BlockSpec((SEG, dm), lambda i: (i, 0)),
        compiler_params=_cparams(("arbitrary",)),
        name="modulate0",
    )(x, tab)


def _mix_out_kernel(z_ref, x_ref, w_ref, tab_ref, lnp_ref, wr_ref, br_ref, xo_ref, vo_ref, ro_ref, *, alpha):
    m = _dot(z_ref[...], w_ref[...])
    xn = _norm_rows(alpha * x_ref[...] + tab_ref[0, 0:1, :] * m, LN_EPS) * lnp_ref[0:1, :] + lnp_ref[1:2, :]
    xo_ref[...] = xn
    v = _norm_rows(xn, ADA_EPS) * (1.0 + tab_ref[0, 2:3, :]) + tab_ref[0, 1:2, :]
    vo_ref[...] = v.astype(vo_ref.dtype)
    ro_ref[...] = _route_rows(_dot3(v, wr_ref[...]) + br_ref[0:1, :])


def _mix_out(z, x, w_out, layer, tab, lnp, w_r, b_r, nt, alpha):
    rows, dm = x.shape
    row = lambda i: (i, 0)
    const = lambda i: (0, 0)
    return pl.pallas_call(
        functools.partial(_mix_out_kernel, alpha=alpha),
        out_shape=(jax.ShapeDtypeStruct((rows, dm), F32), jax.ShapeDtypeStruct((rows, dm), BF16),
                   jax.ShapeDtypeStruct((rows, LANE), F32)),
        grid=(rows // SEG,),
        in_specs=[pl.BlockSpec((SEG, dm), row), pl.BlockSpec((SEG, dm), row),
                  pl.BlockSpec((None, dm, dm), lambda i: (layer, 0, 0)),
                  pl.BlockSpec((1, 8, dm), lambda i: (2 * (i // nt) + jnp.minimum(i % nt, 1), 0, 0)),
                  pl.BlockSpec((8, dm), const), pl.BlockSpec((dm, LANE), const), pl.BlockSpec((8, LANE), const)],
        out_specs=(pl.BlockSpec((SEG, dm), row), pl.BlockSpec((SEG, dm), row), pl.BlockSpec((SEG, LANE), row)),
        compiler_params=_cparams(("arbitrary",)),
        name="mix_out",
    )(z, x, w_out, tab, lnp, w_r, b_r)


def _moe_out_kernel(x_ref, y0_ref, y1_ref, tab_ref, lnp_ref, xo_ref, *u_ref, alpha):
    y = y0_ref[0].astype(F32) + y1_ref[0].astype(F32)
    xn = _norm_rows(alpha * x_ref[...] + tab_ref[0, 0:1, :] * y, LN_EPS) * lnp_ref[0:1, :] + lnp_ref[1:2, :]
    xo_ref[...] = xn
    if u_ref:
        u_ref[0][...] = (_norm_rows(xn, ADA_EPS) * (1.0 + tab_ref[0, 2:3, :]) + tab_ref[0, 1:2, :]).astype(BF16)


def _moe_out(x, yg, tab, lnp, bsz, nt, skip, alpha, emit_u):
    dm = x.shape[1]
    nk = nt - skip
    rows = bsz * nk * SEG
    xmap = lambda b, t: (b * nt + skip + t, 0)
    omap = lambda b, t: (b * nk + t, 0)
    out_shape = [jax.ShapeDtypeStruct((rows, dm), F32)]
    out_specs = [pl.BlockSpec((SEG, dm), omap)]
    if emit_u:
        out_shape.append(jax.ShapeDtypeStruct((rows, dm), BF16))
        out_specs.append(pl.BlockSpec((SEG, dm), omap))
    return pl.pallas_call(
        functools.partial(_moe_out_kernel, alpha=alpha),
        out_shape=tuple(out_shape),
        grid=(bsz, nk),
        in_specs=[pl.BlockSpec((SEG, dm), xmap),
                  pl.BlockSpec((1, SEG, dm), lambda b, t: (0, b * nk + t, 0)),
                  pl.BlockSpec((1, SEG, dm), lambda b, t: (1, b * nk + t, 0)),
                  pl.BlockSpec((1, 8, dm), lambda b, t: (2 * b + jnp.minimum(skip + t, 1), 0, 0)),
                  pl.BlockSpec((8, dm), lambda b, t: (0, 0))],
        out_specs=tuple(out_specs),
        compiler_params=_cparams(("arbitrary", "arbitrary")),
        name="moe_out",
    )(x, yg, yg, tab, lnp)


def _rope_kernel(pd_ref, pw_ref, cs_ref, dqk_ref, wqk_ref):
    cos = cs_ref[0]
    sin = cs_ref[1]
    first_half = lax.broadcasted_iota(jnp.int32, cos.shape, 1) % (2 * ROPE_AX_FREQS) < ROPE_AX_FREQS

    def rot(x, scale):
        x = x.astype(F32)
        sw = jnp.where(first_half, pltpu.roll(x, LANE - ROPE_AX_FREQS, axis=1), pltpu.roll(x, ROPE_AX_FREQS, axis=1))
        return ((x * cos + sw * sin) * scale).astype(BF16)

    for c in range(2 * DIFF_W // LANE):
        scale = DIFF_SCALE * LOG2E if c < DIFF_W // LANE else 1.0
        dqk_ref[:, c * LANE:(c + 1) * LANE] = rot(pd_ref[:, c * LANE:(c + 1) * LANE], scale)
    for c in range((WIN_W + WIN_KV_W) // LANE):
        scale = WIN_SCALE * LOG2E if c < WIN_W // LANE else 1.0
        wqk_ref[:, c * LANE:(c + 1) * LANE] = rot(pw_ref[:, c * LANE:(c + 1) * LANE], scale)


def _rope(p, cs, nt):
    rows = p.shape[0]
    wd, ww = 2 * DIFF_W, WIN_W + WIN_KV_W
    return pl.pallas_call(
        _rope_kernel,
        out_shape=(jax.ShapeDtypeStruct((rows, wd), BF16), jax.ShapeDtypeStruct((rows, ww), BF16)),
        grid=(rows // SEG,),
        in_specs=[pl.BlockSpec((SEG, wd), lambda i: (i, P_DIFF // wd)),
                  pl.BlockSpec((SEG, ww), lambda i: (i, P_WIN // ww)),
                  pl.BlockSpec((2, SEG, LANE), lambda i: (0, i % nt, 0))],
        out_specs=(pl.BlockSpec((SEG, wd), lambda i: (i, 0)), pl.BlockSpec((SEG, ww), lambda i: (i, 0))),
        compiler_params=_cparams(("arbitrary",)),
        name="rope",
    )(p, p, cs)


def _readout_kernel(y0_ref, y1_ref, bv0_ref, bv1_ref, g_ref, lnx_ref, avg_ref, o_ref):
    avg = avg_ref[...]

    def head_mean(t):
        hi, lo = _split(t)
        return jnp.dot(hi, avg, preferred_element_type=F32) + jnp.dot(lo, avg, preferred_element_type=F32)

    y = y0_ref[...] + y1_ref[...]
    dev = y - head_mean(y)
    yn = dev * lax.rsqrt(head_mean(dev * dev) + RWKV_GN_EPS) * lnx_ref[0:1, :] + lnx_ref[1:2, :]
    bonus = bv0_ref[...].astype(F32) + bv1_ref[...].astype(F32)
    o_ref[...] = ((yn + bonus) * g_ref[...].astype(F32)).astype(o_ref.dtype)


def _readout(y0, y1, bv0, bv1, g, lnx, avg, tm):
    rows = y0.shape[0]
    row = pl.BlockSpec((tm, RWKV_W), lambda i: (i, 0))
    return pl.pallas_call(
        _readout_kernel,
        out_shape=jax.ShapeDtypeStruct((rows, RWKV_W), BF16),
        grid=(rows // tm,),
        in_specs=[row, row, row, row, row,
                  pl.BlockSpec((2, RWKV_W), lambda i: (0, 0)),
                  pl.BlockSpec((RWKV_W, RWKV_W), lambda i: (0, 0))],
        out_specs=row,
        compiler_params=_cparams(("arbitrary",)),
        name="rwkv_readout",
    )(y0, y1, bv0, bv1, g, lnx, avg)


def _rwkv_features(d, chunk, p_ref, hp_ref, hn_ref, mu_ref, vec_ref, wup_ref, aup_ref, gup_ref, msk_ref,
                   bv_ref, g_ref, nc_ctx, nc_lat):
    n = CHUNK
    p = p_ref[...].astype(F32)
    first = jnp.logical_or(chunk == 0, chunk == nc_ctx)
    last = jnp.logical_or(chunk == nc_ctx - 1, chunk == nc_ctx + nc_lat - 1)
    hp = jnp.where(first, 0.0, hp_ref[15:16, :].astype(F32))
    hn = jnp.where(last, 0.0, hn_ref[0:1, :].astype(F32))
    row = lax.broadcasted_iota(jnp.int32, (n, 1), 0)
    prev = jnp.where(row == 0, hp, pltpu.roll(p, 1, axis=0))
    nxt = jnp.where(row == n - 1, hn, pltpu.roll(p, n - 1, axis=0))
    ps = p + mu_ref[0:1, :] * (prev - p) + mu_ref[1:2, :] * (nxt - p)

    r = ps[:, 0:RWKV_W]
    k = ps[:, RWKV_W:2 * RWKV_W]
    v = ps[:, 2 * RWKV_W:3 * RWKV_W]
    o = 3 * RWKV_W
    wd = jnp.tanh(ps[:, o:o + 2 * DECAY_LORA])
    ad = ps[:, o + 2 * DECAY_LORA:o + 2 * DECAY_LORA + 2 * AAA_LORA]
    gd = ps[:, o + 2 * DECAY_LORA + 2 * AAA_LORA:]
    w0 = vec_ref[d, 0:1, :]
    a0 = vec_ref[d, 1:2, :]
    k_k = vec_ref[d, 2:3, :]
    k_a = vec_ref[d, 3:4, :]
    r_k = vec_ref[d, 4:5, :]
    w_log = w0 + _dot(wd, wup_ref[d])
    a = _sigmoid(a0 + _dot(ad, aup_ref[d]))
    if g_ref is not None:
        g_ref[...] = _dot(_sigmoid(gd), gup_ref[...]).astype(g_ref.dtype)
    logw = -math.exp(-0.5) * _sigmoid(w_log)

    incl_f = msk_ref[d, 1]
    lw_hi, lw_lo = _split(logw)
    incl_b = incl_f.astype(BF16)
    cl = (jnp.dot(incl_b, lw_hi, preferred_element_type=F32)
          + jnp.dot(incl_b, lw_lo, preferred_element_type=F32))
    tot = jnp.sum(logw, axis=0, keepdims=True)
    e_in = jnp.exp(cl)
    e_ex = jnp.exp(cl - logw)
    e_inv = jnp.exp(-cl)
    e_end = jnp.exp(tot - cl)
    p_all = jnp.exp(tot)

    sls = [slice(h * RWKV_HEAD, (h + 1) * RWKV_HEAD) for h in range(RWKV_HEADS)]
    kk_n = k * k_k
    kk_sq = kk_n * kk_n
    kd_all = k * (1.0 + (a - 1.0) * k_a)
    rkd = r * kd_all * r_k
    inv_norm = [1.0 / jnp.maximum(jnp.sqrt(jnp.sum(kk_sq[:, sl], axis=-1, keepdims=True)), 1e-12) for sl in sls]
    bonus = [jnp.sum(rkd[:, sl], axis=-1, keepdims=True) for sl in sls]
    bv_ref[...] = jnp.concatenate([bonus[h] * v[:, sl] for h, sl in enumerate(sls)], axis=1).astype(bv_ref.dtype)
    kk_all = jnp.concatenate([kk_n[:, sl] * inv_norm[h] for h, sl in enumerate(sls)], axis=1)
    bd_all = kk_all * a
    return dict(
        v=v, strict_f=msk_ref[d, 0], incl_f=incl_f, p_all=p_all,
        kk_t=kk_all * e_ex,
        r_t=r * e_in,
        b_i=bd_all * e_inv, k_i=kd_all * e_inv,
        k_e=kd_all * e_end,
        b_e=bd_all * e_end)


def _rwkv_kernel(p0_ref, hp0_ref, hn0_ref, p1_ref, hp1_ref, hn1_ref, mu_ref, vec_ref, wup_ref, aup_ref,
                 gup_ref, msk_ref, y0_ref, y1_ref, bv0_ref, bv1_ref, g_ref, state, *, nc_ctx, nc_lat):
    i = pl.program_id(1)

    @pl.when(i == 0)
    def _():
        state[...] = jnp.zeros_like(state)

    n = CHUNK
    shared = (mu_ref, vec_ref, wup_ref, aup_ref, gup_ref, msk_ref)
    f = [_rwkv_features(0, i, p0_ref, hp0_ref, hn0_ref, *shared, bv0_ref, g_ref, nc_ctx, nc_lat),
         _rwkv_features(1, _rwkv_mirror(i, nc_ctx, nc_lat), p1_ref, hp1_ref, hn1_ref, *shared, bv1_ref, None,
                        nc_ctx, nc_lat)]

    hw = 2 * RWKV_HEAD
    lane_a = lax.broadcasted_iota(jnp.int32, (n, hw), 1) < RWKV_HEAD
    sel_a = lambda x: jnp.where(lane_a, x, 0.0)
    sel_b = lambda x: jnp.where(lane_a, 0.0, x)
    pick = lambda xa, xb: jnp.where(lane_a, xa, xb)
    bdiag = lambda x: jnp.concatenate([sel_a(x), sel_b(x)], axis=0)
    same_head = ((lax.broadcasted_iota(jnp.int32, (hw, hw), 0) < RWKV_HEAD)
                 == (lax.broadcasted_iota(jnp.int32, (hw, hw), 1) < RWKV_HEAD))
    eye_f = (lax.broadcasted_iota(jnp.int32, (n, n), 0) == lax.broadcasted_iota(jnp.int32, (n, n), 1)).astype(F32)
    eye2 = jnp.concatenate([eye_f, eye_f], axis=1)
    strict2 = [jnp.concatenate([f[d]["strict_f"]] * 2, axis=1) > 0.5 for d in range(2)]
    incl2 = [jnp.concatenate([f[d]["incl_f"]] * 2, axis=1) > 0.5 for d in range(2)]

    items = [(d, q) for d in range(2) for q in range(RWKV_HEADS // 2)]
    idx = range(len(items))
    get = lambda name, j: f[items[j][0]][name][:, items[j][1] * hw:(items[j][1] + 1) * hw]
    strict = [strict2[d] for d, _ in items]
    incl = [incl2[d] for d, _ in items]

    v_p = [get("v", j) for j in idx]
    kk_t = [get("kk_t", j) for j in idx]
    r_t = [get("r_t", j) for j in idx]
    b_i = [get("b_i", j) for j in idx]
    k_i = [get("k_i", j) for j in idx]
    lhs = [jnp.concatenate([kk_t[j], r_t[j]], axis=0) for j in idx]
    gram = [_dot_nt(lhs[j], jnp.concatenate([sel_a(b_i[j]), sel_a(k_i[j]), sel_b(k_i[j]), sel_b(b_i[j])], axis=0))
            for j in idx]
    gram_a = [gram[j][:, :hw] for j in idx]
    gram_b = [gram[j][:, hw:] for j in idx]
    pw = [jnp.where(strict[j], -pick(gram_a[j][:n], gram_b[j][:n]), 0.0) for j in idx]
    l_k = [jnp.where(strict[j], pick(gram_b[j][:n], gram_a[j][:n]), 0.0) for j in idx]
    m_rb = [jnp.where(incl[j], pick(gram_a[j][n:], gram_b[j][n:]), 0.0) for j in idx]
    m_rk = [jnp.where(incl[j], pick(gram_b[j][n:], gram_a[j][n:]), 0.0) for j in idx]
    v_x = [jnp.concatenate([sel_b(v_p[j]), sel_a(v_p[j])], axis=0) for j in idx]
    lm = [_dot(jnp.concatenate([l_k[j], m_rk[j]], axis=0), v_x[j]) for j in idx]
    lkv = [lm[j][:n] for j in idx]
    t_inv = [eye2 + pw[j] for j in idx]
    pw = [_dot(pw[j], bdiag(pw[j])) for j in idx]
    for _ in range(4):
        both = [_dot(jnp.concatenate([t_inv[j], pw[j]], axis=0), bdiag(pw[j])) for j in idx]
        t_inv = [t_inv[j] + both[j][:n] for j in idx]
        pw = [both[j][n:] for j in idx]
    t_inv = [t_inv[j] + _dot(t_inv[j], bdiag(pw[j])) for j in idx]
    tx = [_dot(t_inv[j], jnp.concatenate([bdiag(kk_t[j]), bdiag(lkv[j])], axis=1)) for j in idx]
    w_p = [tx[j][:, :hw] for j in idx]
    u_p = [tx[j][:, hw:] for j in idx]
    mw = [_dot(m_rb[j], jnp.concatenate([bdiag(w_p[j]), bdiag(u_p[j])], axis=1)) for j in idx]
    r_eff = [r_t[j] - mw[j][:, :hw] for j in idx]
    y_loc = [lm[j][n:] - mw[j][:, hw:] for j in idx]
    b_e = [-get("b_e", j) for j in idx]
    g_a = [_dot_tn(w_p[j], b_e[j]) for j in idx]
    g_d = [_dot_tn(jnp.concatenate([v_p[j], u_p[j]], axis=0),
                   jnp.concatenate([get("k_e", j), b_e[j]], axis=0)) for j in idx]
    s0 = [state[d, q] for d, q in items]
    ys = [_dot_nt(r_eff[j], bdiag(s0[j])) + y_loc[j] for j in idx]
    half = len(items) // 2
    y0_ref[...] = jnp.concatenate(ys[:half], axis=1)
    y1_ref[...] = jnp.concatenate(ys[half:], axis=1)
    s1 = [_dot(s0[j], jnp.where(same_head, g_a[j], 0.0)) for j in idx]
    for j, (d, q) in enumerate(items):
        state[d, q] = s0[j] * get("p_all", j) + s1[j] + pick(g_d[j][:n], g_d[j][n:])


def _rwkv_mirror(i, nc_ctx, nc_lat):
    return jnp.where(i < nc_ctx, nc_ctx - 1 - i, 2 * nc_ctx + nc_lat - 1 - i)


def _rwkv_scan(p, mu, vecs, wup, aup, gup, masks, bsz, n_ctx, n_lat):
    ltot = n_ctx + n_lat
    nc_ctx, nc_lat = n_ctx // CHUNK, n_lat // CHUNK
    nc = nc_ctx + nc_lat
    rows = bsz * ltot
    hb = CHUNK // 16
    n_hblk = rows // 16
    chunk_of = (lambda i: i, lambda i: _rwkv_mirror(i, nc_ctx, nc_lat))

    def specs(d):
        main = lambda b, i: (b * nc + chunk_of[d](i), 0)
        prev = lambda b, i: (jnp.maximum((b * nc + chunk_of[d](i)) * hb - 1, 0), 0)
        nxt = lambda b, i: (jnp.minimum((b * nc + chunk_of[d](i) + 1) * hb, n_hblk - 1), 0)
        return main, [pl.BlockSpec((CHUNK, RWKV_IN), main), pl.BlockSpec((16, RWKV_IN), prev),
                      pl.BlockSpec((16, RWKV_IN), nxt)]

    (main0, in0), (main1, in1) = specs(0), specs(1)
    whole = lambda shape: pl.BlockSpec(shape, lambda b, i: (0,) * len(shape))
    out_f = jax.ShapeDtypeStruct((rows, RWKV_W), F32)
    out_bf = jax.ShapeDtypeStruct((rows, RWKV_W), BF16)
    ospec = lambda m: pl.BlockSpec((CHUNK, RWKV_W), m)
    kern = functools.partial(_rwkv_kernel, nc_ctx=nc_ctx, nc_lat=nc_lat)
    return pl.pallas_call(
        kern,
        out_shape=(out_f, out_f, out_bf, out_bf, out_bf),
        grid=(bsz, nc),
        in_specs=in0 + in1 + [whole((2, RWKV_IN)), whole((2, 8, RWKV_W)),
                              whole((2, 2 * DECAY_LORA, RWKV_W)), whole((2, 2 * AAA_LORA, RWKV_W)),
                              whole((GATE_LORA, RWKV_W)), whole((2, 2, CHUNK, CHUNK))],
        out_specs=(ospec(main0), ospec(main1), ospec(main0), ospec(main1), ospec(main0)),
        scratch_shapes=[pltpu.VMEM((2, RWKV_HEADS // 2, RWKV_HEAD, 2 * RWKV_HEAD), F32)],
        compiler_params=_cparams(("arbitrary", "arbitrary")),
        name="rwkv_scan",
    )(p, p, p, p, p, p, mu, vecs, wup, aup, gup, masks)


def _diff_kernel(q_ref, k_ref, v_ref, par_ref, o_ref, *, n_ctx):
    t = pl.program_id(2)
    lam = par_ref[0:1, :]
    gain = par_ref[1:2, :]

    def attend(nk):
        q = q_ref[...]
        k = k_ref[0:nk, :]
        v = v_ref[0:nk, :]

        def softmax_v(sl):
            s = _dot_nt(q[:, sl], k[:, sl])
            e = jnp.exp2(s - jnp.max(s, axis=-1, keepdims=True))
            return _dot(e, v) / jnp.sum(e, axis=-1, keepdims=True)

        o = softmax_v(slice(0, DIFF_DK)) - lam * softmax_v(slice(DIFF_DK, 2 * DIFF_DK))
        o = o * lax.rsqrt(jnp.mean(o * o, axis=-1, keepdims=True) + 1e-5)
        o_ref[...] = (o * gain).astype(o_ref.dtype)

    @pl.when(t * SEG < n_ctx)
    def _():
        attend(n_ctx)

    @pl.when(t * SEG >= n_ctx)
    def _():
        attend(k_ref.shape[0])


def _diff_attn(qk, p, par, bsz, n_ctx, ltot):
    rows = bsz * ltot
    nt = ltot // SEG
    voff = (P_DIFF + 2 * DIFF_W) // DIFF_DV
    return pl.pallas_call(
        functools.partial(_diff_kernel, n_ctx=n_ctx),
        out_shape=jax.ShapeDtypeStruct((rows, DIFF_W), BF16),
        grid=(bsz, DIFF_HEADS, nt),
        in_specs=[pl.BlockSpec((SEG, DIFF_DV), lambda b, h, t: (b * nt + t, h)),
                  pl.BlockSpec((ltot, DIFF_DV), lambda b, h, t: (b, DIFF_HEADS + h)),
                  pl.BlockSpec((ltot, DIFF_DV), lambda b, h, t: (b, voff + h)),
                  pl.BlockSpec((8, DIFF_DV), lambda b, h, t: (0, 0))],
        out_specs=pl.BlockSpec((SEG, DIFF_DV), lambda b, h, t: (b * nt + t, h)),
        compiler_params=_cparams(("arbitrary", "arbitrary", "arbitrary")),
        name="diff_attn",
    )(qk, qk, p, par)


def _win_kernel(sink_ref, q_ref, kp_ref, kc_ref, kn_ref, kx_ref, vp_ref, vc_ref, vn_ref, vx_ref,
                o_ref, *, n_ctx, n_lat):
    blk = pl.program_id(1)
    ncb = n_ctx // QBLK
    nq = WIN_GROUP * QBLK

    def run(keys, vals, mask):
        outs = []
        for g in range(WIN_KV_HEADS):
            ksl = slice(g * WIN_HEAD, (g + 1) * WIN_HEAD)
            qg = jnp.concatenate(
                [q_ref[:, (g * WIN_GROUP + j) * WIN_HEAD:(g * WIN_GROUP + j + 1) * WIN_HEAD]
                 for j in range(WIN_GROUP)], axis=0)
            s = _dot_nt(qg, keys[:, ksl])
            if mask is not None:
                s = jnp.where(mask, s, NEG_INF)
            hrow = lax.broadcasted_iota(jnp.int32, (nq, 1), 0) // QBLK
            sink = jnp.zeros((nq, 1), F32)
            for j in range(WIN_GROUP):
                sink = jnp.where(hrow == j, sink_ref[g * WIN_GROUP + j] * LOG2E, sink)
            m = jnp.maximum(jnp.max(s, axis=-1, keepdims=True), sink)
            e = jnp.exp2(s - m)
            den = jnp.sum(e, axis=-1, keepdims=True) + jnp.exp2(sink - m)
            og = _dot(e, vals[:, ksl]) / den
            outs.extend(og[j * QBLK:(j + 1) * QBLK] for j in range(WIN_GROUP))
        o_ref[...] = jnp.concatenate(outs, axis=1).astype(o_ref.dtype)

    @pl.when(blk < ncb)
    def _():
        run(kx_ref[...], vx_ref[...], None)

    @pl.when(blk >= ncb)
    def _():
        j = blk - ncb
        keys = jnp.concatenate([kp_ref[...], kc_ref[...], kn_ref[...], kx_ref[...]], axis=0)
        vals = jnp.concatenate([vp_ref[...], vc_ref[...], vn_ref[...], vx_ref[...]], axis=0)
        nk = 3 * QBLK + n_ctx
        qpos = j * QBLK + lax.broadcasted_iota(jnp.int32, (nq, nk), 0) % QBLK
        col = lax.broadcasted_iota(jnp.int32, (nq, nk), 1)
        kpos = (j - 1) * QBLK + col
        band = (jnp.abs(qpos - kpos) <= WINDOW) & (kpos >= 0) & (kpos < n_lat)
        run(keys, vals, band | (col >= 3 * QBLK))


def _win_attn(sink, qk, p, bsz, n_ctx, n_lat):
    ltot = n_ctx + n_lat
    rows = bsz * ltot
    nb = ltot // QBLK
    ncb = n_ctx // QBLK
    koff = WIN_W // WIN_KV_W
    voff = (P_WIN + WIN_W + WIN_KV_W) // WIN_KV_W

    def lat_blk(b, t, shift):
        j = jnp.clip(t - ncb + shift, 0, nb - ncb - 1)
        return b * nb + ncb + j

    kspec = lambda shift, c: pl.BlockSpec((QBLK, WIN_KV_W), lambda b, t, s: (lat_blk(b, t, shift), c))
    xspec = lambda c: pl.BlockSpec((n_ctx, WIN_KV_W), lambda b, t, s: (b * (ltot // n_ctx), c))
    return pl.pallas_call(
        functools.partial(_win_kernel, n_ctx=n_ctx, n_lat=n_lat),
        out_shape=jax.ShapeDtypeStruct((rows, WIN_W), BF16),
        grid_spec=pltpu.PrefetchScalarGridSpec(
            num_scalar_prefetch=1,
            grid=(bsz, nb),
            in_specs=[pl.BlockSpec((QBLK, WIN_W), lambda b, t, s: (b * nb + t, 0)),
                      kspec(-1, koff), kspec(0, koff), kspec(1, koff), xspec(koff),
                      kspec(-1, voff), kspec(0, voff), kspec(1, voff), xspec(voff)],
            out_specs=pl.BlockSpec((QBLK, WIN_W), lambda b, t, s: (b * nb + t, 0))),
        compiler_params=_cparams(("arbitrary", "arbitrary")),
        name="win_attn",
    )(sink, qk, qk, qk, qk, qk, p, p, p, p)


def _merge_kernel(ya_ref, yb_ref, yc_ref, ga_ref, gb_ref, gc_ref, wa_ref, wb_ref, wc_ref, o_ref):
    z = _sigmoid(ga_ref[...].astype(F32)) * _dot(ya_ref[...], wa_ref[...])
    z = z + _sigmoid(gb_ref[...].astype(F32)) * _dot(yb_ref[...], wb_ref[...])
    z = z + _sigmoid(gc_ref[...].astype(F32)) * _dot(yc_ref[...], wc_ref[...])
    o_ref[...] = z.astype(o_ref.dtype)


def _merge(ya, yb, yc, p, w_branch, layer, tm, tn):
    m = ya.shape[0]
    assert RWKV_W == DIFF_W and (RWKV_W + DIFF_W) % WIN_W == 0
    nj = D_MODEL // tn
    goff = P_GATE // tn
    gspec = lambda br: pl.BlockSpec((tm, tn), lambda j, i: (i, goff + br * nj + j))
    return pl.pallas_call(
        _merge_kernel,
        out_shape=jax.ShapeDtypeStruct((m, D_MODEL), BF16),
        grid=(nj, m // tm),
        in_specs=[pl.BlockSpec((tm, RWKV_W), lambda j, i: (i, 0)),
                  pl.BlockSpec((tm, DIFF_W), lambda j, i: (i, 0)),
                  pl.BlockSpec((tm, WIN_W), lambda j, i: (i, 0)),
                  gspec(0), gspec(1), gspec(2),
                  pl.BlockSpec((None, RWKV_W, tn), lambda j, i: (layer, 0, j)),
                  pl.BlockSpec((None, DIFF_W, tn), lambda j, i: (layer, 1, j)),
                  pl.BlockSpec((None, WIN_W, tn), lambda j, i: (layer, (RWKV_W + DIFF_W) // WIN_W, j))],
        out_specs=pl.BlockSpec((tm, tn), lambda j, i: (i, j)),
        compiler_params=_cparams(("arbitrary", "arbitrary")),
        name="merge",
    )(ya, yb, yc, p, p, p, w_branch, w_branch, w_branch)


def _expert_kernel(be_ref, nu_ref, x_ref, gw_ref, w1_ref, w3_ref, w2_ref, o_ref, w1b, w3b, w2b):
    i = pl.program_id(0)
    prev = be_ref[jnp.maximum(i - 1, 0)]

    @pl.when(jnp.logical_or(i == 0, be_ref[i] != prev))
    def _():
        w1b[...] = w1_ref[...].astype(BF16)
        w3b[...] = w3_ref[...].astype(BF16)
        w2b[...] = w2_ref[...].astype(BF16)

    @pl.when(i < nu_ref[0])
    def _():
        x = x_ref[...]
        h1 = _dot(x, w1b[...])
        h = h1 * _sigmoid(h1) * _dot(x, w3b[...])
        o_ref[...] = (_dot(h, w2b[...]) * gw_ref[...]).astype(o_ref.dtype)

    @pl.when(i >= nu_ref[0])
    def _():
        o_ref[...] = jnp.zeros_like(o_ref)


def _experts(blk_expert, n_used, xb, gw, w1, w3, w2, layer):
    rows = xb.shape[0]
    nb = rows // MOE_BLK
    return pl.pallas_call(
        _expert_kernel,
        out_shape=jax.ShapeDtypeStruct((rows, D_MODEL), BF16),
        grid_spec=pltpu.PrefetchScalarGridSpec(
            num_scalar_prefetch=2,
            grid=(nb,),
            in_specs=[pl.BlockSpec((MOE_BLK, D_MODEL), lambda i, be, nu: (i, 0)),
                      pl.BlockSpec((MOE_BLK, 1), lambda i, be, nu: (i, 0)),
                      pl.BlockSpec((None, None, D_MODEL, D_EXPERT), lambda i, be, nu: (layer, be[i], 0, 0)),
                      pl.BlockSpec((None, None, D_MODEL, D_EXPERT), lambda i, be, nu: (layer, be[i], 0, 0)),
                      pl.BlockSpec((None, None, D_EXPERT, D_MODEL), lambda i, be, nu: (layer, be[i], 0, 0))],
            out_specs=pl.BlockSpec((MOE_BLK, D_MODEL), lambda i, be, nu: (i, 0)),
            scratch_shapes=[pltpu.VMEM((D_MODEL, D_EXPERT), BF16),
                            pltpu.VMEM((D_MODEL, D_EXPERT), BF16),
                            pltpu.VMEM((D_EXPERT, D_MODEL), BF16)]),
        compiler_params=_cparams(("arbitrary",)),
        name="experts",
    )(blk_expert, n_used, xb, gw, w1, w3, w2)


def _rope_tables(n_ctx, n_lat):
    rows = n_lat // GRID_W
    row = jnp.repeat(jnp.arange(rows), GRID_W).astype(F32)
    col = (jnp.arange(rows * GRID_W) % GRID_W).astype(F32)
    inv = ROPE_BASE ** (-jnp.arange(ROPE_AX_FREQS, dtype=F32) / ROPE_AX_FREQS)
    ang = jnp.stack([row[:, None] * inv, col[:, None] * inv], axis=1)
    cos, sin = jnp.cos(ang), jnp.sin(ang)
    cos4 = jnp.stack([cos, cos], axis=2).reshape(n_lat, 4 * ROPE_AX_FREQS)
    sin4 = jnp.stack([-sin, sin], axis=2).reshape(n_lat, 4 * ROPE_AX_FREQS)
    cos4 = jnp.concatenate([jnp.ones((n_ctx, 64), F32), cos4], axis=0)
    sin4 = jnp.concatenate([jnp.zeros((n_ctx, 64), F32), sin4], axis=0)
    return jnp.stack([jnp.tile(cos4, (1, LANE // 64)), jnp.tile(sin4, (1, LANE // 64))])


def _moe(route, v, w1, w3, w2, layer):
    t = v.shape[0]
    experts = route[:, :EXPERT_TOP_K].astype(jnp.int32)
    gates = route[:, EXPERT_TOP_K:2 * EXPERT_TOP_K]
    k = EXPERT_TOP_K
    a = t * k
    e_n = N_EXPERTS
    nb = -(-a // MOE_BLK) + e_n
    e_flat = experts.T.reshape(a)
    iota = jnp.arange(a, dtype=jnp.int32)
    e_s, order, g_s = lax.sort((e_flat, iota, gates.T.reshape(a)), num_keys=1, is_stable=True)
    ids = jnp.arange(e_n, dtype=jnp.int32)
    start = jnp.sum(e_s[None, :] < ids[:, None], axis=1, dtype=jnp.int32)
    counts = jnp.sum(e_s[None, :] == ids[:, None], axis=1, dtype=jnp.int32)
    padded = (counts + MOE_BLK - 1) // MOE_BLK * MOE_BLK
    pad_end = jnp.cumsum(padded)
    pad_start = pad_end - padded
    shift = jnp.sum(jnp.where(e_s[:, None] == ids[None, :], (pad_start - start)[None, :], 0), axis=1)
    pos = lax.sort((order, iota + shift), num_keys=1)[1]
    blk_first = jnp.arange(nb, dtype=jnp.int32) * MOE_BLK
    blk_expert = jnp.minimum(jnp.sum(pad_end[None, :] <= blk_first[:, None], axis=1, dtype=jnp.int32), e_n - 1)
    n_used = (pad_end[-1:] // MOE_BLK).astype(jnp.int32)
    rank = blk_first[:, None] + jnp.arange(MOE_BLK, dtype=jnp.int32)[None, :] - pad_start[blk_expert][:, None]
    filled = (rank < counts[blk_expert][:, None]).reshape(nb * MOE_BLK)
    sidx = jnp.clip(start[blk_expert][:, None] + rank, 0, a - 1).reshape(nb * MOE_BLK)
    spread = jnp.arange(nb * MOE_BLK, dtype=jnp.int32) % t
    slot_tok = jnp.where(filled, order.at[sidx].get(mode="promise_in_bounds") % t, spread)
    slot_gate = jnp.where(filled, g_s.at[sidx].get(mode="promise_in_bounds"), 0.0)
    xb = v.at[slot_tok].get(mode="promise_in_bounds")
    ys = _experts(blk_expert, n_used, xb, slot_gate[:, None], w1, w3, w2, layer)
    yg = lax.optimization_barrier(ys.at[pos].get(mode="promise_in_bounds"))
    return yg.reshape(k, t, D_MODEL)


def kernel(x, c, ctx, c_ctx, w_mod, b_mod, w_in, rwkv_mu, rwkv_w0, rwkv_w_up, rwkv_a0, rwkv_a_up,
           rwkv_g_up, rwkv_kvec, rwkv_lnx, diff_lam, diff_subln, win_sink, w_branch, w_out, ln_g, ln_b,
           w_rg, b_rg, w_re, b_re, w1, w3, w2):
    bsz, n_lat, dm = x.shape
    n_ctx = ctx.shape[1]
    depth = w_mod.shape[0]
    ltot = n_ctx + n_lat
    rows = bsz * ltot
    dn_alpha = (2 * depth) ** 0.25
    assert dm == D_MODEL and n_ctx % SEG == 0 and n_lat % SEG == 0 and ltot % n_ctx == 0

    nt = ltot // SEG
    cs = _rope_tables(n_ctx, n_lat)
    fwd = jnp.tril(jnp.ones((CHUNK, CHUNK), F32))
    masks = jnp.stack([jnp.stack([fwd - jnp.eye(CHUNK, dtype=F32), fwd]),
                       jnp.stack([fwd.T - jnp.eye(CHUNK, dtype=F32), fwd.T])])
    head_id = jnp.arange(RWKV_W) // RWKV_HEAD
    head_avg = ((head_id[:, None] == head_id[None, :]).astype(F32) / RWKV_HEAD).astype(BF16)

    xs = jnp.concatenate([ctx, x], axis=1).reshape(rows, dm)
    cvec = jnp.concatenate([c_ctx[None, :], c], axis=0)
    cpad = jnp.zeros((32, dm), F32).at[:bsz + 1].set(jax.nn.silu(cvec))
    w_in_p = jnp.concatenate([w_in[:, :, :RWKV_IN], jnp.zeros((depth, dm, P_GATE - RWKV_IN), F32),
                              w_in[:, :, GATE_OFF:], w_in[:, :, DIFF_OFF:GATE_OFF]], axis=2).astype(BF16)
    w_branch_b = w_branch.astype(BF16)
    w_out_b = w_out.astype(BF16)
    w_r = jnp.zeros((depth, dm, LANE), F32).at[:, :, :N_GROUPS].set(w_rg)
    w_r = w_r.at[:, :, N_GROUPS:N_GROUPS + N_EXPERTS].set(w_re)
    b_r = jnp.zeros((depth, 8, LANE), F32).at[:, :, :N_GROUPS].set(b_rg[:, None, :])
    b_r = b_r.at[:, :, N_GROUPS:N_GROUPS + N_EXPERTS].set(b_re[:, None, :])

    mods = [_mm(cpad, w_mod, i, F32, 32, 1024, "mod")[:bsz + 1] + b_mod[i] for i in range(depth)]

    def table(gate, shift, scale):
        def both(m, j):
            v = m[:, j * dm:(j + 1) * dm]
            return jnp.stack([jnp.broadcast_to(v[0], (bsz, dm)), v[1:]], axis=1)
        t = jnp.stack([both(*gate), both(*shift), both(*scale)], axis=2)
        return jnp.pad(t, ((0, 0), (0, 0), (0, 5), (0, 0))).reshape(2 * bsz, 8, dm)

    u = _modulate0(xs, table((mods[0], 0), (mods[0], 0), (mods[0], 1)), nt)
    for i in range(depth):
        last = i == depth - 1
        lam_init = 0.8 - 0.6 * math.exp(-0.3 * i)
        mod = mods[i]
        p = _mm(u, w_in_p, i, BF16, 512, 2048, "in_proj")

        vecs = jnp.zeros((2, 8, RWKV_W), F32)
        vecs = vecs.at[:, 0].set(rwkv_w0[i]).at[:, 1].set(rwkv_a0[i])
        vecs = vecs.at[:, 2:5].set(jnp.broadcast_to(rwkv_kvec[i][None], (2, 3, RWKV_W)))
        wup = jnp.zeros((2, 2 * DECAY_LORA, RWKV_W), BF16)
        aup = jnp.zeros((2, 2 * AAA_LORA, RWKV_W), BF16)
        for d in range(2):
            wup = wup.at[d, d * DECAY_LORA:(d + 1) * DECAY_LORA].set(rwkv_w_up[i, d].astype(BF16))
            aup = aup.at[d, d * AAA_LORA:(d + 1) * AAA_LORA].set(rwkv_a_up[i, d].astype(BF16))
        scan = _rwkv_scan(p, rwkv_mu[i], vecs, wup, aup, rwkv_g_up[i].astype(BF16), masks, bsz, n_ctx, n_lat)
        ya = _readout(*scan, rwkv_lnx[i], head_avg, 512)

        dqk, wqk = _rope(p, cs, nt)
        lf = diff_lam[i]
        lam = jnp.exp(jnp.sum(lf[0] * lf[1])) - jnp.exp(jnp.sum(lf[2] * lf[3])) + lam_init
        par = jnp.zeros((8, DIFF_DV), F32).at[0].set(lam).at[1].set(diff_subln[i] * (1 - lam_init))
        yb = _diff_attn(dqk, p, par, bsz, n_ctx, ltot)
        yc = _win_attn(win_sink[i], wqk, p, bsz, n_ctx, n_lat)
        z = _merge(ya, yb, yc, p, w_branch_b, i, 512, 1024)

        lnp = lambda j: jnp.zeros((8, dm), F32).at[0].set(ln_g[i, j]).at[1].set(ln_b[i, j])
        xs, v, route = _mix_out(z, xs, w_out_b, i, table((mod, 2), (mod, 3), (mod, 4)), lnp(0),
                                w_r[i], b_r[i], nt, dn_alpha)

        if last:
            lat = lambda t: t.reshape(bsz, ltot, -1)[:, n_ctx:].reshape(bsz * n_lat, -1)
            yg = _moe(lat(route), lat(v), w1, w3, w2, i)
            tab = table((mod, 5), (mod, 0), (mod, 1))
            (out,) = _moe_out(xs, yg, tab, lnp(1), bsz, nt, n_ctx // SEG, dn_alpha, False)
            return out.reshape(bsz, n_lat, dm)
        yg = _moe(route, v, w1, w3, w2, i)
        tab = table((mod, 5), (mods[i + 1], 0), (mods[i + 1], 1))
        xs, u = _moe_out(xs, yg, tab, lnp(1), bsz, nt, 0, dn_alpha, True)
    return None
```

```python
import functools
import math

import jax
import jax.numpy as jnp
from jax import lax
from jax.experimental import pallas as pl
from jax.experimental.pallas import tpu as pltpu

F32 = jnp.float32
BF16 = jnp.bfloat16

D_MODEL = 2048
GRID_W = 64
RWKV_HEADS = 12
RWKV_HEAD = 64
RWKV_W = RWKV_HEADS * RWKV_HEAD
DECAY_LORA = 64
AAA_LORA = 64
GATE_LORA = 128
RWKV_GN_EPS = 64e-5
DIFF_HEADS = 6
DIFF_DK = 64
DIFF_DV = 2 * DIFF_DK
DIFF_W = DIFF_HEADS * DIFF_DV
DIFF_SCALE = DIFF_DK ** -0.5
WIN_Q_HEADS = 8
WIN_KV_HEADS = 2
WIN_GROUP = WIN_Q_HEADS // WIN_KV_HEADS
WIN_HEAD = 64
WIN_W = WIN_Q_HEADS * WIN_HEAD
WIN_KV_W = WIN_KV_HEADS * WIN_HEAD
WIN_SCALE = WIN_HEAD ** -0.5
WINDOW = 128
QBLK = WINDOW
MIX_W = RWKV_W + DIFF_W + WIN_W
N_BRANCH = 3
ROPE_BASE = 10000.0
ROPE_AX_FREQS = 16
RWKV_IN = 3 * RWKV_W + 2 * DECAY_LORA + 2 * AAA_LORA + GATE_LORA
DIFF_IN = 3 * DIFF_W
WIN_IN = WIN_W + 2 * WIN_KV_W
DIFF_OFF = RWKV_IN
WIN_OFF = DIFF_OFF + DIFF_IN
GATE_OFF = WIN_OFF + WIN_IN
N_IN = GATE_OFF + N_BRANCH * D_MODEL
N_GROUPS = 4
EXPERTS_PER_GROUP = 8
N_EXPERTS = N_GROUPS * EXPERTS_PER_GROUP
EXPERT_TOP_K = 2
D_EXPERT = D_MODEL // 4
MOE_BLK = 256
ADA_EPS = 1e-6
LN_EPS = 1e-5
NEG_INF = -1e30
LOG2E = math.log2(math.e)

LANE = 128
SEG = 256
P_RWKV = 0
P_GATE = 3072
P_DIFF = P_GATE + N_BRANCH * D_MODEL
P_WIN = P_DIFF + DIFF_IN
P_COLS = P_WIN + WIN_IN
CHUNK = 64
VMEM_LIMIT = 56 * 1024 * 1024


def _cparams(sem):
    return pltpu.CompilerParams(dimension_semantics=sem, vmem_limit_bytes=VMEM_LIMIT)


def _dot(a, b):
    return jnp.dot(a.astype(BF16), b.astype(BF16), preferred_element_type=F32)


def _dot_nt(a, b):
    return lax.dot_general(a.astype(BF16), b.astype(BF16), (((1,), (1,)), ((), ())),
                           preferred_element_type=F32)


def _dot_tn(a, b):
    return lax.dot_general(a.astype(BF16), b.astype(BF16), (((0,), (0,)), ((), ())),
                           preferred_element_type=F32)


def _split(x):
    hi = x.astype(BF16)
    lo = (x - hi.astype(F32)).astype(BF16)
    return hi, lo


def _dot3(a, b):
    ah, al = _split(a)
    bh, bl = _split(b)
    d = functools.partial(jnp.dot, preferred_element_type=F32)
    return d(ah, bh) + (d(ah, bl) + d(al, bh))


def _sigmoid(x):
    return 1.0 / (1.0 + jnp.exp(-x))


def _mm_kernel(a_ref, w_ref, o_ref):
    o_ref[...] = _dot(a_ref[...], w_ref[...]).astype(o_ref.dtype)


def _mm(a, w, layer, out_dtype, tm, tn, name):
    m, k = a.shape
    n = w.shape[2]
    return pl.pallas_call(
        _mm_kernel,
        out_shape=jax.ShapeDtypeStruct((m, n), out_dtype),
        grid=(n // tn, m // tm),
        in_specs=[pl.BlockSpec((tm, k), lambda j, i: (i, 0)),
                  pl.BlockSpec((None, k, tn), lambda j, i: (layer, 0, j))],
        out_specs=pl.BlockSpec((tm, tn), lambda j, i: (i, j)),
        compiler_params=_cparams(("arbitrary", "arbitrary")),
        name=name,
    )(a, w)


def _norm_rows(x, eps):
    mu = jnp.mean(x, axis=-1, keepdims=True)
    xc = x - mu
    return xc * lax.rsqrt(jnp.mean(xc * xc, axis=-1, keepdims=True) + eps)


def _route_rows(logits):
    col = lax.broadcasted_iota(jnp.int32, logits.shape, 1).astype(F32)
    big = float(LANE)
    is_g = col < N_GROUPS
    lg = jnp.where(is_g, logits, NEG_INF)
    g_max = jnp.max(lg, axis=-1, keepdims=True)
    g_sum = jnp.sum(jnp.where(is_g, jnp.exp(lg - g_max), 0.0), axis=-1, keepdims=True)
    pg_top = 1.0 / g_sum
    g_idx = jnp.min(jnp.where(is_g & (lg == g_max), col, big), axis=-1, keepdims=True)
    lo = N_GROUPS + EXPERTS_PER_GROUP * g_idx
    sel = (col >= lo) & (col < lo + EXPERTS_PER_GROUP)
    le = jnp.where(sel, logits, NEG_INF)
    m1 = jnp.max(le, axis=-1, keepdims=True)
    den = jnp.sum(jnp.where(sel, jnp.exp(le - m1), 0.0), axis=-1, keepdims=True)
    i1 = jnp.min(jnp.where(sel & (le == m1), col, big), axis=-1, keepdims=True)
    rest = sel & (col != i1)
    le2 = jnp.where(rest, logits, NEG_INF)
    m2 = jnp.max(le2, axis=-1, keepdims=True)
    i2 = jnp.min(jnp.where(rest & (le2 == m2), col, big), axis=-1, keepdims=True)
    p1 = 1.0 / den
    p2 = jnp.exp(m2 - m1) / den
    tot = p1 + p2
    out = jnp.where(col == 0.0, i1 - N_GROUPS, 0.0)
    out = jnp.where(col == 1.0, i2 - N_GROUPS, out)
    out = jnp.where(col == 2.0, pg_top * p1 / tot, out)
    return jnp.where(col == 3.0, pg_top * p2 / tot, out)


def _mod_kernel(x_ref, tab_ref, u_ref):
    u_ref[...] = (_norm_rows(x_ref[...], ADA_EPS) * (1.0 + tab_ref[0, 2:3, :]) + tab_ref[0, 1:2, :]).astype(u_ref.dtype)


def _modulate0(x, tab, nt):
    rows, dm = x.shape
    return pl.pallas_call(
        _mod_kernel,
        out_shape=jax.ShapeDtypeStruct((rows, dm), BF16),
        grid=(rows // SEG,),
        in_specs=[pl.BlockSpec((SEG, dm), lambda i: (i, 0)),
                  pl.BlockSpec((1, 8, dm), lambda i: (2 * (i // nt) + jnp.minimum(i % nt, 1), 0, 0))],
        out_specs=pl.BlockSpec((SEG, dm), lambda i: (i, 0)),
        compiler_params=_cparams(("arbitrary",)),
        name="modulate0",
    )(x, tab)


def _mix_out_kernel(z_ref, x_ref, w_ref, tab_ref, lnp_ref, wr_ref, br_ref, xo_ref, vo_ref, ro_ref, *, alpha):
    m = _dot(z_ref[...], w_ref[...])
    xn = _norm_rows(alpha * x_ref[...] + tab_ref[0, 0:1, :] * m, LN_EPS) * lnp_ref[0:1, :] + lnp_ref[1:2, :]
    xo_ref[...] = xn
    v = _norm_rows(xn, ADA_EPS) * (1.0 + tab_ref[0, 2:3, :]) + tab_ref[0, 1:2, :]
    vo_ref[...] = v.astype(vo_ref.dtype)
    ro_ref[...] = _route_rows(_dot3(v, wr_ref[...]) + br_ref[0:1, :])


def _mix_out(z, x, w_out, layer, tab, lnp, w_r, b_r, nt, alpha):
    rows, dm = x.shape
    row = lambda i: (i, 0)
    const = lambda i: (0, 0)
    return pl.pallas_call(
        functools.partial(_mix_out_kernel, alpha=alpha),
        out_shape=(jax.ShapeDtypeStruct((rows, dm), F32), jax.ShapeDtypeStruct((rows, dm), BF16),
                   jax.ShapeDtypeStruct((rows, LANE), F32)),
        grid=(rows // SEG,),
        in_specs=[pl.BlockSpec((SEG, dm), row), pl.BlockSpec((SEG, dm), row),
                  pl.BlockSpec((None, dm, dm), lambda i: (layer, 0, 0)),
                  pl.BlockSpec((1, 8, dm), lambda i: (2 * (i // nt) + jnp.minimum(i % nt, 1), 0, 0)),
                  pl.BlockSpec((8, dm), const), pl.BlockSpec((dm, LANE), const), pl.BlockSpec((8, LANE), const)],
        out_specs=(pl.BlockSpec((SEG, dm), row), pl.BlockSpec((SEG, dm), row), pl.BlockSpec((SEG, LANE), row)),
        compiler_params=_cparams(("arbitrary",)),
        name="mix_out",
    )(z, x, w_out, tab, lnp, w_r, b_r)


def _moe_out_kernel(x_ref, y0_ref, y1_ref, tab_ref, lnp_ref, xo_ref, *u_ref, alpha):
    y = y0_ref[0].astype(F32) + y1_ref[0].astype(F32)
    xn = _norm_rows(alpha * x_ref[...] + tab_ref[0, 0:1, :] * y, LN_EPS) * lnp_ref[0:1, :] + lnp_ref[1:2, :]
    xo_ref[...] = xn
    if u_ref:
        u_ref[0][...] = (_norm_rows(xn, ADA_EPS) * (1.0 + tab_ref[0, 2:3, :]) + tab_ref[0, 1:2, :]).astype(BF16)


def _moe_out(x, yg, tab, lnp, bsz, nt, skip, alpha, emit_u):
    dm = x.shape[1]
    nk = nt - skip
    rows = bsz * nk * SEG
    xmap = lambda b, t: (b * nt + skip + t, 0)
    omap = lambda b, t: (b * nk + t, 0)
    out_shape = [jax.ShapeDtypeStruct((rows, dm), F32)]
    out_specs = [pl.BlockSpec((SEG, dm), omap)]
    if emit_u:
        out_shape.append(jax.ShapeDtypeStruct((rows, dm), BF16))
        out_specs.append(pl.BlockSpec((SEG, dm), omap))
    return pl.pallas_call(
        functools.partial(_moe_out_kernel, alpha=alpha),
        out_shape=tuple(out_shape),
        grid=(bsz, nk),
        in_specs=[pl.BlockSpec((SEG, dm), xmap),
                  pl.BlockSpec((1, SEG, dm), lambda b, t: (0, b * nk + t, 0)),
                  pl.BlockSpec((1, SEG, dm), lambda b, t: (1, b * nk + t, 0)),
                  pl.BlockSpec((1, 8, dm), lambda b, t: (2 * b + jnp.minimum(skip + t, 1), 0, 0)),
                  pl.BlockSpec((8, dm), lambda b, t: (0, 0))],
        out_specs=tuple(out_specs),
        compiler_params=_cparams(("arbitrary", "arbitrary")),
        name="moe_out",
    )(x, yg, yg, tab, lnp)


def _rope_kernel(pd_ref, pw_ref, cs_ref, dqk_ref, wqk_ref):
    cos = cs_ref[0]
    sin = cs_ref[1]
    first_half = lax.broadcasted_iota(jnp.int32, cos.shape, 1) % (2 * ROPE_AX_FREQS) < ROPE_AX_FREQS

    def rot(x, scale):
        x = x.astype(F32)
        sw = jnp.where(first_half, pltpu.roll(x, LANE - ROPE_AX_FREQS, axis=1), pltpu.roll(x, ROPE_AX_FREQS, axis=1))
        return ((x * cos + sw * sin) * scale).astype(BF16)

    for c in range(2 * DIFF_W // LANE):
        scale = DIFF_SCALE * LOG2E if c < DIFF_W // LANE else 1.0
        dqk_ref[:, c * LANE:(c + 1) * LANE] = rot(pd_ref[:, c * LANE:(c + 1) * LANE], scale)
    for c in range((WIN_W + WIN_KV_W) // LANE):
        scale = WIN_SCALE * LOG2E if c < WIN_W // LANE else 1.0
        wqk_ref[:, c * LANE:(c + 1) * LANE] = rot(pw_ref[:, c * LANE:(c + 1) * LANE], scale)


def _rope(p, cs, nt):
    rows = p.shape[0]
    wd, ww = 2 * DIFF_W, WIN_W + WIN_KV_W
    return pl.pallas_call(
        _rope_kernel,
        out_shape=(jax.ShapeDtypeStruct((rows, wd), BF16), jax.ShapeDtypeStruct((rows, ww), BF16)),
        grid=(rows // SEG,),
        in_specs=[pl.BlockSpec((SEG, wd), lambda i: (i, P_DIFF // wd)),
                  pl.BlockSpec((SEG, ww), lambda i: (i, P_WIN // ww)),
                  pl.BlockSpec((2, SEG, LANE), lambda i: (0, i % nt, 0))],
        out_specs=(pl.BlockSpec((SEG, wd), lambda i: (i, 0)), pl.BlockSpec((SEG, ww), lambda i: (i, 0))),
        compiler_params=_cparams(("arbitrary",)),
        name="rope",
    )(p, p, cs)


def _readout_kernel(y0_ref, y1_ref, bv0_ref, bv1_ref, g_ref, lnx_ref, avg_ref, o_ref):
    avg = avg_ref[...]

    def head_mean(t):
        hi, lo = _split(t)
        return jnp.dot(hi, avg, preferred_element_type=F32) + jnp.dot(lo, avg, preferred_element_type=F32)

    y = y0_ref[...] + y1_ref[...]
    dev = y - head_mean(y)
    yn = dev * lax.rsqrt(head_mean(dev * dev) + RWKV_GN_EPS) * lnx_ref[0:1, :] + lnx_ref[1:2, :]
    bonus = bv0_ref[...].astype(F32) + bv1_ref[...].astype(F32)
    o_ref[...] = ((yn + bonus) * g_ref[...].astype(F32)).astype(o_ref.dtype)


def _readout(y0, y1, bv0, bv1, g, lnx, avg, tm):
    rows = y0.shape[0]
    row = pl.BlockSpec((tm, RWKV_W), lambda i: (i, 0))
    return pl.pallas_call(
        _readout_kernel,
        out_shape=jax.ShapeDtypeStruct((rows, RWKV_W), BF16),
        grid=(rows // tm,),
        in_specs=[row, row, row, row, row,
                  pl.BlockSpec((2, RWKV_W), lambda i: (0, 0)),
                  pl.BlockSpec((RWKV_W, RWKV_W), lambda i: (0, 0))],
        out_specs=row,
        compiler_params=_cparams(("arbitrary",)),
        name="rwkv_readout",
    )(y0, y1, bv0, bv1, g, lnx, avg)


def _rwkv_features(d, chunk, p_ref, hp_ref, hn_ref, mu_ref, vec_ref, wup_ref, aup_ref, gup_ref, msk_ref,
                   bv_ref, g_ref, nc_ctx, nc_lat):
    n = CHUNK
    p = p_ref[...].astype(F32)
    first = jnp.logical_or(chunk == 0, chunk == nc_ctx)
    last = jnp.logical_or(chunk == nc_ctx - 1, chunk == nc_ctx + nc_lat - 1)
    hp = jnp.where(first, 0.0, hp_ref[15:16, :].astype(F32))
    hn = jnp.where(last, 0.0, hn_ref[0:1, :].astype(F32))
    row = lax.broadcasted_iota(jnp.int32, (n, 1), 0)
    prev = jnp.where(row == 0, hp, pltpu.roll(p, 1, axis=0))
    nxt = jnp.where(row == n - 1, hn, pltpu.roll(p, n - 1, axis=0))
    ps = p + mu_ref[0:1, :] * (prev - p) + mu_ref[1:2, :] * (nxt - p)

    r = ps[:, 0:RWKV_W]
    k = ps[:, RWKV_W:2 * RWKV_W]
    v = ps[:, 2 * RWKV_W:3 * RWKV_W]
    o = 3 * RWKV_W
    wd = jnp.tanh(ps[:, o:o + 2 * DECAY_LORA])
    ad = ps[:, o + 2 * DECAY_LORA:o + 2 * DECAY_LORA + 2 * AAA_LORA]
    gd = ps[:, o + 2 * DECAY_LORA + 2 * AAA_LORA:]
    w0 = vec_ref[d, 0:1, :]
    a0 = vec_ref[d, 1:2, :]
    k_k = vec_ref[d, 2:3, :]
    k_a = vec_ref[d, 3:4, :]
    r_k = vec_ref[d, 4:5, :]
    w_log = w0 + _dot(wd, wup_ref[d])
    a = _sigmoid(a0 + _dot(ad, aup_ref[d]))
    if g_ref is not None:
        g_ref[...] = _dot(_sigmoid(gd), gup_ref[...]).astype(g_ref.dtype)
    logw = -math.exp(-0.5) * _sigmoid(w_log)

    incl_f = msk_ref[d, 1]
    lw_hi, lw_lo = _split(logw)
    incl_b = incl_f.astype(BF16)
    cl = (jnp.dot(incl_b, lw_hi, preferred_element_type=F32)
          + jnp.dot(incl_b, lw_lo, preferred_element_type=F32))
    tot = jnp.sum(logw, axis=0, keepdims=True)
    e_in = jnp.exp(cl)
    e_ex = jnp.exp(cl - logw)
    e_inv = jnp.exp(-cl)
    e_end = jnp.exp(tot - cl)
    p_all = jnp.exp(tot)

    sls = [slice(h * RWKV_HEAD, (h + 1) * RWKV_HEAD) for h in range(RWKV_HEADS)]
    kk_n = k * k_k
    kk_sq = kk_n * kk_n
    kd_all = k * (1.0 + (a - 1.0) * k_a)
    rkd = r * kd_all * r_k
    inv_norm = [1.0 / jnp.maximum(jnp.sqrt(jnp.sum(kk_sq[:, sl], axis=-1, keepdims=True)), 1e-12) for sl in sls]
    bonus = [jnp.sum(rkd[:, sl], axis=-1, keepdims=True) for sl in sls]
    bv_ref[...] = jnp.concatenate([bonus[h] * v[:, sl] for h, sl in enumerate(sls)], axis=1).astype(bv_ref.dtype)
    kk_all = jnp.concatenate([kk_n[:, sl] * inv_norm[h] for h, sl in enumerate(sls)], axis=1)
    bd_all = kk_all * a
    return dict(
        v=v, strict_f=msk_ref[d, 0], incl_f=incl_f, p_all=p_all,
        kk_t=kk_all * e_ex,
        r_t=r * e_in,
        b_i=bd_all * e_inv, k_i=kd_all * e_inv,
        k_e=kd_all * e_end,
        b_e=bd_all * e_end)


def _rwkv_kernel(p0_ref, hp0_ref, hn0_ref, p1_ref, hp1_ref, hn1_ref, mu_ref, vec_ref, wup_ref, aup_ref,
                 gup_ref, msk_ref, y0_ref, y1_ref, bv0_ref, bv1_ref, g_ref, state, *, nc_ctx, nc_lat):
    i = pl.program_id(1)

    @pl.when(i == 0)
    def _():
        state[...] = jnp.zeros_like(state)

    n = CHUNK
    shared = (mu_ref, vec_ref, wup_ref, aup_ref, gup_ref, msk_ref)
    f = [_rwkv_features(0, i, p0_ref, hp0_ref, hn0_ref, *shared, bv0_ref, g_ref, nc_ctx, nc_lat),
         _rwkv_features(1, _rwkv_mirror(i, nc_ctx, nc_lat), p1_ref, hp1_ref, hn1_ref, *shared, bv1_ref, None,
                        nc_ctx, nc_lat)]

    hw = 2 * RWKV_HEAD
    lane_a = lax.broadcasted_iota(jnp.int32, (n, hw), 1) < RWKV_HEAD
    sel_a = lambda x: jnp.where(lane_a, x, 0.0)
    sel_b = lambda x: jnp.where(lane_a, 0.0, x)
    pick = lambda xa, xb: jnp.where(lane_a, xa, xb)
    bdiag = lambda x: jnp.concatenate([sel_a(x), sel_b(x)], axis=0)
    same_head = ((lax.broadcasted_iota(jnp.int32, (hw, hw), 0) < RWKV_HEAD)
                 == (lax.broadcasted_iota(jnp.int32, (hw, hw), 1) < RWKV_HEAD))
    eye_f = (lax.broadcasted_iota(jnp.int32, (n, n), 0) == lax.broadcasted_iota(jnp.int32, (n, n), 1)).astype(F32)
    eye2 = jnp.concatenate([eye_f, eye_f], axis=1)
    strict2 = [jnp.concatenate([f[d]["strict_f"]] * 2, axis=1) > 0.5 for d in range(2)]
    incl2 = [jnp.concatenate([f[d]["incl_f"]] * 2, axis=1) > 0.5 for d in range(2)]

    items = [(d, q) for d in range(2) for q in range(RWKV_HEADS // 2)]
    idx = range(len(items))
    get = lambda name, j: f[items[j][0]][name][:, items[j][1] * hw:(items[j][1] + 1) * hw]
    strict = [strict2[d] for d, _ in items]
    incl = [incl2[d] for d, _ in items]

    v_p = [get("v", j) for j in idx]
    kk_t = [get("kk_t", j) for j in idx]
    r_t = [get("r_t", j) for j in idx]
    b_i = [get("b_i", j) for j in idx]
    k_i = [get("k_i", j) for j in idx]
    lhs = [jnp.concatenate([kk_t[j], r_t[j]], axis=0) for j in idx]
    gram = [_dot_nt(lhs[j], jnp.concatenate([sel_a(b_i[j]), sel_a(k_i[j]), sel_b(k_i[j]), sel_b(b_i[j])], axis=0))
            for j in idx]
    gram_a = [gram[j][:, :hw] for j in idx]
    gram_b = [gram[j][:, hw:] for j in idx]
    pw = [jnp.where(strict[j], -pick(gram_a[j][:n], gram_b[j][:n]), 0.0) for j in idx]
    l_k = [jnp.where(strict[j], pick(gram_b[j][:n], gram_a[j][:n]), 0.0) for j in idx]
    m_rb = [jnp.where(incl[j], pick(gram_a[j][n:], gram_b[j][n:]), 0.0) for j in idx]
    m_rk = [jnp.where(incl[j], pick(gram_b[j][n:], gram_a[j][n:]), 0.0) for j in idx]
    v_x = [jnp.concatenate([sel_b(v_p[j]), sel_a(v_p[j])], axis=0) for j in idx]
    lm = [_dot(jnp.concatenate([l_k[j], m_rk[j]], axis=0), v_x[j]) for j in idx]
    lkv = [lm[j][:n] for j in idx]
    tok_r = lax.broadcasted_iota(jnp.int32, (n, hw), 0)
    tok_c = lax.broadcasted_iota(jnp.int32, (n, hw), 1) % n
    blk = lambda m: (tok_r // m) == (tok_c // m)
    nl = pw
    pw = [jnp.where(blk(8), nl[j], 0.0) for j in idx]
    t_inv = [eye2 + pw[j] for j in idx]
    pw = [_dot(pw[j], bdiag(pw[j])) for j in idx]
    both = [_dot(jnp.concatenate([t_inv[j], pw[j]], axis=0), bdiag(pw[j])) for j in idx]
    t_inv = [t_inv[j] + both[j][:n] for j in idx]
    t_inv = [t_inv[j] + _dot(t_inv[j], bdiag(both[j][n:])) for j in idx]
    for m in (8, 16, 32):
        off = [jnp.where(jnp.logical_and(blk(2 * m), jnp.logical_not(blk(m))), nl[j], 0.0) for j in idx]
        ct = [_dot(off[j], bdiag(t_inv[j])) for j in idx]
        t_inv = [t_inv[j] + _dot(t_inv[j], bdiag(ct[j])) for j in idx]
    tx = [_dot(t_inv[j], jnp.concatenate([bdiag(kk_t[j]), bdiag(lkv[j])], axis=1)) for j in idx]
    w_p = [tx[j][:, :hw] for j in idx]
    u_p = [tx[j][:, hw:] for j in idx]
    mw = [_dot(m_rb[j], jnp.concatenate([bdiag(w_p[j]), bdiag(u_p[j])], axis=1)) for j in idx]
    r_eff = [r_t[j] - mw[j][:, :hw] for j in idx]
    y_loc = [lm[j][n:] - mw[j][:, hw:] for j in idx]
    b_e = [-get("b_e", j) for j in idx]
    g_a = [_dot_tn(w_p[j], b_e[j]) for j in idx]
    g_d = [_dot_tn(jnp.concatenate([v_p[j], u_p[j]], axis=0),
                   jnp.concatenate([get("k_e", j), b_e[j]], axis=0)) for j in idx]
    s0 = [state[d, q] for d, q in items]
    ys = [_dot_nt(r_eff[j], bdiag(s0[j])) + y_loc[j] for j in idx]
    half = len(items) // 2
    y0_ref[...] = jnp.concatenate(ys[:half], axis=1)
    y1_ref[...] = jnp.concatenate(ys[half:], axis=1)
    s1 = [_dot(s0[j], jnp.where(same_head, g_a[j], 0.0)) for j in idx]
    for j, (d, q) in enumerate(items):
        state[d, q] = s0[j] * get("p_all", j) + s1[j] + pick(g_d[j][:n], g_d[j][n:])


def _rwkv_mirror(i, nc_ctx, nc_lat):
    return jnp.where(i < nc_ctx, nc_ctx - 1 - i, 2 * nc_ctx + nc_lat - 1 - i)


def _rwkv_scan(p, mu, vecs, wup, aup, gup, masks, bsz, n_ctx, n_lat):
    ltot = n_ctx + n_lat
    nc_ctx, nc_lat = n_ctx // CHUNK, n_lat // CHUNK
    nc = nc_ctx + nc_lat
    rows = bsz * ltot
    hb = CHUNK // 16
    n_hblk = rows // 16
    chunk_of = (lambda i: i, lambda i: _rwkv_mirror(i, nc_ctx, nc_lat))

    def specs(d):
        main = lambda b, i: (b * nc + chunk_of[d](i), 0)
        prev = lambda b, i: (jnp.maximum((b * nc + chunk_of[d](i)) * hb - 1, 0), 0)
        nxt = lambda b, i: (jnp.minimum((b * nc + chunk_of[d](i) + 1) * hb, n_hblk - 1), 0)
        return main, [pl.BlockSpec((CHUNK, RWKV_IN), main), pl.BlockSpec((16, RWKV_IN), prev),
                      pl.BlockSpec((16, RWKV_IN), nxt)]

    (main0, in0), (main1, in1) = specs(0), specs(1)
    whole = lambda shape: pl.BlockSpec(shape, lambda b, i: (0,) * len(shape))
    out_f = jax.ShapeDtypeStruct((rows, RWKV_W), F32)
    out_bf = jax.ShapeDtypeStruct((rows, RWKV_W), BF16)
    ospec = lambda m: pl.BlockSpec((CHUNK, RWKV_W), m)
    kern = functools.partial(_rwkv_kernel, nc_ctx=nc_ctx, nc_lat=nc_lat)
    return pl.pallas_call(
        kern,
        out_shape=(out_f, out_f, out_bf, out_bf, out_bf),
        grid=(bsz, nc),
        in_specs=in0 + in1 + [whole((2, RWKV_IN)), whole((2, 8, RWKV_W)),
                              whole((2, 2 * DECAY_LORA, RWKV_W)), whole((2, 2 * AAA_LORA, RWKV_W)),
                              whole((GATE_LORA, RWKV_W)), whole((2, 2, CHUNK, CHUNK))],
        out_specs=(ospec(main0), ospec(main1), ospec(main0), ospec(main1), ospec(main0)),
        scratch_shapes=[pltpu.VMEM((2, RWKV_HEADS // 2, RWKV_HEAD, 2 * RWKV_HEAD), F32)],
        compiler_params=_cparams(("arbitrary", "arbitrary")),
        name="rwkv_scan",
    )(p, p, p, p, p, p, mu, vecs, wup, aup, gup, masks)


def _diff_kernel(q_ref, k_ref, v_ref, par_ref, o_ref, *, n_ctx):
    t = pl.program_id(2)
    lam = par_ref[0:1, :]
    gain = par_ref[1:2, :]

    def attend(nk):
        q = q_ref[...]
        k = k_ref[0:nk, :]
        v = v_ref[0:nk, :]

        def softmax_v(sl):
            s = _dot_nt(q[:, sl], k[:, sl])
            e = jnp.exp2(s - jnp.max(s, axis=-1, keepdims=True))
            return _dot(e, v) / jnp.sum(e, axis=-1, keepdims=True)

        o = softmax_v(slice(0, DIFF_DK)) - lam * softmax_v(slice(DIFF_DK, 2 * DIFF_DK))
        o = o * lax.rsqrt(jnp.mean(o * o, axis=-1, keepdims=True) + 1e-5)
        o_ref[...] = (o * gain).astype(o_ref.dtype)

    @pl.when(t * SEG < n_ctx)
    def _():
        attend(n_ctx)

    @pl.when(t * SEG >= n_ctx)
    def _():
        attend(k_ref.shape[0])


def _diff_attn(qk, p, par, bsz, n_ctx, ltot):
    rows = bsz * ltot
    nt = ltot // SEG
    voff = (P_DIFF + 2 * DIFF_W) // DIFF_DV
    return pl.pallas_call(
        functools.partial(_diff_kernel, n_ctx=n_ctx),
        out_shape=jax.ShapeDtypeStruct((rows, DIFF_W), BF16),
        grid=(bsz, DIFF_HEADS, nt),
        in_specs=[pl.BlockSpec((SEG, DIFF_DV), lambda b, h, t: (b * nt + t, h)),
                  pl.BlockSpec((ltot, DIFF_DV), lambda b, h, t: (b, DIFF_HEADS + h)),
                  pl.BlockSpec((ltot, DIFF_DV), lambda b, h, t: (b, voff + h)),
                  pl.BlockSpec((8, DIFF_DV), lambda b, h, t: (0, 0))],
        out_specs=pl.BlockSpec((SEG, DIFF_DV), lambda b, h, t: (b * nt + t, h)),
        compiler_params=_cparams(("arbitrary", "arbitrary", "arbitrary")),
        name="diff_attn",
    )(qk, qk, p, par)


def _win_kernel(sink_ref, q_ref, kp_ref, kc_ref, kn_ref, kx_ref, vp_ref, vc_ref, vn_ref, vx_ref,
                o_ref, *, n_ctx, n_lat):
    blk = pl.program_id(1)
    ncb = n_ctx // QBLK
    nq = WIN_GROUP * QBLK

    def run(keys, vals, mask):
        outs = []
        for g in range(WIN_KV_HEADS):
            ksl = slice(g * WIN_HEAD, (g + 1) * WIN_HEAD)
            qg = jnp.concatenate(
                [q_ref[:, (g * WIN_GROUP + j) * WIN_HEAD:(g * WIN_GROUP + j + 1) * WIN_HEAD]
                 for j in range(WIN_GROUP)], axis=0)
            s = _dot_nt(qg, keys[:, ksl])
            if mask is not None:
                s = jnp.where(mask, s, NEG_INF)
            hrow = lax.broadcasted_iota(jnp.int32, (nq, 1), 0) // QBLK
            sink = jnp.zeros((nq, 1), F32)
            for j in range(WIN_GROUP):
                sink = jnp.where(hrow == j, sink_ref[g * WIN_GROUP + j] * LOG2E, sink)
            m = jnp.maximum(jnp.max(s, axis=-1, keepdims=True), sink)
            e = jnp.exp2(s - m)
            den = jnp.sum(e, axis=-1, keepdims=True) + jnp.exp2(sink - m)
            og = _dot(e, vals[:, ksl]) / den
            outs.extend(og[j * QBLK:(j + 1) * QBLK] for j in range(WIN_GROUP))
        o_ref[...] = jnp.concatenate(outs, axis=1).astype(o_ref.dtype)

    @pl.when(blk < ncb)
    def _():
        run(kx_ref[...], vx_ref[...], None)

    @pl.when(blk >= ncb)
    def _():
        j = blk - ncb
        keys = jnp.concatenate([kp_ref[...], kc_ref[...], kn_ref[...], kx_ref[...]], axis=0)
        vals = jnp.concatenate([vp_ref[...], vc_ref[...], vn_ref[...], vx_ref[...]], axis=0)
        nk = 3 * QBLK + n_ctx
        qpos = j * QBLK + lax.broadcasted_iota(jnp.int32, (nq, nk), 0) % QBLK
        col = lax.broadcasted_iota(jnp.int32, (nq, nk), 1)
        kpos = (j - 1) * QBLK + col
        band = (jnp.abs(qpos - kpos) <= WINDOW) & (kpos >= 0) & (kpos < n_lat)
        run(keys, vals, band | (col >= 3 * QBLK))


def _win_attn(sink, qk, p, bsz, n_ctx, n_lat):
    ltot = n_ctx + n_lat
    rows = bsz * ltot
    nb = ltot // QBLK
    ncb = n_ctx // QBLK
    koff = WIN_W // WIN_KV_W
    voff = (P_WIN + WIN_W + WIN_KV_W) // WIN_KV_W

    def lat_blk(b, t, shift):
        j = jnp.clip(t - ncb + shift, 0, nb - ncb - 1)
        return b * nb + ncb + j

    kspec = lambda shift, c: pl.BlockSpec((QBLK, WIN_KV_W), lambda b, t, s: (lat_blk(b, t, shift), c))
    xspec = lambda c: pl.BlockSpec((n_ctx, WIN_KV_W), lambda b, t, s: (b * (ltot // n_ctx), c))
    return pl.pallas_call(
        functools.partial(_win_kernel, n_ctx=n_ctx, n_lat=n_lat),
        out_shape=jax.ShapeDtypeStruct((rows, WIN_W), BF16),
        grid_spec=pltpu.PrefetchScalarGridSpec(
            num_scalar_prefetch=1,
            grid=(bsz, nb),
            in_specs=[pl.BlockSpec((QBLK, WIN_W), lambda b, t, s: (b * nb + t, 0)),
                      kspec(-1, koff), kspec(0, koff), kspec(1, koff), xspec(koff),
                      kspec(-1, voff), kspec(0, voff), kspec(1, voff), xspec(voff)],
            out_specs=pl.BlockSpec((QBLK, WIN_W), lambda b, t, s: (b * nb + t, 0))),
        compiler_params=_cparams(("arbitrary", "arbitrary")),
        name="win_attn",
    )(sink, qk, qk, qk, qk, qk, p, p, p, p)


def _merge_kernel(ya_ref, yb_ref, yc_ref, ga_ref, gb_ref, gc_ref, wa_ref, wb_ref, wc_ref, o_ref):
    z = _sigmoid(ga_ref[...].astype(F32)) * _dot(ya_ref[...], wa_ref[...])
    z = z + _sigmoid(gb_ref[...].astype(F32)) * _dot(yb_ref[...], wb_ref[...])
    z = z + _sigmoid(gc_ref[...].astype(F32)) * _dot(yc_ref[...], wc_ref[...])
    o_ref[...] = z.astype(o_ref.dtype)


def _merge(ya, yb, yc, p, w_branch, layer, tm, tn):
    m = ya.shape[0]
    assert RWKV_W == DIFF_W and (RWKV_W + DIFF_W) % WIN_W == 0
    nj = D_MODEL // tn
    goff = P_GATE // tn
    gspec = lambda br: pl.BlockSpec((tm, tn), lambda j, i: (i, goff + br * nj + j))
    return pl.pallas_call(
        _merge_kernel,
        out_shape=jax.ShapeDtypeStruct((m, D_MODEL), BF16),
        grid=(nj, m // tm),
        in_specs=[pl.BlockSpec((tm, RWKV_W), lambda j, i: (i, 0)),
                  pl.BlockSpec((tm, DIFF_W), lambda j, i: (i, 0)),
                  pl.BlockSpec((tm, WIN_W), lambda j, i: (i, 0)),
                  gspec(0), gspec(1), gspec(2),
                  pl.BlockSpec((None, RWKV_W, tn), lambda j, i: (layer, 0, j)),
                  pl.BlockSpec((None, DIFF_W, tn), lambda j, i: (layer, 1, j)),
                  pl.BlockSpec((None, WIN_W, tn), lambda j, i: (layer, (RWKV_W + DIFF_W) // WIN_W, j))],
        out_specs=pl.BlockSpec((tm, tn), lambda j, i: (i, j)),
        compiler_params=_cparams(("arbitrary", "arbitrary")),
        name="merge",
    )(ya, yb, yc, p, p, p, w_branch, w_branch, w_branch)


def _expert_kernel(be_ref, nu_ref, x_ref, gw_ref, w1_ref, w3_ref, w2_ref, o_ref, w1b, w3b, w2b):
    i = pl.program_id(0)
    prev = be_ref[jnp.maximum(i - 1, 0)]

    @pl.when(jnp.logical_or(i == 0, be_ref[i] != prev))
    def _():
        w1b[...] = w1_ref[...].astype(BF16)
        w3b[...] = w3_ref[...].astype(BF16)
        w2b[...] = w2_ref[...].astype(BF16)

    @pl.when(i < nu_ref[0])
    def _():
        x = x_ref[...]
        h1 = _dot(x, w1b[...])
        h = h1 * _sigmoid(h1) * _dot(x, w3b[...])
        o_ref[...] = (_dot(h, w2b[...]) * gw_ref[...]).astype(o_ref.dtype)

    @pl.when(i >= nu_ref[0])
    def _():
        o_ref[...] = jnp.zeros_like(o_ref)


def _experts(blk_expert, n_used, xb, gw, w1, w3, w2, layer):
    rows = xb.shape[0]
    nb = rows // MOE_BLK
    return pl.pallas_call(
        _expert_kernel,
        out_shape=jax.ShapeDtypeStruct((rows, D_MODEL), BF16),
        grid_spec=pltpu.PrefetchScalarGridSpec(
            num_scalar_prefetch=2,
            grid=(nb,),
            in_specs=[pl.BlockSpec((MOE_BLK, D_MODEL), lambda i, be, nu: (i, 0)),
                      pl.BlockSpec((MOE_BLK, 1), lambda i, be, nu: (i, 0)),
                      pl.BlockSpec((None, None, D_MODEL, D_EXPERT), lambda i, be, nu: (layer, be[i], 0, 0)),
                      pl.BlockSpec((None, None, D_MODEL, D_EXPERT), lambda i, be, nu: (layer, be[i], 0, 0)),
                      pl.BlockSpec((None, None, D_EXPERT, D_MODEL), lambda i, be, nu: (layer, be[i], 0, 0))],
            out_specs=pl.BlockSpec((MOE_BLK, D_MODEL), lambda i, be, nu: (i, 0)),
            scratch_shapes=[pltpu.VMEM((D_MODEL, D_EXPERT), BF16),
                            pltpu.VMEM((D_MODEL, D_EXPERT), BF16),
                            pltpu.VMEM((D_EXPERT, D_MODEL), BF16)]),
        compiler_params=_cparams(("arbitrary",)),
        name="experts",
    )(blk_expert, n_used, xb, gw, w1, w3, w2)


def _rope_tables(n_ctx, n_lat):
    rows = n_lat // GRID_W
    row = jnp.repeat(jnp.arange(rows), GRID_W).astype(F32)
    col = (jnp.arange(rows * GRID_W) % GRID_W).astype(F32)
    inv = ROPE_BASE ** (-jnp.arange(ROPE_AX_FREQS, dtype=F32) / ROPE_AX_FREQS)
    ang = jnp.stack([row[:, None] * inv, col[:, None] * inv], axis=1)
    cos, sin = jnp.cos(ang), jnp.sin(ang)
    cos4 = jnp.stack([cos, cos], axis=2).reshape(n_lat, 4 * ROPE_AX_FREQS)
    sin4 = jnp.stack([-sin, sin], axis=2).reshape(n_lat, 4 * ROPE_AX_FREQS)
    cos4 = jnp.concatenate([jnp.ones((n_ctx, 64), F32), cos4], axis=0)
    sin4 = jnp.concatenate([jnp.zeros((n_ctx, 64), F32), sin4], axis=0)
    return jnp.stack([jnp.tile(cos4, (1, LANE // 64)), jnp.tile(sin4, (1, LANE // 64))])


def _moe(route, v, w1, w3, w2, layer):
    t = v.shape[0]
    experts = route[:, :EXPERT_TOP_K].astype(jnp.int32)
    gates = route[:, EXPERT_TOP_K:2 * EXPERT_TOP_K]
    k = EXPERT_TOP_K
    a = t * k
    e_n = N_EXPERTS
    nb = -(-a // MOE_BLK) + e_n
    e_flat = experts.T.reshape(a)
    iota = jnp.arange(a, dtype=jnp.int32)
    e_s, order, g_s = lax.sort((e_flat, iota, gates.T.reshape(a)), num_keys=1, is_stable=True)
    ids = jnp.arange(e_n, dtype=jnp.int32)
    start = jnp.sum(e_s[None, :] < ids[:, None], axis=1, dtype=jnp.int32)
    counts = jnp.sum(e_s[None, :] == ids[:, None], axis=1, dtype=jnp.int32)
    padded = (counts + MOE_BLK - 1) // MOE_BLK * MOE_BLK
    pad_end = jnp.cumsum(padded)
    pad_start = pad_end - padded
    shift = jnp.sum(jnp.where(e_s[:, None] == ids[None, :], (pad_start - start)[None, :], 0), axis=1)
    pos = lax.sort((order, iota + shift), num_keys=1)[1]
    blk_first = jnp.arange(nb, dtype=jnp.int32) * MOE_BLK
    blk_expert = jnp.minimum(jnp.sum(pad_end[None, :] <= blk_first[:, None], axis=1, dtype=jnp.int32), e_n - 1)
    n_used = (pad_end[-1:] // MOE_BLK).astype(jnp.int32)
    rank = blk_first[:, None] + jnp.arange(MOE_BLK, dtype=jnp.int32)[None, :] - pad_start[blk_expert][:, None]
    filled = (rank < counts[blk_expert][:, None]).reshape(nb * MOE_BLK)
    sidx = jnp.clip(start[blk_expert][:, None] + rank, 0, a - 1).reshape(nb * MOE_BLK)
    spread = jnp.arange(nb * MOE_BLK, dtype=jnp.int32) % t
    slot_tok = jnp.where(filled, order.at[sidx].get(mode="promise_in_bounds") % t, spread)
    slot_gate = jnp.where(filled, g_s.at[sidx].get(mode="promise_in_bounds"), 0.0)
    xb = v.at[slot_tok].get(mode="promise_in_bounds")
    ys = _experts(blk_expert, n_used, xb, slot_gate[:, None], w1, w3, w2, layer)
    yg = lax.optimization_barrier(ys.at[pos].get(mode="promise_in_bounds"))
    return yg.reshape(k, t, D_MODEL)


def kernel(x, c, ctx, c_ctx, w_mod, b_mod, w_in, rwkv_mu, rwkv_w0, rwkv_w_up, rwkv_a0, rwkv_a_up,
           rwkv_g_up, rwkv_kvec, rwkv_lnx, diff_lam, diff_subln, win_sink, w_branch, w_out, ln_g, ln_b,
           w_rg, b_rg, w_re, b_re, w1, w3, w2):
    bsz, n_lat, dm = x.shape
    n_ctx = ctx.shape[1]
    depth = w_mod.shape[0]
    ltot = n_ctx + n_lat
    rows = bsz * ltot
    dn_alpha = (2 * depth) ** 0.25
    assert dm == D_MODEL and n_ctx % SEG == 0 and n_lat % SEG == 0 and ltot % n_ctx == 0

    nt = ltot // SEG
    cs = _rope_tables(n_ctx, n_lat)
    fwd = jnp.tril(jnp.ones((CHUNK, CHUNK), F32))
    masks = jnp.stack([jnp.stack([fwd - jnp.eye(CHUNK, dtype=F32), fwd]),
                       jnp.stack([fwd.T - jnp.eye(CHUNK, dtype=F32), fwd.T])])
    head_id = jnp.arange(RWKV_W) // RWKV_HEAD
    head_avg = ((head_id[:, None] == head_id[None, :]).astype(F32) / RWKV_HEAD).astype(BF16)

    xs = jnp.concatenate([ctx, x], axis=1).reshape(rows, dm)
    cvec = jnp.concatenate([c_ctx[None, :], c], axis=0)
    cpad = jnp.zeros((32, dm), F32).at[:bsz + 1].set(jax.nn.silu(cvec))
    w_in_p = jnp.concatenate([w_in[:, :, :RWKV_IN], jnp.zeros((depth, dm, P_GATE - RWKV_IN), F32),
                              w_in[:, :, GATE_OFF:], w_in[:, :, DIFF_OFF:GATE_OFF]], axis=2).astype(BF16)
    w_branch_b = w_branch.astype(BF16)
    w_out_b = w_out.astype(BF16)
    w_r = jnp.zeros((depth, dm, LANE), F32).at[:, :, :N_GROUPS].set(w_rg)
    w_r = w_r.at[:, :, N_GROUPS:N_GROUPS + N_EXPERTS].set(w_re)
    b_r = jnp.zeros((depth, 8, LANE), F32).at[:, :, :N_GROUPS].set(b_rg[:, None, :])
    b_r = b_r.at[:, :, N_GROUPS:N_GROUPS + N_EXPERTS].set(b_re[:, None, :])

    mods = [_mm(cpad, w_mod, i, F32, 32, 1024, "mod")[:bsz + 1] + b_mod[i] for i in range(depth)]

    def table(gate, shift, scale):
        def both(m, j):
            v = m[:, j * dm:(j + 1) * dm]
            return jnp.stack([jnp.broadcast_to(v[0], (bsz, dm)), v[1:]], axis=1)
        t = jnp.stack([both(*gate), both(*shift), both(*scale)], axis=2)
        return jnp.pad(t, ((0, 0), (0, 0), (0, 5), (0, 0))).reshape(2 * bsz, 8, dm)

    u = _modulate0(xs, table((mods[0], 0), (mods[0], 0), (mods[0], 1)), nt)
    for i in range(depth):
        last = i == depth - 1
        lam_init = 0.8 - 0.6 * math.exp(-0.3 * i)
        mod = mods[i]
        p = _mm(u, w_in_p, i, BF16, 512, 2048, "in_proj")

        vecs = jnp.zeros((2, 8, RWKV_W), F32)
        vecs = vecs.at[:, 0].set(rwkv_w0[i]).at[:, 1].set(rwkv_a0[i])
        vecs = vecs.at[:, 2:5].set(jnp.broadcast_to(rwkv_kvec[i][None], (2, 3, RWKV_W)))
        wup = jnp.zeros((2, 2 * DECAY_LORA, RWKV_W), BF16)
        aup = jnp.zeros((2, 2 * AAA_LORA, RWKV_W), BF16)
        for d in range(2):
            wup = wup.at[d, d * DECAY_LORA:(d + 1) * DECAY_LORA].set(rwkv_w_up[i, d].astype(BF16))
            aup = aup.at[d, d * AAA_LORA:(d + 1) * AAA_LORA].set(rwkv_a_up[i, d].astype(BF16))
        scan = _rwkv_scan(p, rwkv_mu[i], vecs, wup, aup, rwkv_g_up[i].astype(BF16), masks, bsz, n_ctx, n_lat)
        ya = _readout(*scan, rwkv_lnx[i], head_avg, 512)

        dqk, wqk = _rope(p, cs, nt)
        lf = diff_lam[i]
        lam = jnp.exp(jnp.sum(lf[0] * lf[1])) - jnp.exp(jnp.sum(lf[2] * lf[3])) + lam_init
        par = jnp.zeros((8, DIFF_DV), F32).at[0].set(lam).at[1].set(diff_subln[i] * (1 - lam_init))
        yb = _diff_attn(dqk, p, par, bsz, n_ctx, ltot)
        yc = _win_attn(win_sink[i], wqk, p, bsz, n_ctx, n_lat)
        z = _merge(ya, yb, yc, p, w_branch_b, i, 512, 1024)

        lnp = lambda j: jnp.zeros((8, dm), F32).at[0].set(ln_g[i, j]).at[1].set(ln_b[i, j])
        xs, v, route = _mix_out(z, xs, w_out_b, i, table((mod, 2), (mod, 3), (mod, 4)), lnp(0),
                                w_r[i], b_r[i], nt, dn_alpha)

        if last:
            lat = lambda t: t.reshape(bsz, ltot, -1)[:, n_ctx:].reshape(bsz * n_lat, -1)
            yg = _moe(lat(route), lat(v), w1, w3, w2, i)
            tab = table((mod, 5), (mod, 0), (mod, 1))
            (out,) = _moe_out(xs, yg, tab, lnp(1), bsz, nt, n_ctx // SEG, dn_alpha, False)
            return out.reshape(bsz, n_lat, dm)
        yg = _moe(route, v, w1, w3, w2, i)
        tab = table((mod, 5), (mods[i + 1], 0), (mods[i + 1], 1))
        xs, u = _moe_out(xs, yg, tab, lnp(1), bsz, nt, 0, dn_alpha, True)
    return None
```

```python
import functools
import math

import jax
import jax.numpy as jnp
from jax import lax
from jax.experimental import pallas as pl
from jax.experimental.pallas import tpu as pltpu

F32 = jnp.float32
BF16 = jnp.bfloat16

D_MODEL = 2048
GRID_W = 64
RWKV_HEADS = 12
RWKV_HEAD = 64
RWKV_W = RWKV_HEADS * RWKV_HEAD
DECAY_LORA = 64
AAA_LORA = 64
GATE_LORA = 128
RWKV_GN_EPS = 64e-5
DIFF_HEADS = 6
DIFF_DK = 64
DIFF_DV = 2 * DIFF_DK
DIFF_W = DIFF_HEADS * DIFF_DV
DIFF_SCALE = DIFF_DK ** -0.5
WIN_Q_HEADS = 8
WIN_KV_HEADS = 2
WIN_GROUP = WIN_Q_HEADS // WIN_KV_HEADS
WIN_HEAD = 64
WIN_W = WIN_Q_HEADS * WIN_HEAD
WIN_KV_W = WIN_KV_HEADS * WIN_HEAD
WIN_SCALE = WIN_HEAD ** -0.5
WINDOW = 128
QBLK = WINDOW
MIX_W = RWKV_W + DIFF_W + WIN_W
N_BRANCH = 3
ROPE_BASE = 10000.0
ROPE_AX_FREQS = 16
RWKV_IN = 3 * RWKV_W + 2 * DECAY_LORA + 2 * AAA_LORA + GATE_LORA
DIFF_IN = 3 * DIFF_W
WIN_IN = WIN_W + 2 * WIN_KV_W
DIFF_OFF = RWKV_IN
WIN_OFF = DIFF_OFF + DIFF_IN
GATE_OFF = WIN_OFF + WIN_IN
N_IN = GATE_OFF + N_BRANCH * D_MODEL
N_GROUPS = 4
EXPERTS_PER_GROUP = 8
N_EXPERTS = N_GROUPS * EXPERTS_PER_GROUP
EXPERT_TOP_K = 2
D_EXPERT = D_MODEL // 4
MOE_BLK = 256
ADA_EPS = 1e-6
LN_EPS = 1e-5
NEG_INF = -1e30
LOG2E = math.log2(math.e)

LANE = 128
SEG = 256
P_RWKV = 0
P_GATE = 3072
P_DIFF = P_GATE + N_BRANCH * D_MODEL
P_WIN = P_DIFF + DIFF_IN
P_COLS = P_WIN + WIN_IN
CHUNK = 64
VMEM_LIMIT = 56 * 1024 * 1024


def _cparams(sem):
    return pltpu.CompilerParams(dimension_semantics=sem, vmem_limit_bytes=VMEM_LIMIT)


def _dot(a, b):
    return jnp.dot(a.astype(BF16), b.astype(BF16), preferred_element_type=F32)


def _dot_nt(a, b):
    return lax.dot_general(a.astype(BF16), b.astype(BF16), (((1,), (1,)), ((), ())),
                           preferred_element_type=F32)


def _dot_tn(a, b):
    return lax.dot_general(a.astype(BF16), b.astype(BF16), (((0,), (0,)), ((), ())),
                           preferred_element_type=F32)


def _split(x):
    hi = x.astype(BF16)
    lo = (x - hi.astype(F32)).astype(BF16)
    return hi, lo


def _dot3(a, b):
    ah, al = _split(a)
    bh, bl = _split(b)
    d = functools.partial(jnp.dot, preferred_element_type=F32)
    return d(ah, bh) + (d(ah, bl) + d(al, bh))


def _sigmoid(x):
    return 1.0 / (1.0 + jnp.exp(-x))


def _mm_kernel(a_ref, w_ref, o_ref):
    o_ref[...] = _dot(a_ref[...], w_ref[...]).astype(o_ref.dtype)


def _mm(a, w, layer, out_dtype, tm, tn, name):
    m, k = a.shape
    n = w.shape[2]
    return pl.pallas_call(
        _mm_kernel,
        out_shape=jax.ShapeDtypeStruct((m, n), out_dtype),
        grid=(n // tn, m // tm),
        in_specs=[pl.BlockSpec((tm, k), lambda j, i: (i, 0)),
                  pl.BlockSpec((None, k, tn), lambda j, i: (layer, 0, j))],
        out_specs=pl.BlockSpec((tm, tn), lambda j, i: (i, j)),
        compiler_params=_cparams(("arbitrary", "arbitrary")),
        name=name,
    )(a, w)


def _norm_rows(x, eps):
    mu = jnp.mean(x, axis=-1, keepdims=True)
    xc = x - mu
    return xc * lax.rsqrt(jnp.mean(xc * xc, axis=-1, keepdims=True) + eps)


def _route_rows(logits):
    col = lax.broadcasted_iota(jnp.int32, logits.shape, 1).astype(F32)
    big = float(LANE)
    is_g = col < N_GROUPS
    lg = jnp.where(is_g, logits, NEG_INF)
    g_max = jnp.max(lg, axis=-1, keepdims=True)
    g_sum = jnp.sum(jnp.where(is_g, jnp.exp(lg - g_max), 0.0), axis=-1, keepdims=True)
    pg_top = 1.0 / g_sum
    g_idx = jnp.min(jnp.where(is_g & (lg == g_max), col, big), axis=-1, keepdims=True)
    lo = N_GROUPS + EXPERTS_PER_GROUP * g_idx
    sel = (col >= lo) & (col < lo + EXPERTS_PER_GROUP)
    le = jnp.where(sel, logits, NEG_INF)
    m1 = jnp.max(le, axis=-1, keepdims=True)
    den = jnp.sum(jnp.where(sel, jnp.exp(le - m1), 0.0), axis=-1, keepdims=True)
    i1 = jnp.min(jnp.where(sel & (le == m1), col, big), axis=-1, keepdims=True)
    rest = sel & (col != i1)
    le2 = jnp.where(rest, logits, NEG_INF)
    m2 = jnp.max(le2, axis=-1, keepdims=True)
    i2 = jnp.min(jnp.where(rest & (le2 == m2), col, big), axis=-1, keepdims=True)
    p1 = 1.0 / den
    p2 = jnp.exp(m2 - m1) / den
    tot = p1 + p2
    out = jnp.where(col == 0.0, i1 - N_GROUPS, 0.0)
    out = jnp.where(col == 1.0, i2 - N_GROUPS, out)
    out = jnp.where(col == 2.0, pg_top * p1 / tot, out)
    return jnp.where(col == 3.0, pg_top * p2 / tot, out)


def _mod_kernel(x_ref, tab_ref, u_ref):
    u_ref[...] = (_norm_rows(x_ref[...], ADA_EPS) * (1.0 + tab_ref[0, 2:3, :]) + tab_ref[0, 1:2, :]).astype(u_ref.dtype)


def _modulate0(x, tab, nt):
    rows, dm = x.shape
    return pl.pallas_call(
        _mod_kernel,
        out_shape=jax.ShapeDtypeStruct((rows, dm), BF16),
        grid=(rows // SEG,),
        in_specs=[pl.BlockSpec((SEG, dm), lambda i: (i, 0)),
                  pl.BlockSpec((1, 8, dm), lambda i: (2 * (i // nt) + jnp.minimum(i % nt, 1), 0, 0))],
        out_specs=pl.BlockSpec((SEG, dm), lambda i: (i, 0)),
        compiler_params=_cparams(("arbitrary",)),
        name="modulate0",
    )(x, tab)


def _mix_out_kernel(z_ref, x_ref, w_ref, tab_ref, lnp_ref, wr_ref, br_ref, xo_ref, vo_ref, ro_ref, *, alpha):
    m = _dot(z_ref[...], w_ref[...])
    xn = _norm_rows(alpha * x_ref[...] + tab_ref[0, 0:1, :] * m, LN_EPS) * lnp_ref[0:1, :] + lnp_ref[1:2, :]
    xo_ref[...] = xn
    v = _norm_rows(xn, ADA_EPS) * (1.0 + tab_ref[0, 2:3, :]) + tab_ref[0, 1:2, :]
    vo_ref[...] = v.astype(vo_ref.dtype)
    ro_ref[...] = _route_rows(_dot3(v, wr_ref[...]) + br_ref[0:1, :])


def _mix_out(z, x, w_out, layer, tab, lnp, w_r, b_r, nt, alpha):
    rows, dm = x.shape
    row = lambda i: (i, 0)
    const = lambda i: (0, 0)
    return pl.pallas_call(
        functools.partial(_mix_out_kernel, alpha=alpha),
        out_shape=(jax.ShapeDtypeStruct((rows, dm), F32), jax.ShapeDtypeStruct((rows, dm), BF16),
                   jax.ShapeDtypeStruct((rows, LANE), F32)),
        grid=(rows // SEG,),
        in_specs=[pl.BlockSpec((SEG, dm), row), pl.BlockSpec((SEG, dm), row),
                  pl.BlockSpec((None, dm, dm), lambda i: (layer, 0, 0)),
                  pl.BlockSpec((1, 8, dm), lambda i: (2 * (i // nt) + jnp.minimum(i % nt, 1), 0, 0)),
                  pl.BlockSpec((8, dm), const), pl.BlockSpec((dm, LANE), const), pl.BlockSpec((8, LANE), const)],
        out_specs=(pl.BlockSpec((SEG, dm), row), pl.BlockSpec((SEG, dm), row), pl.BlockSpec((SEG, LANE), row)),
        compiler_params=_cparams(("arbitrary",)),
        name="mix_out",
    )(z, x, w_out, tab, lnp, w_r, b_r)


def _moe_out_kernel(x_ref, y0_ref, y1_ref, tab_ref, lnp_ref, xo_ref, *u_ref, alpha):
    y = y0_ref[0].astype(F32) + y1_ref[0].astype(F32)
    xn = _norm_rows(alpha * x_ref[...] + tab_ref[0, 0:1, :] * y, LN_EPS) * lnp_ref[0:1, :] + lnp_ref[1:2, :]
    xo_ref[...] = xn
    if u_ref:
        u_ref[0][...] = (_norm_rows(xn, ADA_EPS) * (1.0 + tab_ref[0, 2:3, :]) + tab_ref[0, 1:2, :]).astype(BF16)


def _moe_out(x, yg, tab, lnp, bsz, nt, skip, alpha, emit_u):
    dm = x.shape[1]
    nk = nt - skip
    rows = bsz * nk * SEG
    xmap = lambda b, t: (b * nt + skip + t, 0)
    omap = lambda b, t: (b * nk + t, 0)
    out_shape = [jax.ShapeDtypeStruct((rows, dm), F32)]
    out_specs = [pl.BlockSpec((SEG, dm), omap)]
    if emit_u:
        out_shape.append(jax.ShapeDtypeStruct((rows, dm), BF16))
        out_specs.append(pl.BlockSpec((SEG, dm), omap))
    return pl.pallas_call(
        functools.partial(_moe_out_kernel, alpha=alpha),
        out_shape=tuple(out_shape),
        grid=(bsz, nk),
        in_specs=[pl.BlockSpec((SEG, dm), xmap),
                  pl.BlockSpec((1, SEG, dm), lambda b, t: (0, b * nk + t, 0)),
                  pl.BlockSpec((1, SEG, dm), lambda b, t: (1, b * nk + t, 0)),
                  pl.BlockSpec((1, 8, dm), lambda b, t: (2 * b + jnp.minimum(skip + t, 1), 0, 0)),
                  pl.BlockSpec((8, dm), lambda b, t: (0, 0))],
        out_specs=tuple(out_specs),
        compiler_params=_cparams(("arbitrary", "arbitrary")),
        name="moe_out",
    )(x, yg, yg, tab, lnp)


def _rope_kernel(pd_ref, pw_ref, cs_ref, dqk_ref, wqk_ref):
    cos = cs_ref[0]
    sin = cs_ref[1]
    first_half = lax.broadcasted_iota(jnp.int32, cos.shape, 1) % (2 * ROPE_AX_FREQS) < ROPE_AX_FREQS

    def rot(x, scale):
        x = x.astype(F32)
        sw = jnp.where(first_half, pltpu.roll(x, LANE - ROPE_AX_FREQS, axis=1), pltpu.roll(x, ROPE_AX_FREQS, axis=1))
        return ((x * cos + sw * sin) * scale).astype(BF16)

    for c in range(2 * DIFF_W // LANE):
        scale = DIFF_SCALE * LOG2E if c < DIFF_W // LANE else 1.0
        dqk_ref[:, c * LANE:(c + 1) * LANE] = rot(pd_ref[:, c * LANE:(c + 1) * LANE], scale)
    for c in range((WIN_W + WIN_KV_W) // LANE):
        scale = WIN_SCALE * LOG2E if c < WIN_W // LANE else 1.0
        wqk_ref[:, c * LANE:(c + 1) * LANE] = rot(pw_ref[:, c * LANE:(c + 1) * LANE], scale)


def _rope(p, cs, nt):
    rows = p.shape[0]
    wd, ww = 2 * DIFF_W, WIN_W + WIN_KV_W
    return pl.pallas_call(
        _rope_kernel,
        out_shape=(jax.ShapeDtypeStruct((rows, wd), BF16), jax.ShapeDtypeStruct((rows, ww), BF16)),
        grid=(rows // SEG,),
        in_specs=[pl.BlockSpec((SEG, wd), lambda i: (i, P_DIFF // wd)),
                  pl.BlockSpec((SEG, ww), lambda i: (i, P_WIN // ww)),
                  pl.BlockSpec((2, SEG, LANE), lambda i: (0, i % nt, 0))],
        out_specs=(pl.BlockSpec((SEG, wd), lambda i: (i, 0)), pl.BlockSpec((SEG, ww), lambda i: (i, 0))),
        compiler_params=_cparams(("arbitrary",)),
        name="rope",
    )(p, p, cs)


def _readout_kernel(y0_ref, y1_ref, bv0_ref, bv1_ref, g_ref, lnx_ref, avg_ref, o_ref):
    avg = avg_ref[...]

    def head_mean(t):
        hi, lo = _split(t)
        return jnp.dot(hi, avg, preferred_element_type=F32) + jnp.dot(lo, avg, preferred_element_type=F32)

    y = y0_ref[...] + y1_ref[...]
    dev = y - head_mean(y)
    yn = dev * lax.rsqrt(head_mean(dev * dev) + RWKV_GN_EPS) * lnx_ref[0:1, :] + lnx_ref[1:2, :]
    bonus = bv0_ref[...].astype(F32) + bv1_ref[...].astype(F32)
    o_ref[...] = ((yn + bonus) * g_ref[...].astype(F32)).astype(o_ref.dtype)


def _readout(y0, y1, bv0, bv1, g, lnx, avg, tm):
    rows = y0.shape[0]
    row = pl.BlockSpec((tm, RWKV_W), lambda i: (i, 0))
    return pl.pallas_call(
        _readout_kernel,
        out_shape=jax.ShapeDtypeStruct((rows, RWKV_W), BF16),
        grid=(rows // tm,),
        in_specs=[row, row, row, row, row,
                  pl.BlockSpec((2, RWKV_W), lambda i: (0, 0)),
                  pl.BlockSpec((RWKV_W, RWKV_W), lambda i: (0, 0))],
        out_specs=row,
        compiler_params=_cparams(("arbitrary",)),
        name="rwkv_readout",
    )(y0, y1, bv0, bv1, g, lnx, avg)


def _rwkv_features(d, chunk, p_ref, hp_ref, hn_ref, mu_ref, vec_ref, wup_ref, aup_ref, gup_ref, msk_ref,
                   bv_ref, g_ref, nc_ctx, nc_lat):
    n = CHUNK
    p = p_ref[...].astype(F32)
    first = jnp.logical_or(chunk == 0, chunk == nc_ctx)
    last = jnp.logical_or(chunk == nc_ctx - 1, chunk == nc_ctx + nc_lat - 1)
    hp = jnp.where(first, 0.0, hp_ref[15:16, :].astype(F32))
    hn = jnp.where(last, 0.0, hn_ref[0:1, :].astype(F32))
    row = lax.broadcasted_iota(jnp.int32, (n, 1), 0)
    prev = jnp.where(row == 0, hp, pltpu.roll(p, 1, axis=0))
    nxt = jnp.where(row == n - 1, hn, pltpu.roll(p, n - 1, axis=0))
    ps = p + mu_ref[0:1, :] * (prev - p) + mu_ref[1:2, :] * (nxt - p)

    r = ps[:, 0:RWKV_W]
    k = ps[:, RWKV_W:2 * RWKV_W]
    v = ps[:, 2 * RWKV_W:3 * RWKV_W]
    o = 3 * RWKV_W
    wd = jnp.tanh(ps[:, o:o + 2 * DECAY_LORA])
    ad = ps[:, o + 2 * DECAY_LORA:o + 2 * DECAY_LORA + 2 * AAA_LORA]
    gd = ps[:, o + 2 * DECAY_LORA + 2 * AAA_LORA:]
    w0 = vec_ref[d, 0:1, :]
    a0 = vec_ref[d, 1:2, :]
    k_k = vec_ref[d, 2:3, :]
    k_a = vec_ref[d, 3:4, :]
    r_k = vec_ref[d, 4:5, :]
    w_log = w0 + _dot(wd, wup_ref[d])
    a = _sigmoid(a0 + _dot(ad, aup_ref[d]))
    if g_ref is not None:
        g_ref[...] = _dot(_sigmoid(gd), gup_ref[...]).astype(g_ref.dtype)
    logw = -math.exp(-0.5) * _sigmoid(w_log)

    incl_f = msk_ref[d, 1]
    lw_hi, lw_lo = _split(logw)
    incl_b = incl_f.astype(BF16)
    cl = (jnp.dot(incl_b, lw_hi, preferred_element_type=F32)
          + jnp.dot(incl_b, lw_lo, preferred_element_type=F32))
    tot = jnp.sum(logw, axis=0, keepdims=True)
    e_in = jnp.exp(cl)
    e_ex = jnp.exp(cl - logw)
    e_inv = jnp.exp(-cl)
    e_end = jnp.exp(tot - cl)
    p_all = jnp.exp(tot)

    sls = [slice(h * RWKV_HEAD, (h + 1) * RWKV_HEAD) for h in range(RWKV_HEADS)]
    kk_n = k * k_k
    kk_sq = kk_n * kk_n
    kd_all = k * (1.0 + (a - 1.0) * k_a)
    rkd = r * kd_all * r_k
    inv_norm = [1.0 / jnp.maximum(jnp.sqrt(jnp.sum(kk_sq[:, sl], axis=-1, keepdims=True)), 1e-12) for sl in sls]
    bonus = [jnp.sum(rkd[:, sl], axis=-1, keepdims=True) for sl in sls]
    bv_ref[...] = jnp.concatenate([bonus[h] * v[:, sl] for h, sl in enumerate(sls)], axis=1).astype(bv_ref.dtype)
    kk_all = jnp.concatenate([kk_n[:, sl] * inv_norm[h] for h, sl in enumerate(sls)], axis=1)
    bd_all = kk_all * a
    return dict(
        v=v, strict_f=msk_ref[d, 0], incl_f=incl_f, p_all=p_all,
        kk_t=kk_all * e_ex,
        r_t=r * e_in,
        b_i=bd_all * e_inv, k_i=kd_all * e_inv,
        k_e=kd_all * e_end,
        b_e=bd_all * e_end)


def _rwkv_kernel(p0_ref, hp0_ref, hn0_ref, p1_ref, hp1_ref, hn1_ref, mu_ref, vec_ref, wup_ref, aup_ref,
                 gup_ref, msk_ref, y0_ref, y1_ref, bv0_ref, bv1_ref, g_ref, state, *, nc_ctx, nc_lat):
    i = pl.program_id(1)

    @pl.when(i == 0)
    def _():
        state[...] = jnp.zeros_like(state)

    n = CHUNK
    shared = (mu_ref, vec_ref, wup_ref, aup_ref, gup_ref, msk_ref)
    f = [_rwkv_features(0, i, p0_ref, hp0_ref, hn0_ref, *shared, bv0_ref, g_ref, nc_ctx, nc_lat),
         _rwkv_features(1, _rwkv_mirror(i, nc_ctx, nc_lat), p1_ref, hp1_ref, hn1_ref, *shared, bv1_ref, None,
                        nc_ctx, nc_lat)]

    hw = 2 * RWKV_HEAD
    lane_a = lax.broadcasted_iota(jnp.int32, (n, hw), 1) < RWKV_HEAD
    sel_a = lambda x: jnp.where(lane_a, x, 0.0)
    sel_b = lambda x: jnp.where(lane_a, 0.0, x)
    pick = lambda xa, xb: jnp.where(lane_a, xa, xb)
    bdiag = lambda x: jnp.concatenate([sel_a(x), sel_b(x)], axis=0)
    same_head = ((lax.broadcasted_iota(jnp.int32, (hw, hw), 0) < RWKV_HEAD)
                 == (lax.broadcasted_iota(jnp.int32, (hw, hw), 1) < RWKV_HEAD))
    eye_f = (lax.broadcasted_iota(jnp.int32, (n, n), 0) == lax.broadcasted_iota(jnp.int32, (n, n), 1)).astype(F32)
    eye2 = jnp.concatenate([eye_f, eye_f], axis=1)
    strict2 = [jnp.concatenate([f[d]["strict_f"]] * 2, axis=1) > 0.5 for d in range(2)]
    incl2 = [jnp.concatenate([f[d]["incl_f"]] * 2, axis=1) > 0.5 for d in range(2)]

    items = [(d, q) for d in range(2) for q in range(RWKV_HEADS // 2)]
    idx = range(len(items))
    get = lambda name, j: f[items[j][0]][name][:, items[j][1] * hw:(items[j][1] + 1) * hw]
    strict = [strict2[d] for d, _ in items]
    incl = [incl2[d] for d, _ in items]

    v_p = [get("v", j) for j in idx]
    kk_t = [get("kk_t", j) for j in idx]
    r_t = [get("r_t", j) for j in idx]
    b_i = [get("b_i", j) for j in idx]
    k_i = [get("k_i", j) for j in idx]
    lhs = [jnp.concatenate([kk_t[j], r_t[j]], axis=0) for j in idx]
    gram = [_dot_nt(lhs[j], jnp.concatenate([sel_a(b_i[j]), sel_a(k_i[j]), sel_b(k_i[j]), sel_b(b_i[j])], axis=0))
            for j in idx]
    gram_a = [gram[j][:, :hw] for j in idx]
    gram_b = [gram[j][:, hw:] for j in idx]
    pw = [jnp.where(strict[j], -pick(gram_a[j][:n], gram_b[j][:n]), 0.0) for j in idx]
    l_k = [jnp.where(strict[j], pick(gram_b[j][:n], gram_a[j][:n]), 0.0) for j in idx]
    m_rb = [jnp.where(incl[j], pick(gram_a[j][n:], gram_b[j][n:]), 0.0) for j in idx]
    m_rk = [jnp.where(incl[j], pick(gram_b[j][n:], gram_a[j][n:]), 0.0) for j in idx]
    v_x = [jnp.concatenate([sel_b(v_p[j]), sel_a(v_p[j])], axis=0) for j in idx]
    lm = [_dot(jnp.concatenate([l_k[j], m_rk[j]], axis=0), v_x[j]) for j in idx]
    lkv = [lm[j][:n] for j in idx]
    tok_r = lax.broadcasted_iota(jnp.int32, (n, hw), 0)
    tok_c = lax.broadcasted_iota(jnp.int32, (n, hw), 1) % n
    blk = lambda m: (tok_r // m) == (tok_c // m)
    nl = pw
    pw = [jnp.where(blk(8), nl[j], 0.0) for j in idx]
    t_inv = [eye2 + pw[j] for j in idx]
    pw = [_dot(pw[j], bdiag(pw[j])) for j in idx]
    both = [_dot(jnp.concatenate([t_inv[j], pw[j]], axis=0), bdiag(pw[j])) for j in idx]
    t_inv = [t_inv[j] + both[j][:n] for j in idx]
    t_inv = [t_inv[j] + _dot(t_inv[j], bdiag(both[j][n:])) for j in idx]
    for m in (8, 16, 32):
        off = [jnp.where(jnp.logical_and(blk(2 * m), jnp.logical_not(blk(m))), nl[j], 0.0) for j in idx]
        ct = [_dot(off[j], bdiag(t_inv[j])) for j in idx]
        t_inv = [t_inv[j] + _dot(t_inv[j], bdiag(ct[j])) for j in idx]
    tx = [_dot(t_inv[j], jnp.concatenate([bdiag(kk_t[j]), bdiag(lkv[j])], axis=1)) for j in idx]
    w_p = [tx[j][:, :hw] for j in idx]
    u_p = [tx[j][:, hw:] for j in idx]
    mw = [_dot(m_rb[j], jnp.concatenate([bdiag(w_p[j]), bdiag(u_p[j])], axis=1)) for j in idx]
    r_eff = [r_t[j] - mw[j][:, :hw] for j in idx]
    y_loc = [lm[j][n:] - mw[j][:, hw:] for j in idx]
    b_e = [-get("b_e", j) for j in idx]
    g_a = [_dot_tn(w_p[j], b_e[j]) for j in idx]
    g_d = [_dot_tn(jnp.concatenate([v_p[j], u_p[j]], axis=0),
                   jnp.concatenate([get("k_e", j), b_e[j]], axis=0)) for j in idx]
    s0 = [state[d, q] for d, q in items]
    ys = [_dot_nt(r_eff[j], bdiag(s0[j])) + y_loc[j] for j in idx]
    half = len(items) // 2
    y0_ref[...] = jnp.concatenate(ys[:half], axis=1)
    y1_ref[...] = jnp.concatenate(ys[half:], axis=1)
    s1 = [_dot(s0[j], jnp.where(same_head, g_a[j], 0.0)) for j in idx]
    for j, (d, q) in enumerate(items):
        state[d, q] = s0[j] * get("p_all", j) + s1[j] + pick(g_d[j][:n], g_d[j][n:])


def _rwkv_mirror(i, nc_ctx, nc_lat):
    return jnp.where(i < nc_ctx, nc_ctx - 1 - i, 2 * nc_ctx + nc_lat - 1 - i)


def _rwkv_scan(p, mu, vecs, wup, aup, gup, masks, bsz, n_ctx, n_lat):
    ltot = n_ctx + n_lat
    nc_ctx, nc_lat = n_ctx // CHUNK, n_lat // CHUNK
    nc = nc_ctx + nc_lat
    rows = bsz * ltot
    hb = CHUNK // 16
    n_hblk = rows // 16
    chunk_of = (lambda i: i, lambda i: _rwkv_mirror(i, nc_ctx, nc_lat))

    def specs(d):
        main = lambda b, i: (b * nc + chunk_of[d](i), 0)
        prev = lambda b, i: (jnp.maximum((b * nc + chunk_of[d](i)) * hb - 1, 0), 0)
        nxt = lambda b, i: (jnp.minimum((b * nc + chunk_of[d](i) + 1) * hb, n_hblk - 1), 0)
        return main, [pl.BlockSpec((CHUNK, RWKV_IN), main), pl.BlockSpec((16, RWKV_IN), prev),
                      pl.BlockSpec((16, RWKV_IN), nxt)]

    (main0, in0), (main1, in1) = specs(0), specs(1)
    whole = lambda shape: pl.BlockSpec(shape, lambda b, i: (0,) * len(shape))
    out_f = jax.ShapeDtypeStruct((rows, RWKV_W), F32)
    out_bf = jax.ShapeDtypeStruct((rows, RWKV_W), BF16)
    ospec = lambda m: pl.BlockSpec((CHUNK, RWKV_W), m)
    kern = functools.partial(_rwkv_kernel, nc_ctx=nc_ctx, nc_lat=nc_lat)
    return pl.pallas_call(
        kern,
        out_shape=(out_f, out_f, out_bf, out_bf, out_bf),
        grid=(bsz, nc),
        in_specs=in0 + in1 + [whole((2, RWKV_IN)), whole((2, 8, RWKV_W)),
                              whole((2, 2 * DECAY_LORA, RWKV_W)), whole((2, 2 * AAA_LORA, RWKV_W)),
                              whole((GATE_LORA, RWKV_W)), whole((2, 2, CHUNK, CHUNK))],
        out_specs=(ospec(main0), ospec(main1), ospec(main0), ospec(main1), ospec(main0)),
        scratch_shapes=[pltpu.VMEM((2, RWKV_HEADS // 2, RWKV_HEAD, 2 * RWKV_HEAD), F32)],
        compiler_params=_cparams(("arbitrary", "arbitrary")),
        name="rwkv_scan",
    )(p, p, p, p, p, p, mu, vecs, wup, aup, gup, masks)


def _diff_kernel(q_ref, k_ref, v_ref, par_ref, o_ref, *, n_ctx):
    t = pl.program_id(2)
    lam = par_ref[0:1, :]
    gain = par_ref[1:2, :]

    def attend(nk):
        q = q_ref[...]
        k = k_ref[0:nk, :]
        v = v_ref[0:nk, :]

        def softmax_v(sl):
            s = _dot_nt(q[:, sl], k[:, sl])
            e = jnp.exp2(s - jnp.max(s, axis=-1, keepdims=True))
            return _dot(e, v) / jnp.sum(e, axis=-1, keepdims=True)

        o = softmax_v(slice(0, DIFF_DK)) - lam * softmax_v(slice(DIFF_DK, 2 * DIFF_DK))
        o = o * lax.rsqrt(jnp.mean(o * o, axis=-1, keepdims=True) + 1e-5)
        o_ref[...] = (o * gain).astype(o_ref.dtype)

    @pl.when(t * SEG < n_ctx)
    def _():
        attend(n_ctx)

    @pl.when(t * SEG >= n_ctx)
    def _():
        attend(k_ref.shape[0])


def _diff_attn(qk, p, par, bsz, n_ctx, ltot):
    rows = bsz * ltot
    nt = ltot // SEG
    voff = (P_DIFF + 2 * DIFF_W) // DIFF_DV
    return pl.pallas_call(
        functools.partial(_diff_kernel, n_ctx=n_ctx),
        out_shape=jax.ShapeDtypeStruct((rows, DIFF_W), BF16),
        grid=(bsz, DIFF_HEADS, nt),
        in_specs=[pl.BlockSpec((SEG, DIFF_DV), lambda b, h, t: (b * nt + t, h)),
                  pl.BlockSpec((ltot, DIFF_DV), lambda b, h, t: (b, DIFF_HEADS + h)),
                  pl.BlockSpec((ltot, DIFF_DV), lambda b, h, t: (b, voff + h)),
                  pl.BlockSpec((8, DIFF_DV), lambda b, h, t: (0, 0))],
        out_specs=pl.BlockSpec((SEG, DIFF_DV), lambda b, h, t: (b * nt + t, h)),
        compiler_params=_cparams(("arbitrary", "arbitrary", "arbitrary")),
        name="diff_attn",
    )(qk, qk, p, par)


def _win_kernel(sink_ref, q_ref, kp_ref, kc_ref, kn_ref, kx_ref, vp_ref, vc_ref, vn_ref, vx_ref, bias_ref,
                o_ref, *, n_ctx, n_lat):
    blk = pl.program_id(1)
    ncb = n_ctx // QBLK
    nq = WIN_GROUP * QBLK

    def run(keys, vals, bias):
        outs = []
        for g in range(WIN_KV_HEADS):
            ksl = slice(g * WIN_HEAD, (g + 1) * WIN_HEAD)
            qg = jnp.concatenate(
                [q_ref[:, (g * WIN_GROUP + j) * WIN_HEAD:(g * WIN_GROUP + j + 1) * WIN_HEAD]
                 for j in range(WIN_GROUP)], axis=0)
            s = _dot_nt(qg, keys[:, ksl])
            if bias is not None:
                s = s + bias
            hrow = lax.broadcasted_iota(jnp.int32, (nq, 1), 0) // QBLK
            sink = jnp.zeros((nq, 1), F32)
            for j in range(WIN_GROUP):
                sink = jnp.where(hrow == j, sink_ref[g * WIN_GROUP + j] * LOG2E, sink)
            m = jnp.maximum(jnp.max(s, axis=-1, keepdims=True), sink)
            e = jnp.exp2(s - m)
            den = jnp.sum(e, axis=-1, keepdims=True) + jnp.exp2(sink - m)
            og = _dot(e, vals[:, ksl]) / den
            outs.extend(og[j * QBLK:(j + 1) * QBLK] for j in range(WIN_GROUP))
        o_ref[...] = jnp.concatenate(outs, axis=1).astype(o_ref.dtype)

    @pl.when(blk < ncb)
    def _():
        run(kx_ref[...], vx_ref[...], None)

    @pl.when(blk >= ncb)
    def _():
        keys = jnp.concatenate([kp_ref[...], kc_ref[...], kn_ref[...], kx_ref[...]], axis=0)
        vals = jnp.concatenate([vp_ref[...], vc_ref[...], vn_ref[...], vx_ref[...]], axis=0)
        run(keys, vals, bias_ref[0])


def _win_bias(n_ctx, n_lat):
    nq, nk = WIN_GROUP * QBLK, 3 * QBLK + n_ctx
    assert n_lat >= 2 * QBLK
    r = (jnp.arange(nq) % QBLK)[:, None]
    col = jnp.arange(nk)[None, :]
    band = (jnp.abs(r - col + QBLK) <= WINDOW) & (col < 3 * QBLK)
    ctx = col >= 3 * QBLK
    valid = jnp.stack([(band & (col >= QBLK)) | ctx, band | ctx, (band & (col < 2 * QBLK)) | ctx])
    return jnp.where(valid, 0.0, NEG_INF).astype(F32)


def _win_attn(sink, qk, p, bsz, n_ctx, n_lat):
    ltot = n_ctx + n_lat
    rows = bsz * ltot
    nb = ltot // QBLK
    ncb = n_ctx // QBLK
    koff = WIN_W // WIN_KV_W
    voff = (P_WIN + WIN_W + WIN_KV_W) // WIN_KV_W

    def lat_blk(b, t, shift):
        j = jnp.clip(t - ncb + shift, 0, nb - ncb - 1)
        return b * nb + ncb + j

    kspec = lambda shift, c: pl.BlockSpec((QBLK, WIN_KV_W), lambda b, t, s: (lat_blk(b, t, shift), c))
    xspec = lambda c: pl.BlockSpec((n_ctx, WIN_KV_W), lambda b, t, s: (b * (ltot // n_ctx), c))
    variant = lambda t: jnp.where(t <= ncb, 0, jnp.where(t >= nb - 1, 2, 1))
    bspec = pl.BlockSpec((1, WIN_GROUP * QBLK, 3 * QBLK + n_ctx), lambda b, t, s: (variant(t), 0, 0))
    return pl.pallas_call(
        functools.partial(_win_kernel, n_ctx=n_ctx, n_lat=n_lat),
        out_shape=jax.ShapeDtypeStruct((rows, WIN_W), BF16),
        grid_spec=pltpu.PrefetchScalarGridSpec(
            num_scalar_prefetch=1,
            grid=(bsz, nb),
            in_specs=[pl.BlockSpec((QBLK, WIN_W), lambda b, t, s: (b * nb + t, 0)),
                      kspec(-1, koff), kspec(0, koff), kspec(1, koff), xspec(koff),
                      kspec(-1, voff), kspec(0, voff), kspec(1, voff), xspec(voff), bspec],
            out_specs=pl.BlockSpec((QBLK, WIN_W), lambda b, t, s: (b * nb + t, 0))),
        compiler_params=_cparams(("arbitrary", "arbitrary")),
        name="win_attn",
    )(sink, qk, qk, qk, qk, qk, p, p, p, p, _win_bias(n_ctx, n_lat))


def _merge_kernel(ya_ref, yb_ref, yc_ref, ga_ref, gb_ref, gc_ref, wa_ref, wb_ref, wc_ref, o_ref):
    z = _sigmoid(ga_ref[...].astype(F32)) * _dot(ya_ref[...], wa_ref[...])
    z = z + _sigmoid(gb_ref[...].astype(F32)) * _dot(yb_ref[...], wb_ref[...])
    z = z + _sigmoid(gc_ref[...].astype(F32)) * _dot(yc_ref[...], wc_ref[...])
    o_ref[...] = z.astype(o_ref.dtype)


def _merge(ya, yb, yc, p, w_branch, layer, tm, tn):
    m = ya.shape[0]
    assert RWKV_W == DIFF_W and (RWKV_W + DIFF_W) % WIN_W == 0
    nj = D_MODEL // tn
    goff = P_GATE // tn
    gspec = lambda br: pl.BlockSpec((tm, tn), lambda j, i: (i, goff + br * nj + j))
    return pl.pallas_call(
        _merge_kernel,
        out_shape=jax.ShapeDtypeStruct((m, D_MODEL), BF16),
        grid=(nj, m // tm),
        in_specs=[pl.BlockSpec((tm, RWKV_W), lambda j, i: (i, 0)),
                  pl.BlockSpec((tm, DIFF_W), lambda j, i: (i, 0)),
                  pl.BlockSpec((tm, WIN_W), lambda j, i: (i, 0)),
                  gspec(0), gspec(1), gspec(2),
                  pl.BlockSpec((None, RWKV_W, tn), lambda j, i: (layer, 0, j)),
                  pl.BlockSpec((None, DIFF_W, tn), lambda j, i: (layer, 1, j)),
                  pl.BlockSpec((None, WIN_W, tn), lambda j, i: (layer, (RWKV_W + DIFF_W) // WIN_W, j))],
        out_specs=pl.BlockSpec((tm, tn), lambda j, i: (i, j)),
        compiler_params=_cparams(("arbitrary", "arbitrary")),
        name="merge",
    )(ya, yb, yc, p, p, p, w_branch, w_branch, w_branch)


def _expert_kernel(be_ref, nu_ref, x_ref, gw_ref, w1_ref, w3_ref, w2_ref, o_ref, w1b, w3b, w2b):
    i = pl.program_id(0)
    prev = be_ref[jnp.maximum(i - 1, 0)]

    @pl.when(jnp.logical_or(i == 0, be_ref[i] != prev))
    def _():
        w1b[...] = w1_ref[...].astype(BF16)
        w3b[...] = w3_ref[...].astype(BF16)
        w2b[...] = w2_ref[...].astype(BF16)

    @pl.when(i < nu_ref[0])
    def _():
        x = x_ref[...]
        h1 = _dot(x, w1b[...])
        h = h1 * _sigmoid(h1) * _dot(x, w3b[...])
        o_ref[...] = (_dot(h, w2b[...]) * gw_ref[...]).astype(o_ref.dtype)

    @pl.when(i >= nu_ref[0])
    def _():
        o_ref[...] = jnp.zeros_like(o_ref)


def _experts(blk_expert, n_used, xb, gw, w1, w3, w2, layer):
    rows = xb.shape[0]
    nb = rows // MOE_BLK
    return pl.pallas_call(
        _expert_kernel,
        out_shape=jax.ShapeDtypeStruct((rows, D_MODEL), BF16),
        grid_spec=pltpu.PrefetchScalarGridSpec(
            num_scalar_prefetch=2,
            grid=(nb,),
            in_specs=[pl.BlockSpec((MOE_BLK, D_MODEL), lambda i, be, nu: (i, 0)),
                      pl.BlockSpec((MOE_BLK, 1), lambda i, be, nu: (i, 0)),
                      pl.BlockSpec((None, None, D_MODEL, D_EXPERT), lambda i, be, nu: (layer, be[i], 0, 0)),
                      pl.BlockSpec((None, None, D_MODEL, D_EXPERT), lambda i, be, nu: (layer, be[i], 0, 0)),
                      pl.BlockSpec((None, None, D_EXPERT, D_MODEL), lambda i, be, nu: (layer, be[i], 0, 0))],
            out_specs=pl.BlockSpec((MOE_BLK, D_MODEL), lambda i, be, nu: (i, 0)),
            scratch_shapes=[pltpu.VMEM((D_MODEL, D_EXPERT), BF16),
                            pltpu.VMEM((D_MODEL, D_EXPERT), BF16),
                            pltpu.VMEM((D_EXPERT, D_MODEL), BF16)]),
        compiler_params=_cparams(("arbitrary",)),
        name="experts",
    )(blk_expert, n_used, xb, gw, w1, w3, w2)


def _rope_tables(n_ctx, n_lat):
    rows = n_lat // GRID_W
    row = jnp.repeat(jnp.arange(rows), GRID_W).astype(F32)
    col = (jnp.arange(rows * GRID_W) % GRID_W).astype(F32)
    inv = ROPE_BASE ** (-jnp.arange(ROPE_AX_FREQS, dtype=F32) / ROPE_AX_FREQS)
    ang = jnp.stack([row[:, None] * inv, col[:, None] * inv], axis=1)
    cos, sin = jnp.cos(ang), jnp.sin(ang)
    cos4 = jnp.stack([cos, cos], axis=2).reshape(n_lat, 4 * ROPE_AX_FREQS)
    sin4 = jnp.stack([-sin, sin], axis=2).reshape(n_lat, 4 * ROPE_AX_FREQS)
    cos4 = jnp.concatenate([jnp.ones((n_ctx, 64), F32), cos4], axis=0)
    sin4 = jnp.concatenate([jnp.zeros((n_ctx, 64), F32), sin4], axis=0)
    return jnp.stack([jnp.tile(cos4, (1, LANE // 64)), jnp.tile(sin4, (1, LANE // 64))])


def _moe(route, v, w1, w3, w2, layer):
    t = v.shape[0]
    experts = route[:, :EXPERT_TOP_K].astype(jnp.int32)
    gates = route[:, EXPERT_TOP_K:2 * EXPERT_TOP_K]
    k = EXPERT_TOP_K
    a = t * k
    e_n = N_EXPERTS
    nb = -(-a // MOE_BLK) + e_n
    e_flat = experts.T.reshape(a)
    iota = jnp.arange(a, dtype=jnp.int32)
    e_s, order, g_s = lax.sort((e_flat, iota, gates.T.reshape(a)), num_keys=1, is_stable=True)
    ids = jnp.arange(e_n, dtype=jnp.int32)
    start = jnp.sum(e_s[None, :] < ids[:, None], axis=1, dtype=jnp.int32)
    counts = jnp.sum(e_s[None, :] == ids[:, None], axis=1, dtype=jnp.int32)
    padded = (counts + MOE_BLK - 1) // MOE_BLK * MOE_BLK
    pad_end = jnp.cumsum(padded)
    pad_start = pad_end - padded
    shift = jnp.sum(jnp.where(e_s[:, None] == ids[None, :], (pad_start - start)[None, :], 0), axis=1)
    pos = lax.sort((order, iota + shift), num_keys=1)[1]
    blk_first = jnp.arange(nb, dtype=jnp.int32) * MOE_BLK
    blk_expert = jnp.minimum(jnp.sum(pad_end[None, :] <= blk_first[:, None], axis=1, dtype=jnp.int32), e_n - 1)
    n_used = (pad_end[-1:] // MOE_BLK).astype(jnp.int32)
    rank = blk_first[:, None] + jnp.arange(MOE_BLK, dtype=jnp.int32)[None, :] - pad_start[blk_expert][:, None]
    filled = (rank < counts[blk_expert][:, None]).reshape(nb * MOE_BLK)
    sidx = jnp.clip(start[blk_expert][:, None] + rank, 0, a - 1).reshape(nb * MOE_BLK)
    spread = jnp.arange(nb * MOE_BLK, dtype=jnp.int32) % t
    slot_tok = jnp.where(filled, order.at[sidx].get(mode="promise_in_bounds") % t, spread)
    slot_gate = jnp.where(filled, g_s.at[sidx].get(mode="promise_in_bounds"), 0.0)
    xb = v.at[slot_tok].get(mode="promise_in_bounds")
    ys = _experts(blk_expert, n_used, xb, slot_gate[:, None], w1, w3, w2, layer)
    yg = lax.optimization_barrier(ys.at[pos].get(mode="promise_in_bounds"))
    return yg.reshape(k, t, D_MODEL)


def kernel(x, c, ctx, c_ctx, w_mod, b_mod, w_in, rwkv_mu, rwkv_w0, rwkv_w_up, rwkv_a0, rwkv_a_up,
           rwkv_g_up, rwkv_kvec, rwkv_lnx, diff_lam, diff_subln, win_sink, w_branch, w_out, ln_g, ln_b,
           w_rg, b_rg, w_re, b_re, w1, w3, w2):
    bsz, n_lat, dm = x.shape
    n_ctx = ctx.shape[1]
    depth = w_mod.shape[0]
    ltot = n_ctx + n_lat
    rows = bsz * ltot
    dn_alpha = (2 * depth) ** 0.25
    assert dm == D_MODEL and n_ctx % SEG == 0 and n_lat % SEG == 0 and ltot % n_ctx == 0

    nt = ltot // SEG
    cs = _rope_tables(n_ctx, n_lat)
    fwd = jnp.tril(jnp.ones((CHUNK, CHUNK), F32))
    masks = jnp.stack([jnp.stack([fwd - jnp.eye(CHUNK, dtype=F32), fwd]),
                       jnp.stack([fwd.T - jnp.eye(CHUNK, dtype=F32), fwd.T])])
    head_id = jnp.arange(RWKV_W) // RWKV_HEAD
    head_avg = ((head_id[:, None] == head_id[None, :]).astype(F32) / RWKV_HEAD).astype(BF16)

    xs = jnp.concatenate([ctx, x], axis=1).reshape(rows, dm)
    cvec = jnp.concatenate([c_ctx[None, :], c], axis=0)
    cpad = jnp.zeros((32, dm), F32).at[:bsz + 1].set(jax.nn.silu(cvec))
    w_in_p = jnp.concatenate([w_in[:, :, :RWKV_IN], jnp.zeros((depth, dm, P_GATE - RWKV_IN), F32),
                              w_in[:, :, GATE_OFF:], w_in[:, :, DIFF_OFF:GATE_OFF]], axis=2).astype(BF16)
    w_branch_b = w_branch.astype(BF16)
    w_out_b = w_out.astype(BF16)
    w_r = jnp.zeros((depth, dm, LANE), F32).at[:, :, :N_GROUPS].set(w_rg)
    w_r = w_r.at[:, :, N_GROUPS:N_GROUPS + N_EXPERTS].set(w_re)
    b_r = jnp.zeros((depth, 8, LANE), F32).at[:, :, :N_GROUPS].set(b_rg[:, None, :])
    b_r = b_r.at[:, :, N_GROUPS:N_GROUPS + N_EXPERTS].set(b_re[:, None, :])

    mods = [_mm(cpad, w_mod, i, F32, 32, 1024, "mod")[:bsz + 1] + b_mod[i] for i in range(depth)]

    def table(gate, shift, scale):
        def both(m, j):
            v = m[:, j * dm:(j + 1) * dm]
            return jnp.stack([jnp.broadcast_to(v[0], (bsz, dm)), v[1:]], axis=1)
        t = jnp.stack([both(*gate), both(*shift), both(*scale)], axis=2)
        return jnp.pad(t, ((0, 0), (0, 0), (0, 5), (0, 0))).reshape(2 * bsz, 8, dm)

    u = _modulate0(xs, table((mods[0], 0), (mods[0], 0), (mods[0], 1)), nt)
    for i in range(depth):
        last = i == depth - 1
        lam_init = 0.8 - 0.6 * math.exp(-0.3 * i)
        mod = mods[i]
        p = _mm(u, w_in_p, i, BF16, 512, 2048, "in_proj")

        vecs = jnp.zeros((2, 8, RWKV_W), F32)
        vecs = vecs.at[:, 0].set(rwkv_w0[i]).at[:, 1].set(rwkv_a0[i])
        vecs = vecs.at[:, 2:5].set(jnp.broadcast_to(rwkv_kvec[i][None], (2, 3, RWKV_W)))
        wup = jnp.zeros((2, 2 * DECAY_LORA, RWKV_W), BF16)
        aup = jnp.zeros((2, 2 * AAA_LORA, RWKV_W), BF16)
        for d in range(2):
            wup = wup.at[d, d * DECAY_LORA:(d + 1) * DECAY_LORA].set(rwkv_w_up[i, d].astype(BF16))
            aup = aup.at[d, d * AAA_LORA:(d + 1) * AAA_LORA].set(rwkv_a_up[i, d].astype(BF16))
        scan = _rwkv_scan(p, rwkv_mu[i], vecs, wup, aup, rwkv_g_up[i].astype(BF16), masks, bsz, n_ctx, n_lat)
        ya = _readout(*scan, rwkv_lnx[i], head_avg, 512)

        dqk, wqk = _rope(p, cs, nt)
        lf = diff_lam[i]
        lam = jnp.exp(jnp.sum(lf[0] * lf[1])) - jnp.exp(jnp.sum(lf[2] * lf[3])) + lam_init
        par = jnp.zeros((8, DIFF_DV), F32).at[0].set(lam).at[1].set(diff_subln[i] * (1 - lam_init))
        yb = _diff_attn(dqk, p, par, bsz, n_ctx, ltot)
        yc = _win_attn(win_sink[i], wqk, p, bsz, n_ctx, n_lat)
        z = _merge(ya, yb, yc, p, w_branch_b, i, 512, 1024)

        lnp = lambda j: jnp.zeros((8, dm), F32).at[0].set(ln_g[i, j]).at[1].set(ln_b[i, j])
        xs, v, route = _mix_out(z, xs, w_out_b, i, table((mod, 2), (mod, 3), (mod, 4)), lnp(0),
                                w_r[i], b_r[i], nt, dn_alpha)

        if last:
            lat = lambda t: t.reshape(bsz, ltot, -1)[:, n_ctx:].reshape(bsz * n_lat, -1)
            yg = _moe(lat(route), lat(v), w1, w3, w2, i)
            tab = table((mod, 5), (mod, 0), (mod, 1))
            (out,) = _moe_out(xs, yg, tab, lnp(1), bsz, nt, n_ctx // SEG, dn_alpha, False)
            return out.reshape(bsz, n_lat, dm)
        yg = _moe(route, v, w1, w3, w2, i)
        tab = table((mod, 5), (mods[i + 1], 0), (mods[i + 1], 1))
        xs, u = _moe_out(xs, yg, tab, lnp(1), bsz, nt, 0, dn_alpha, True)
    return None
```

```python
import functools
import math

import jax
import jax.numpy as jnp
from jax import lax
from jax.experimental import pallas as pl
from jax.experimental.pallas import tpu as pltpu

F32 = jnp.float32
BF16 = jnp.bfloat16

D_MODEL = 2048
GRID_W = 64
RWKV_HEADS = 12
RWKV_HEAD = 64
RWKV_W = RWKV_HEADS * RWKV_HEAD
DECAY_LORA = 64
AAA_LORA = 64
GATE_LORA = 128
RWKV_GN_EPS = 64e-5
DIFF_HEADS = 6
DIFF_DK = 64
DIFF_DV = 2 * DIFF_DK
DIFF_W = DIFF_HEADS * DIFF_DV
DIFF_SCALE = DIFF_DK ** -0.5
WIN_Q_HEADS = 8
WIN_KV_HEADS = 2
WIN_GROUP = WIN_Q_HEADS // WIN_KV_HEADS
WIN_HEAD = 64
WIN_W = WIN_Q_HEADS * WIN_HEAD
WIN_KV_W = WIN_KV_HEADS * WIN_HEAD
WIN_SCALE = WIN_HEAD ** -0.5
WINDOW = 128
QBLK = WINDOW
MIX_W = RWKV_W + DIFF_W + WIN_W
N_BRANCH = 3
ROPE_BASE = 10000.0
ROPE_AX_FREQS = 16
RWKV_IN = 3 * RWKV_W + 2 * DECAY_LORA + 2 * AAA_LORA + GATE_LORA
DIFF_IN = 3 * DIFF_W
WIN_IN = WIN_W + 2 * WIN_KV_W
DIFF_OFF = RWKV_IN
WIN_OFF = DIFF_OFF + DIFF_IN
GATE_OFF = WIN_OFF + WIN_IN
N_IN = GATE_OFF + N_BRANCH * D_MODEL
N_GROUPS = 4
EXPERTS_PER_GROUP = 8
N_EXPERTS = N_GROUPS * EXPERTS_PER_GROUP
EXPERT_TOP_K = 2
D_EXPERT = D_MODEL // 4
MOE_BLK = 256
ADA_EPS = 1e-6
LN_EPS = 1e-5
NEG_INF = -1e30
LOG2E = math.log2(math.e)

LANE = 128
SEG = 256
P_RWKV = 0
P_GATE = 3072
P_DIFF = P_GATE + N_BRANCH * D_MODEL
P_WIN = P_DIFF + DIFF_IN
P_COLS = P_WIN + WIN_IN
CHUNK = 64
VMEM_LIMIT = 56 * 1024 * 1024


def _cparams(sem):
    return pltpu.CompilerParams(dimension_semantics=sem, vmem_limit_bytes=VMEM_LIMIT)


def _dot(a, b):
    return jnp.dot(a.astype(BF16), b.astype(BF16), preferred_element_type=F32)


def _dot_nt(a, b):
    return lax.dot_general(a.astype(BF16), b.astype(BF16), (((1,), (1,)), ((), ())),
                           preferred_element_type=F32)


def _dot_tn(a, b):
    return lax.dot_general(a.astype(BF16), b.astype(BF16), (((0,), (0,)), ((), ())),
                           preferred_element_type=F32)


def _split(x):
    hi = x.astype(BF16)
    lo = (x - hi.astype(F32)).astype(BF16)
    return hi, lo


def _dot3(a, b):
    ah, al = _split(a)
    bh, bl = _split(b)
    d = functools.partial(jnp.dot, preferred_element_type=F32)
    return d(ah, bh) + (d(ah, bl) + d(al, bh))


def _sigmoid(x):
    return 1.0 / (1.0 + jnp.exp(-x))


def _mm_kernel(a_ref, w_ref, o_ref):
    o_ref[...] = _dot(a_ref[...], w_ref[...]).astype(o_ref.dtype)


def _mm(a, w, layer, out_dtype, tm, tn, name):
    m, k = a.shape
    n = w.shape[2]
    return pl.pallas_call(
        _mm_kernel,
        out_shape=jax.ShapeDtypeStruct((m, n), out_dtype),
        grid=(n // tn, m // tm),
        in_specs=[pl.BlockSpec((tm, k), lambda j, i: (i, 0)),
                  pl.BlockSpec((None, k, tn), lambda j, i: (layer, 0, j))],
        out_specs=pl.BlockSpec((tm, tn), lambda j, i: (i, j)),
        compiler_params=_cparams(("arbitrary", "arbitrary")),
        name=name,
    )(a, w)


def _norm_rows(x, eps):
    mu = jnp.mean(x, axis=-1, keepdims=True)
    xc = x - mu
    return xc * lax.rsqrt(jnp.mean(xc * xc, axis=-1, keepdims=True) + eps)


def _route_rows(logits):
    col = lax.broadcasted_iota(jnp.int32, logits.shape, 1).astype(F32)
    big = float(LANE)
    is_g = col < N_GROUPS
    lg = jnp.where(is_g, logits, NEG_INF)
    g_max = jnp.max(lg, axis=-1, keepdims=True)
    g_sum = jnp.sum(jnp.where(is_g, jnp.exp(lg - g_max), 0.0), axis=-1, keepdims=True)
    pg_top = 1.0 / g_sum
    g_idx = jnp.min(jnp.where(is_g & (lg == g_max), col, big), axis=-1, keepdims=True)
    lo = N_GROUPS + EXPERTS_PER_GROUP * g_idx
    sel = (col >= lo) & (col < lo + EXPERTS_PER_GROUP)
    le = jnp.where(sel, logits, NEG_INF)
    m1 = jnp.max(le, axis=-1, keepdims=True)
    den = jnp.sum(jnp.where(sel, jnp.exp(le - m1), 0.0), axis=-1, keepdims=True)
    i1 = jnp.min(jnp.where(sel & (le == m1), col, big), axis=-1, keepdims=True)
    rest = sel & (col != i1)
    le2 = jnp.where(rest, logits, NEG_INF)
    m2 = jnp.max(le2, axis=-1, keepdims=True)
    i2 = jnp.min(jnp.where(rest & (le2 == m2), col, big), axis=-1, keepdims=True)
    p1 = 1.0 / den
    p2 = jnp.exp(m2 - m1) / den
    tot = p1 + p2
    out = jnp.where(col == 0.0, i1 - N_GROUPS, 0.0)
    out = jnp.where(col == 1.0, i2 - N_GROUPS, out)
    out = jnp.where(col == 2.0, pg_top * p1 / tot, out)
    return jnp.where(col == 3.0, pg_top * p2 / tot, out)


def _mod_kernel(x_ref, tab_ref, u_ref):
    u_ref[...] = (_norm_rows(x_ref[...], ADA_EPS) * (1.0 + tab_ref[0, 2:3, :]) + tab_ref[0, 1:2, :]).astype(u_ref.dtype)


def _modulate0(x, tab, nt):
    rows, dm = x.shape
    return pl.pallas_call(
        _mod_kernel,
        out_shape=jax.ShapeDtypeStruct((rows, dm), BF16),
        grid=(rows // SEG,),
        in_specs=[pl.BlockSpec((SEG, dm), lambda i: (i, 0)),
                  pl.BlockSpec((1, 8, dm), lambda i: (2 * (i // nt) + jnp.minimum(i % nt, 1), 0, 0))],
        out_specs=pl.BlockSpec((SEG, dm), lambda i: (i, 0)),
        compiler_params=_cparams(("arbitrary",)),
        name="modulate0",
    )(x, tab)


def _mix_out_kernel(z_ref, x_ref, w_ref, tab_ref, lnp_ref, wr_ref, br_ref, xo_ref, vo_ref, ro_ref, *, alpha):
    m = _dot(z_ref[...], w_ref[...])
    xn = _norm_rows(alpha * x_ref[...] + tab_ref[0, 0:1, :] * m, LN_EPS) * lnp_ref[0:1, :] + lnp_ref[1:2, :]
    xo_ref[...] = xn
    v = _norm_rows(xn, ADA_EPS) * (1.0 + tab_ref[0, 2:3, :]) + tab_ref[0, 1:2, :]
    vo_ref[...] = v.astype(vo_ref.dtype)
    ro_ref[...] = _route_rows(_dot3(v, wr_ref[...]) + br_ref[0:1, :])


def _mix_out(z, x, w_out, layer, tab, lnp, w_r, b_r, nt, alpha):
    rows, dm = x.shape
    row = lambda i: (i, 0)
    const = lambda i: (0, 0)
    return pl.pallas_call(
        functools.partial(_mix_out_kernel, alpha=alpha),
        out_shape=(jax.ShapeDtypeStruct((rows, dm), F32), jax.ShapeDtypeStruct((rows, dm), BF16),
                   jax.ShapeDtypeStruct((rows, LANE), F32)),
        grid=(rows // SEG,),
        in_specs=[pl.BlockSpec((SEG, dm), row), pl.BlockSpec((SEG, dm), row),
                  pl.BlockSpec((None, dm, dm), lambda i: (layer, 0, 0)),
                  pl.BlockSpec((1, 8, dm), lambda i: (2 * (i // nt) + jnp.minimum(i % nt, 1), 0, 0)),
                  pl.BlockSpec((8, dm), const), pl.BlockSpec((dm, LANE), const), pl.BlockSpec((8, LANE), const)],
        out_specs=(pl.BlockSpec((SEG, dm), row), pl.BlockSpec((SEG, dm), row), pl.BlockSpec((SEG, LANE), row)),
        compiler_params=_cparams(("arbitrary",)),
        name="mix_out",
    )(z, x, w_out, tab, lnp, w_r, b_r)


def _moe_out_kernel(x_ref, y0_ref, y1_ref, tab_ref, lnp_ref, xo_ref, *u_ref, alpha):
    y = y0_ref[0].astype(F32) + y1_ref[0].astype(F32)
    xn = _norm_rows(alpha * x_ref[...] + tab_ref[0, 0:1, :] * y, LN_EPS) * lnp_ref[0:1, :] + lnp_ref[1:2, :]
    xo_ref[...] = xn
    if u_ref:
        u_ref[0][...] = (_norm_rows(xn, ADA_EPS) * (1.0 + tab_ref[0, 2:3, :]) + tab_ref[0, 1:2, :]).astype(BF16)


def _moe_out(x, yg, tab, lnp, bsz, nt, skip, alpha, emit_u):
    dm = x.shape[1]
    nk = nt - skip
    rows = bsz * nk * SEG
    xmap = lambda b, t: (b * nt + skip + t, 0)
    omap = lambda b, t: (b * nk + t, 0)
    out_shape = [jax.ShapeDtypeStruct((rows, dm), F32)]
    out_specs = [pl.BlockSpec((SEG, dm), omap)]
    if emit_u:
        out_shape.append(jax.ShapeDtypeStruct((rows, dm), BF16))
        out_specs.append(pl.BlockSpec((SEG, dm), omap))
    return pl.pallas_call(
        functools.partial(_moe_out_kernel, alpha=alpha),
        out_shape=tuple(out_shape),
        grid=(bsz, nk),
        in_specs=[pl.BlockSpec((SEG, dm), xmap),
                  pl.BlockSpec((1, SEG, dm), lambda b, t: (0, b * nk + t, 0)),
                  pl.BlockSpec((1, SEG, dm), lambda b, t: (1, b * nk + t, 0)),
                  pl.BlockSpec((1, 8, dm), lambda b, t: (2 * b + jnp.minimum(skip + t, 1), 0, 0)),
                  pl.BlockSpec((8, dm), lambda b, t: (0, 0))],
        out_specs=tuple(out_specs),
        compiler_params=_cparams(("arbitrary", "arbitrary")),
        name="moe_out",
    )(x, yg, yg, tab, lnp)


def _rope_kernel(pd_ref, pw_ref, cs_ref, dqk_ref, wqk_ref):
    cos = cs_ref[0]
    sin = cs_ref[1]
    first_half = lax.broadcasted_iota(jnp.int32, cos.shape, 1) % (2 * ROPE_AX_FREQS) < ROPE_AX_FREQS

    def rot(x, scale):
        x = x.astype(F32)
        sw = jnp.where(first_half, pltpu.roll(x, LANE - ROPE_AX_FREQS, axis=1), pltpu.roll(x, ROPE_AX_FREQS, axis=1))
        return ((x * cos + sw * sin) * scale).astype(BF16)

    for c in range(2 * DIFF_W // LANE):
        scale = DIFF_SCALE * LOG2E if c < DIFF_W // LANE else 1.0
        dqk_ref[:, c * LANE:(c + 1) * LANE] = rot(pd_ref[:, c * LANE:(c + 1) * LANE], scale)
    for c in range((WIN_W + WIN_KV_W) // LANE):
        scale = WIN_SCALE * LOG2E if c < WIN_W // LANE else 1.0
        wqk_ref[:, c * LANE:(c + 1) * LANE] = rot(pw_ref[:, c * LANE:(c + 1) * LANE], scale)


def _rope(p, cs, nt):
    rows = p.shape[0]
    wd, ww = 2 * DIFF_W, WIN_W + WIN_KV_W
    return pl.pallas_call(
        _rope_kernel,
        out_shape=(jax.ShapeDtypeStruct((rows, wd), BF16), jax.ShapeDtypeStruct((rows, ww), BF16)),
        grid=(rows // SEG,),
        in_specs=[pl.BlockSpec((SEG, wd), lambda i: (i, P_DIFF // wd)),
                  pl.BlockSpec((SEG, ww), lambda i: (i, P_WIN // ww)),
                  pl.BlockSpec((2, SEG, LANE), lambda i: (0, i % nt, 0))],
        out_specs=(pl.BlockSpec((SEG, wd), lambda i: (i, 0)), pl.BlockSpec((SEG, ww), lambda i: (i, 0))),
        compiler_params=_cparams(("arbitrary",)),
        name="rope",
    )(p, p, cs)


def _readout_kernel(y0_ref, y1_ref, bv0_ref, bv1_ref, g_ref, lnx_ref, avg_ref, o_ref):
    avg = avg_ref[...]

    def head_mean(t):
        hi, lo = _split(t)
        return jnp.dot(hi, avg, preferred_element_type=F32) + jnp.dot(lo, avg, preferred_element_type=F32)

    y = y0_ref[...] + y1_ref[...]
    dev = y - head_mean(y)
    yn = dev * lax.rsqrt(head_mean(dev * dev) + RWKV_GN_EPS) * lnx_ref[0:1, :] + lnx_ref[1:2, :]
    bonus = bv0_ref[...].astype(F32) + bv1_ref[...].astype(F32)
    o_ref[...] = ((yn + bonus) * g_ref[...].astype(F32)).astype(o_ref.dtype)


def _readout(y0, y1, bv0, bv1, g, lnx, avg, tm):
    rows = y0.shape[0]
    row = pl.BlockSpec((tm, RWKV_W), lambda i: (i, 0))
    return pl.pallas_call(
        _readout_kernel,
        out_shape=jax.ShapeDtypeStruct((rows, RWKV_W), BF16),
        grid=(rows // tm,),
        in_specs=[row, row, row, row, row,
                  pl.BlockSpec((2, RWKV_W), lambda i: (0, 0)),
                  pl.BlockSpec((RWKV_W, RWKV_W), lambda i: (0, 0))],
        out_specs=row,
        compiler_params=_cparams(("arbitrary",)),
        name="rwkv_readout",
    )(y0, y1, bv0, bv1, g, lnx, avg)


def _rwkv_features(d, chunk, p_ref, hp_ref, hn_ref, mu_ref, vec_ref, wup_ref, aup_ref, gup_ref, msk_ref,
                   bv_ref, g_ref, nc_ctx, nc_lat):
    n = CHUNK
    p = p_ref[...].astype(F32)
    first = jnp.logical_or(chunk == 0, chunk == nc_ctx)
    last = jnp.logical_or(chunk == nc_ctx - 1, chunk == nc_ctx + nc_lat - 1)
    hp = jnp.where(first, 0.0, hp_ref[15:16, :].astype(F32))
    hn = jnp.where(last, 0.0, hn_ref[0:1, :].astype(F32))
    row = lax.broadcasted_iota(jnp.int32, (n, 1), 0)
    prev = jnp.where(row == 0, hp, pltpu.roll(p, 1, axis=0))
    nxt = jnp.where(row == n - 1, hn, pltpu.roll(p, n - 1, axis=0))
    ps = p + mu_ref[0:1, :] * (prev - p) + mu_ref[1:2, :] * (nxt - p)

    r = ps[:, 0:RWKV_W]
    k = ps[:, RWKV_W:2 * RWKV_W]
    v = ps[:, 2 * RWKV_W:3 * RWKV_W]
    o = 3 * RWKV_W
    wd = jnp.tanh(ps[:, o:o + 2 * DECAY_LORA])
    ad = ps[:, o + 2 * DECAY_LORA:o + 2 * DECAY_LORA + 2 * AAA_LORA]
    gd = ps[:, o + 2 * DECAY_LORA + 2 * AAA_LORA:]
    w0 = vec_ref[d, 0:1, :]
    a0 = vec_ref[d, 1:2, :]
    k_k = vec_ref[d, 2:3, :]
    k_a = vec_ref[d, 3:4, :]
    r_k = vec_ref[d, 4:5, :]
    w_log = w0 + _dot(wd, wup_ref[d])
    a = _sigmoid(a0 + _dot(ad, aup_ref[d]))
    if g_ref is not None:
        g_ref[...] = _dot(_sigmoid(gd), gup_ref[...]).astype(g_ref.dtype)
    logw = -math.exp(-0.5) * _sigmoid(w_log)

    incl_f = msk_ref[d, 1]
    lw_hi, lw_lo = _split(logw)
    incl_b = incl_f.astype(BF16)
    cl = (jnp.dot(incl_b, lw_hi, preferred_element_type=F32)
          + jnp.dot(incl_b, lw_lo, preferred_element_type=F32))
    tot = jnp.sum(logw, axis=0, keepdims=True)
    e_in = jnp.exp(cl)
    e_ex = jnp.exp(cl - logw)
    e_inv = jnp.exp(-cl)
    e_end = jnp.exp(tot - cl)
    p_all = jnp.exp(tot)

    sls = [slice(h * RWKV_HEAD, (h + 1) * RWKV_HEAD) for h in range(RWKV_HEADS)]
    kk_n = k * k_k
    kk_sq = kk_n * kk_n
    kd_all = k * (1.0 + (a - 1.0) * k_a)
    rkd = r * kd_all * r_k
    inv_norm = [1.0 / jnp.maximum(jnp.sqrt(jnp.sum(kk_sq[:, sl], axis=-1, keepdims=True)), 1e-12) for sl in sls]
    bonus = [jnp.sum(rkd[:, sl], axis=-1, keepdims=True) for sl in sls]
    bv_ref[...] = jnp.concatenate([bonus[h] * v[:, sl] for h, sl in enumerate(sls)], axis=1).astype(bv_ref.dtype)
    kk_all = jnp.concatenate([kk_n[:, sl] * inv_norm[h] for h, sl in enumerate(sls)], axis=1)
    bd_all = kk_all * a
    return dict(
        v=v, strict_f=msk_ref[d, 0], incl_f=incl_f, p_all=p_all,
        kk_t=kk_all * e_ex,
        r_t=r * e_in,
        b_i=bd_all * e_inv, k_i=kd_all * e_inv,
        k_e=kd_all * e_end,
        b_e=bd_all * e_end)


def _rwkv_kernel(p0_ref, hp0_ref, hn0_ref, p1_ref, hp1_ref, hn1_ref, mu_ref, vec_ref, wup_ref, aup_ref,
                 gup_ref, msk_ref, y0_ref, y1_ref, bv0_ref, bv1_ref, g_ref, state, *, nc_ctx, nc_lat):
    i = pl.program_id(1)

    @pl.when(i == 0)
    def _():
        state[...] = jnp.zeros_like(state)

    n = CHUNK
    shared = (mu_ref, vec_ref, wup_ref, aup_ref, gup_ref, msk_ref)
    f = [_rwkv_features(0, i, p0_ref, hp0_ref, hn0_ref, *shared, bv0_ref, g_ref, nc_ctx, nc_lat),
         _rwkv_features(1, _rwkv_mirror(i, nc_ctx, nc_lat), p1_ref, hp1_ref, hn1_ref, *shared, bv1_ref, None,
                        nc_ctx, nc_lat)]

    hw = 2 * RWKV_HEAD
    lane_a = lax.broadcasted_iota(jnp.int32, (n, hw), 1) < RWKV_HEAD
    sel_a = lambda x: jnp.where(lane_a, x, 0.0)
    sel_b = lambda x: jnp.where(lane_a, 0.0, x)
    pick = lambda xa, xb: jnp.where(lane_a, xa, xb)
    bdiag = lambda x: jnp.concatenate([sel_a(x), sel_b(x)], axis=0)
    same_head = ((lax.broadcasted_iota(jnp.int32, (hw, hw), 0) < RWKV_HEAD)
                 == (lax.broadcasted_iota(jnp.int32, (hw, hw), 1) < RWKV_HEAD))
    eye_f = (lax.broadcasted_iota(jnp.int32, (n, n), 0) == lax.broadcasted_iota(jnp.int32, (n, n), 1)).astype(F32)
    eye2 = jnp.concatenate([eye_f, eye_f], axis=1)
    strict2 = [jnp.concatenate([f[d]["strict_f"]] * 2, axis=1) > 0.5 for d in range(2)]
    incl2 = [jnp.concatenate([f[d]["incl_f"]] * 2, axis=1) > 0.5 for d in range(2)]

    items = [(d, q) for d in range(2) for q in range(RWKV_HEADS // 2)]
    idx = range(len(items))
    get = lambda name, j: f[items[j][0]][name][:, items[j][1] * hw:(items[j][1] + 1) * hw]
    strict = [strict2[d] for d, _ in items]
    incl = [incl2[d] for d, _ in items]

    v_p = [get("v", j) for j in idx]
    kk_t = [get("kk_t", j) for j in idx]
    r_t = [get("r_t", j) for j in idx]
    b_i = [get("b_i", j) for j in idx]
    k_i = [get("k_i", j) for j in idx]
    lhs = [jnp.concatenate([kk_t[j], r_t[j]], axis=0) for j in idx]
    gram = [_dot_nt(lhs[j], jnp.concatenate([sel_a(b_i[j]), sel_a(k_i[j]), sel_b(k_i[j]), sel_b(b_i[j])], axis=0))
            for j in idx]
    gram_a = [gram[j][:, :hw] for j in idx]
    gram_b = [gram[j][:, hw:] for j in idx]
    pw = [jnp.where(strict[j], -pick(gram_a[j][:n], gram_b[j][:n]), 0.0) for j in idx]
    l_k = [jnp.where(strict[j], pick(gram_b[j][:n], gram_a[j][:n]), 0.0) for j in idx]
    m_rb = [jnp.where(incl[j], pick(gram_a[j][n:], gram_b[j][n:]), 0.0) for j in idx]
    m_rk = [jnp.where(incl[j], pick(gram_b[j][n:], gram_a[j][n:]), 0.0) for j in idx]
    v_x = [jnp.concatenate([sel_b(v_p[j]), sel_a(v_p[j])], axis=0) for j in idx]
    lm = [_dot(jnp.concatenate([l_k[j], m_rk[j]], axis=0), v_x[j]) for j in idx]
    lkv = [lm[j][:n] for j in idx]
    tok_r = lax.broadcasted_iota(jnp.int32, (n, hw), 0)
    tok_c = lax.broadcasted_iota(jnp.int32, (n, hw), 1) % n
    blk = lambda m: (tok_r // m) == (tok_c // m)
    nl = pw
    pw = [jnp.where(blk(8), nl[j], 0.0) for j in idx]
    t_inv = [eye2 + pw[j] for j in idx]
    pw = [_dot(pw[j], bdiag(pw[j])) for j in idx]
    both = [_dot(jnp.concatenate([t_inv[j], pw[j]], axis=0), bdiag(pw[j])) for j in idx]
    t_inv = [t_inv[j] + both[j][:n] for j in idx]
    t_inv = [t_inv[j] + _dot(t_inv[j], bdiag(both[j][n:])) for j in idx]
    for m in (8, 16, 32):
        off = [jnp.where(jnp.logical_and(blk(2 * m), jnp.logical_not(blk(m))), nl[j], 0.0) for j in idx]
        ct = [_dot(off[j], bdiag(t_inv[j])) for j in idx]
        t_inv = [t_inv[j] + _dot(t_inv[j], bdiag(ct[j])) for j in idx]
    tx = [_dot(t_inv[j], jnp.concatenate([bdiag(kk_t[j]), bdiag(lkv[j])], axis=1)) for j in idx]
    w_p = [tx[j][:, :hw] for j in idx]
    u_p = [tx[j][:, hw:] for j in idx]
    mw = [_dot(m_rb[j], jnp.concatenate([bdiag(w_p[j]), bdiag(u_p[j])], axis=1)) for j in idx]
    r_eff = [r_t[j] - mw[j][:, :hw] for j in idx]
    y_loc = [lm[j][n:] - mw[j][:, hw:] for j in idx]
    b_e = [-get("b_e", j) for j in idx]
    g_a = [_dot_tn(w_p[j], b_e[j]) for j in idx]
    g_d = [_dot_tn(jnp.concatenate([v_p[j], u_p[j]], axis=0),
                   jnp.concatenate([get("k_e", j), b_e[j]], axis=0)) for j in idx]
    s0 = [state[d, q] for d, q in items]
    ys = [_dot_nt(r_eff[j], bdiag(s0[j])) + y_loc[j] for j in idx]
    half = len(items) // 2
    y0_ref[...] = jnp.concatenate(ys[:half], axis=1)
    y1_ref[...] = jnp.concatenate(ys[half:], axis=1)
    s1 = [_dot(s0[j], jnp.where(same_head, g_a[j], 0.0)) for j in idx]
    for j, (d, q) in enumerate(items):
        state[d, q] = s0[j] * get("p_all", j) + s1[j] + pick(g_d[j][:n], g_d[j][n:])


def _rwkv_mirror(i, nc_ctx, nc_lat):
    return jnp.where(i < nc_ctx, nc_ctx - 1 - i, 2 * nc_ctx + nc_lat - 1 - i)


def _rwkv_scan(p, mu, vecs, wup, aup, gup, masks, bsz, n_ctx, n_lat):
    ltot = n_ctx + n_lat
    nc_ctx, nc_lat = n_ctx // CHUNK, n_lat // CHUNK
    nc = nc_ctx + nc_lat
    rows = bsz * ltot
    hb = CHUNK // 16
    n_hblk = rows // 16
    chunk_of = (lambda i: i, lambda i: _rwkv_mirror(i, nc_ctx, nc_lat))

    def specs(d):
        main = lambda b, i: (b * nc + chunk_of[d](i), 0)
        prev = lambda b, i: (jnp.maximum((b * nc + chunk_of[d](i)) * hb - 1, 0), 0)
        nxt = lambda b, i: (jnp.minimum((b * nc + chunk_of[d](i) + 1) * hb, n_hblk - 1), 0)
        return main, [pl.BlockSpec((CHUNK, RWKV_IN), main), pl.BlockSpec((16, RWKV_IN), prev),
                      pl.BlockSpec((16, RWKV_IN), nxt)]

    (main0, in0), (main1, in1) = specs(0), specs(1)
    whole = lambda shape: pl.BlockSpec(shape, lambda b, i: (0,) * len(shape))
    out_f = jax.ShapeDtypeStruct((rows, RWKV_W), F32)
    out_bf = jax.ShapeDtypeStruct((rows, RWKV_W), BF16)
    ospec = lambda m: pl.BlockSpec((CHUNK, RWKV_W), m)
    kern = functools.partial(_rwkv_kernel, nc_ctx=nc_ctx, nc_lat=nc_lat)
    return pl.pallas_call(
        kern,
        out_shape=(out_f, out_f, out_bf, out_bf, out_bf),
        grid=(bsz, nc),
        in_specs=in0 + in1 + [whole((2, RWKV_IN)), whole((2, 8, RWKV_W)),
                              whole((2, 2 * DECAY_LORA, RWKV_W)), whole((2, 2 * AAA_LORA, RWKV_W)),
                              whole((GATE_LORA, RWKV_W)), whole((2, 2, CHUNK, CHUNK))],
        out_specs=(ospec(main0), ospec(main1), ospec(main0), ospec(main1), ospec(main0)),
        scratch_shapes=[pltpu.VMEM((2, RWKV_HEADS // 2, RWKV_HEAD, 2 * RWKV_HEAD), F32)],
        compiler_params=_cparams(("arbitrary", "arbitrary")),
        name="rwkv_scan",
    )(p, p, p, p, p, p, mu, vecs, wup, aup, gup, masks)


def _diff_kernel(q_ref, k_ref, v_ref, par_ref, o_ref, *, n_ctx):
    t = pl.program_id(2)
    lam = par_ref[0:1, :]
    gain = par_ref[1:2, :]

    def attend(nk):
        q = q_ref[...]
        k = k_ref[0:nk, :]
        v = v_ref[0:nk, :]

        def unnormalised(sl):
            s = _dot_nt(q[:, sl], k[:, sl])
            e = jnp.exp2(s - jnp.max(s, axis=-1, keepdims=True))
            return e.astype(BF16), jnp.sum(e, axis=-1, keepdims=True)

        e0, l0 = unnormalised(slice(0, DIFF_DK))
        e1, l1 = unnormalised(slice(DIFF_DK, 2 * DIFF_DK))
        ov = _dot(jnp.concatenate([e0, e1], axis=0), v)
        nq = q.shape[0]
        o = ov[:nq] / l0 - lam * (ov[nq:] / l1)
        o = o * lax.rsqrt(jnp.mean(o * o, axis=-1, keepdims=True) + 1e-5)
        o_ref[...] = (o * gain).astype(o_ref.dtype)

    @pl.when(t * SEG < n_ctx)
    def _():
        attend(n_ctx)

    @pl.when(t * SEG >= n_ctx)
    def _():
        attend(k_ref.shape[0])


def _diff_attn(qk, p, par, bsz, n_ctx, ltot):
    rows = bsz * ltot
    nt = ltot // SEG
    voff = (P_DIFF + 2 * DIFF_W) // DIFF_DV
    return pl.pallas_call(
        functools.partial(_diff_kernel, n_ctx=n_ctx),
        out_shape=jax.ShapeDtypeStruct((rows, DIFF_W), BF16),
        grid=(bsz, DIFF_HEADS, nt),
        in_specs=[pl.BlockSpec((SEG, DIFF_DV), lambda b, h, t: (b * nt + t, h)),
                  pl.BlockSpec((ltot, DIFF_DV), lambda b, h, t: (b, DIFF_HEADS + h)),
                  pl.BlockSpec((ltot, DIFF_DV), lambda b, h, t: (b, voff + h)),
                  pl.BlockSpec((8, DIFF_DV), lambda b, h, t: (0, 0))],
        out_specs=pl.BlockSpec((SEG, DIFF_DV), lambda b, h, t: (b * nt + t, h)),
        compiler_params=_cparams(("arbitrary", "arbitrary", "arbitrary")),
        name="diff_attn",
    )(qk, qk, p, par)


def _win_kernel(sink_ref, q_ref, kp_ref, kc_ref, kn_ref, kx_ref, vp_ref, vc_ref, vn_ref, vx_ref, bias_ref,
                o_ref, *, n_ctx, n_lat):
    blk = pl.program_id(1)
    ncb = n_ctx // QBLK
    nq = WIN_GROUP * QBLK

    def run(keys, vals, bias):
        outs = []
        for g in range(WIN_KV_HEADS):
            ksl = slice(g * WIN_HEAD, (g + 1) * WIN_HEAD)
            qg = jnp.concatenate(
                [q_ref[:, (g * WIN_GROUP + j) * WIN_HEAD:(g * WIN_GROUP + j + 1) * WIN_HEAD]
                 for j in range(WIN_GROUP)], axis=0)
            s = _dot_nt(qg, keys[:, ksl])
            if bias is not None:
                s = s + bias
            hrow = lax.broadcasted_iota(jnp.int32, (nq, 1), 0) // QBLK
            sink = jnp.zeros((nq, 1), F32)
            for j in range(WIN_GROUP):
                sink = jnp.where(hrow == j, sink_ref[g * WIN_GROUP + j] * LOG2E, sink)
            m = jnp.maximum(jnp.max(s, axis=-1, keepdims=True), sink)
            e = jnp.exp2(s - m)
            den = jnp.sum(e, axis=-1, keepdims=True) + jnp.exp2(sink - m)
            og = _dot(e, vals[:, ksl]) / den
            outs.extend(og[j * QBLK:(j + 1) * QBLK] for j in range(WIN_GROUP))
        o_ref[...] = jnp.concatenate(outs, axis=1).astype(o_ref.dtype)

    @pl.when(blk < ncb)
    def _():
        run(kx_ref[...], vx_ref[...], None)

    @pl.when(blk >= ncb)
    def _():
        keys = jnp.concatenate([kp_ref[...], kc_ref[...], kn_ref[...], kx_ref[...]], axis=0)
        vals = jnp.concatenate([vp_ref[...], vc_ref[...], vn_ref[...], vx_ref[...]], axis=0)
        run(keys, vals, bias_ref[0])


def _win_bias(n_ctx, n_lat):
    nq, nk = WIN_GROUP * QBLK, 3 * QBLK + n_ctx
    assert n_lat >= 2 * QBLK
    r = (jnp.arange(nq) % QBLK)[:, None]
    col = jnp.arange(nk)[None, :]
    band = (jnp.abs(r - col + QBLK) <= WINDOW) & (col < 3 * QBLK)
    ctx = col >= 3 * QBLK
    valid = jnp.stack([(band & (col >= QBLK)) | ctx, band | ctx, (band & (col < 2 * QBLK)) | ctx])
    return jnp.where(valid, 0.0, NEG_INF).astype(F32)


def _win_attn(sink, qk, p, bsz, n_ctx, n_lat):
    ltot = n_ctx + n_lat
    rows = bsz * ltot
    nb = ltot // QBLK
    ncb = n_ctx // QBLK
    koff = WIN_W // WIN_KV_W
    voff = (P_WIN + WIN_W + WIN_KV_W) // WIN_KV_W

    def lat_blk(b, t, shift):
        j = jnp.clip(t - ncb + shift, 0, nb - ncb - 1)
        return b * nb + ncb + j

    kspec = lambda shift, c: pl.BlockSpec((QBLK, WIN_KV_W), lambda b, t, s: (lat_blk(b, t, shift), c))
    xspec = lambda c: pl.BlockSpec((n_ctx, WIN_KV_W), lambda b, t, s: (b * (ltot // n_ctx), c))
    variant = lambda t: jnp.where(t <= ncb, 0, jnp.where(t >= nb - 1, 2, 1))
    bspec = pl.BlockSpec((1, WIN_GROUP * QBLK, 3 * QBLK + n_ctx), lambda b, t, s: (variant(t), 0, 0))
    return pl.pallas_call(
        functools.partial(_win_kernel, n_ctx=n_ctx, n_lat=n_lat),
        out_shape=jax.ShapeDtypeStruct((rows, WIN_W), BF16),
        grid_spec=pltpu.PrefetchScalarGridSpec(
            num_scalar_prefetch=1,
            grid=(bsz, nb),
            in_specs=[pl.BlockSpec((QBLK, WIN_W), lambda b, t, s: (b * nb + t, 0)),
                      kspec(-1, koff), kspec(0, koff), kspec(1, koff), xspec(koff),
                      kspec(-1, voff), kspec(0, voff), kspec(1, voff), xspec(voff), bspec],
            out_specs=pl.BlockSpec((QBLK, WIN_W), lambda b, t, s: (b * nb + t, 0))),
        compiler_params=_cparams(("arbitrary", "arbitrary")),
        name="win_attn",
    )(sink, qk, qk, qk, qk, qk, p, p, p, p, _win_bias(n_ctx, n_lat))


def _merge_kernel(ya_ref, yb_ref, yc_ref, ga_ref, gb_ref, gc_ref, wa_ref, wb_ref, wc_ref, o_ref):
    z = _sigmoid(ga_ref[...].astype(F32)) * _dot(ya_ref[...], wa_ref[...])
    z = z + _sigmoid(gb_ref[...].astype(F32)) * _dot(yb_ref[...], wb_ref[...])
    z = z + _sigmoid(gc_ref[...].astype(F32)) * _dot(yc_ref[...], wc_ref[...])
    o_ref[...] = z.astype(o_ref.dtype)


def _merge(ya, yb, yc, p, w_branch, layer, tm, tn):
    m = ya.shape[0]
    assert RWKV_W == DIFF_W and (RWKV_W + DIFF_W) % WIN_W == 0
    nj = D_MODEL // tn
    goff = P_GATE // tn
    gspec = lambda br: pl.BlockSpec((tm, tn), lambda j, i: (i, goff + br * nj + j))
    return pl.pallas_call(
        _merge_kernel,
        out_shape=jax.ShapeDtypeStruct((m, D_MODEL), BF16),
        grid=(nj, m // tm),
        in_specs=[pl.BlockSpec((tm, RWKV_W), lambda j, i: (i, 0)),
                  pl.BlockSpec((tm, DIFF_W), lambda j, i: (i, 0)),
                  pl.BlockSpec((tm, WIN_W), lambda j, i: (i, 0)),
                  gspec(0), gspec(1), gspec(2),
                  pl.BlockSpec((None, RWKV_W, tn), lambda j, i: (layer, 0, j)),
                  pl.BlockSpec((None, DIFF_W, tn), lambda j, i: (layer, 1, j)),
                  pl.BlockSpec((None, WIN_W, tn), lambda j, i: (layer, (RWKV_W + DIFF_W) // WIN_W, j))],
        out_specs=pl.BlockSpec((tm, tn), lambda j, i: (i, j)),
        compiler_params=_cparams(("arbitrary", "arbitrary")),
        name="merge",
    )(ya, yb, yc, p, p, p, w_branch, w_branch, w_branch)


def _expert_kernel(be_ref, nu_ref, x_ref, gw_ref, w1_ref, w3_ref, w2_ref, o_ref, w1b, w3b, w2b):
    i = pl.program_id(0)
    prev = be_ref[jnp.maximum(i - 1, 0)]

    @pl.when(jnp.logical_or(i == 0, be_ref[i] != prev))
    def _():
        w1b[...] = w1_ref[...].astype(BF16)
        w3b[...] = w3_ref[...].astype(BF16)
        w2b[...] = w2_ref[...].astype(BF16)

    @pl.when(i < nu_ref[0])
    def _():
        x = x_ref[...]
        h1 = _dot(x, w1b[...])
        h = h1 * _sigmoid(h1) * _dot(x, w3b[...])
        o_ref[...] = (_dot(h, w2b[...]) * gw_ref[...]).astype(o_ref.dtype)

    @pl.when(i >= nu_ref[0])
    def _():
        o_ref[...] = jnp.zeros_like(o_ref)


def _experts(blk_expert, n_used, xb, gw, w1, w3, w2, layer):
    rows = xb.shape[0]
    nb = rows // MOE_BLK
    return pl.pallas_call(
        _expert_kernel,
        out_shape=jax.ShapeDtypeStruct((rows, D_MODEL), BF16),
        grid_spec=pltpu.PrefetchScalarGridSpec(
            num_scalar_prefetch=2,
            grid=(nb,),
            in_specs=[pl.BlockSpec((MOE_BLK, D_MODEL), lambda i, be, nu: (i, 0)),
                      pl.BlockSpec((MOE_BLK, 1), lambda i, be, nu: (i, 0)),
                      pl.BlockSpec((None, None, D_MODEL, D_EXPERT), lambda i, be, nu: (layer, be[i], 0, 0)),
                      pl.BlockSpec((None, None, D_MODEL, D_EXPERT), lambda i, be, nu: (layer, be[i], 0, 0)),
                      pl.BlockSpec((None, None, D_EXPERT, D_MODEL), lambda i, be, nu: (layer, be[i], 0, 0))],
            out_specs=pl.BlockSpec((MOE_BLK, D_MODEL), lambda i, be, nu: (i, 0)),
            scratch_shapes=[pltpu.VMEM((D_MODEL, D_EXPERT), BF16),
                            pltpu.VMEM((D_MODEL, D_EXPERT), BF16),
                            pltpu.VMEM((D_EXPERT, D_MODEL), BF16)]),
        compiler_params=_cparams(("arbitrary",)),
        name="experts",
    )(blk_expert, n_used, xb, gw, w1, w3, w2)


def _rope_tables(n_ctx, n_lat):
    rows = n_lat // GRID_W
    row = jnp.repeat(jnp.arange(rows), GRID_W).astype(F32)
    col = (jnp.arange(rows * GRID_W) % GRID_W).astype(F32)
    inv = ROPE_BASE ** (-jnp.arange(ROPE_AX_FREQS, dtype=F32) / ROPE_AX_FREQS)
    ang = jnp.stack([row[:, None] * inv, col[:, None] * inv], axis=1)
    cos, sin = jnp.cos(ang), jnp.sin(ang)
    cos4 = jnp.stack([cos, cos], axis=2).reshape(n_lat, 4 * ROPE_AX_FREQS)
    sin4 = jnp.stack([-sin, sin], axis=2).reshape(n_lat, 4 * ROPE_AX_FREQS)
    cos4 = jnp.concatenate([jnp.ones((n_ctx, 64), F32), cos4], axis=0)
    sin4 = jnp.concatenate([jnp.zeros((n_ctx, 64), F32), sin4], axis=0)
    return jnp.stack([jnp.tile(cos4, (1, LANE // 64)), jnp.tile(sin4, (1, LANE // 64))])


def _moe(route, v, w1, w3, w2, layer):
    t = v.shape[0]
    experts = route[:, :EXPERT_TOP_K].astype(jnp.int32)
    gates = route[:, EXPERT_TOP_K:2 * EXPERT_TOP_K]
    k = EXPERT_TOP_K
    a = t * k
    e_n = N_EXPERTS
    nb = -(-a // MOE_BLK) + e_n
    e_flat = experts.T.reshape(a)
    iota = jnp.arange(a, dtype=jnp.int32)
    e_s, order, g_s = lax.sort((e_flat, iota, gates.T.reshape(a)), num_keys=1, is_stable=True)
    ids = jnp.arange(e_n, dtype=jnp.int32)
    start = jnp.sum(e_s[None, :] < ids[:, None], axis=1, dtype=jnp.int32)
    counts = jnp.sum(e_s[None, :] == ids[:, None], axis=1, dtype=jnp.int32)
    padded = (counts + MOE_BLK - 1) // MOE_BLK * MOE_BLK
    pad_end = jnp.cumsum(padded)
    pad_start = pad_end - padded
    shift = jnp.sum(jnp.where(e_s[:, None] == ids[None, :], (pad_start - start)[None, :], 0), axis=1)
    pos = lax.sort((order, iota + shift), num_keys=1)[1]
    blk_first = jnp.arange(nb, dtype=jnp.int32) * MOE_BLK
    blk_expert = jnp.minimum(jnp.sum(pad_end[None, :] <= blk_first[:, None], axis=1, dtype=jnp.int32), e_n - 1)
    n_used = (pad_end[-1:] // MOE_BLK).astype(jnp.int32)
    rank = blk_first[:, None] + jnp.arange(MOE_BLK, dtype=jnp.int32)[None, :] - pad_start[blk_expert][:, None]
    filled = (rank < counts[blk_expert][:, None]).reshape(nb * MOE_BLK)
    sidx = jnp.clip(start[blk_expert][:, None] + rank, 0, a - 1).reshape(nb * MOE_BLK)
    spread = jnp.arange(nb * MOE_BLK, dtype=jnp.int32) % t
    slot_tok = jnp.where(filled, order.at[sidx].get(mode="promise_in_bounds") % t, spread)
    slot_gate = jnp.where(filled, g_s.at[sidx].get(mode="promise_in_bounds"), 0.0)
    xb = v.at[slot_tok].get(mode="promise_in_bounds")
    ys = _experts(blk_expert, n_used, xb, slot_gate[:, None], w1, w3, w2, layer)
    yg = lax.optimization_barrier(ys.at[pos].get(mode="promise_in_bounds"))
    return yg.reshape(k, t, D_MODEL)


def kernel(x, c, ctx, c_ctx, w_mod, b_mod, w_in, rwkv_mu, rwkv_w0, rwkv_w_up, rwkv_a0, rwkv_a_up,
           rwkv_g_up, rwkv_kvec, rwkv_lnx, diff_lam, diff_subln, win_sink, w_branch, w_out, ln_g, ln_b,
           w_rg, b_rg, w_re, b_re, w1, w3, w2):
    bsz, n_lat, dm = x.shape
    n_ctx = ctx.shape[1]
    depth = w_mod.shape[0]
    ltot = n_ctx + n_lat
    rows = bsz * ltot
    dn_alpha = (2 * depth) ** 0.25
    assert dm == D_MODEL and n_ctx % SEG == 0 and n_lat % SEG == 0 and ltot % n_ctx == 0

    nt = ltot // SEG
    cs = _rope_tables(n_ctx, n_lat)
    fwd = jnp.tril(jnp.ones((CHUNK, CHUNK), F32))
    masks = jnp.stack([jnp.stack([fwd - jnp.eye(CHUNK, dtype=F32), fwd]),
                       jnp.stack([fwd.T - jnp.eye(CHUNK, dtype=F32), fwd.T])])
    head_id = jnp.arange(RWKV_W) // RWKV_HEAD
    head_avg = ((head_id[:, None] == head_id[None, :]).astype(F32) / RWKV_HEAD).astype(BF16)

    xs = jnp.concatenate([ctx, x], axis=1).reshape(rows, dm)
    cvec = jnp.concatenate([c_ctx[None, :], c], axis=0)
    cpad = jnp.zeros((32, dm), F32).at[:bsz + 1].set(jax.nn.silu(cvec))
    w_in_p = jnp.concatenate([w_in[:, :, :RWKV_IN], jnp.zeros((depth, dm, P_GATE - RWKV_IN), F32),
                              w_in[:, :, GATE_OFF:], w_in[:, :, DIFF_OFF:GATE_OFF]], axis=2).astype(BF16)
    w_branch_b = w_branch.astype(BF16)
    w_out_b = w_out.astype(BF16)
    w_r = jnp.zeros((depth, dm, LANE), F32).at[:, :, :N_GROUPS].set(w_rg)
    w_r = w_r.at[:, :, N_GROUPS:N_GROUPS + N_EXPERTS].set(w_re)
    b_r = jnp.zeros((depth, 8, LANE), F32).at[:, :, :N_GROUPS].set(b_rg[:, None, :])
    b_r = b_r.at[:, :, N_GROUPS:N_GROUPS + N_EXPERTS].set(b_re[:, None, :])

    mods = [_mm(cpad, w_mod, i, F32, 32, 1024, "mod")[:bsz + 1] + b_mod[i] for i in range(depth)]

    def table(gate, shift, scale):
        def both(m, j):
            v = m[:, j * dm:(j + 1) * dm]
            return jnp.stack([jnp.broadcast_to(v[0], (bsz, dm)), v[1:]], axis=1)
        t = jnp.stack([both(*gate), both(*shift), both(*scale)], axis=2)
        return jnp.pad(t, ((0, 0), (0, 0), (0, 5), (0, 0))).reshape(2 * bsz, 8, dm)

    u = _modulate0(xs, table((mods[0], 0), (mods[0], 0), (mods[0], 1)), nt)
    for i in range(depth):
        last = i == depth - 1
        lam_init = 0.8 - 0.6 * math.exp(-0.3 * i)
        mod = mods[i]
        p = _mm(u, w_in_p, i, BF16, 512, 2048, "in_proj")

        vecs = jnp.zeros((2, 8, RWKV_W), F32)
        vecs = vecs.at[:, 0].set(rwkv_w0[i]).at[:, 1].set(rwkv_a0[i])
        vecs = vecs.at[:, 2:5].set(jnp.broadcast_to(rwkv_kvec[i][None], (2, 3, RWKV_W)))
        wup = jnp.zeros((2, 2 * DECAY_LORA, RWKV_W), BF16)
        aup = jnp.zeros((2, 2 * AAA_LORA, RWKV_W), BF16)
        for d in range(2):
            wup = wup.at[d, d * DECAY_LORA:(d + 1) * DECAY_LORA].set(rwkv_w_up[i, d].astype(BF16))
            aup = aup.at[d, d * AAA_LORA:(d + 1) * AAA_LORA].set(rwkv_a_up[i, d].astype(BF16))
        scan = _rwkv_scan(p, rwkv_mu[i], vecs, wup, aup, rwkv_g_up[i].astype(BF16), masks, bsz, n_ctx, n_lat)
        ya = _readout(*scan, rwkv_lnx[i], head_avg, 512)

        dqk, wqk = _rope(p, cs, nt)
        lf = diff_lam[i]
        lam = jnp.exp(jnp.sum(lf[0] * lf[1])) - jnp.exp(jnp.sum(lf[2] * lf[3])) + lam_init
        par = jnp.zeros((8, DIFF_DV), F32).at[0].set(lam).at[1].set(diff_subln[i] * (1 - lam_init))
        yb = _diff_attn(dqk, p, par, bsz, n_ctx, ltot)
        yc = _win_attn(win_sink[i], wqk, p, bsz, n_ctx, n_lat)
        z = _merge(ya, yb, yc, p, w_branch_b, i, 512, 1024)

        lnp = lambda j: jnp.zeros((8, dm), F32).at[0].set(ln_g[i, j]).at[1].set(ln_b[i, j])
        xs, v, route = _mix_out(z, xs, w_out_b, i, table((mod, 2), (mod, 3), (mod, 4)), lnp(0),
                                w_r[i], b_r[i], nt, dn_alpha)

        if last:
            lat = lambda t: t.reshape(bsz, ltot, -1)[:, n_ctx:].reshape(bsz * n_lat, -1)
            yg = _moe(lat(route), lat(v), w1, w3, w2, i)
            tab = table((mod, 5), (mod, 0), (mod, 1))
            (out,) = _moe_out(xs, yg, tab, lnp(1), bsz, nt, n_ctx // SEG, dn_alpha, False)
            return out.reshape(bsz, n_lat, dm)
        yg = _moe(route, v, w1, w3, w2, i)
        tab = table((mod, 5), (mods[i + 1], 0), (mods[i + 1], 1))
        xs, u = _moe_out(xs, yg, tab, lnp(1), bsz, nt, 0, dn_alpha, True)
    return None
```
